```python
import math
import jax, jax.numpy as jnp
from jax import lax
import numpy as np

D_MODEL = 2048
BATCH = 1
SEQ = 8192
DEPTH = 1

D_SSD = 2048
SSD_HEAD_DIM = 64
SSD_HEADS = D_SSD // SSD_HEAD_DIM
SSD_GROUPS = 4
SSD_STATE = 128
CONV_K = 5
CONV_CH = D_SSD + 2 * SSD_GROUPS * SSD_STATE
CHUNK = 128
ATT_HEADS = 16
ATT_HEAD_DIM = 64
ATT_V_DIM = 2 * ATT_HEAD_DIM
Q_COLS = 2 * ATT_HEADS * ATT_HEAD_DIM
V_COLS = ATT_HEADS * ATT_V_DIM
Q_BLOCK = 128
ROPE_THETA = 10000.0
N_BRANCHES = 2
IN_SPLITS = (D_SSD, CONV_CH, 2 * SSD_HEADS, Q_COLS, Q_COLS, V_COLS, N_BRANCHES * D_MODEL)
IN_COLS = sum(IN_SPLITS)
SPLIT_POINTS = [int(s) for s in np.cumsum(IN_SPLITS)[:-1]]
PEER_HEADS = 8
PEER_NKEYS = 128
PEER_EXPERTS = PEER_NKEYS * PEER_NKEYS
PEER_KEY_DIM = 256
PEER_HALF = PEER_KEY_DIM // 2
PEER_TOPK = 16
TOK_BLOCK = 128
PLE_DIM = 256
EPS = 1e-6

kernel_name = 'hybrid_ssd_diffattn_peer_block'


def rms_norm(x, g):
    xf = x.astype(jnp.float32)
    y = xf * lax.rsqrt(jnp.mean(xf * xf, axis=-1, keepdims=True) + EPS)
    return (y * g.astype(jnp.float32)).astype(x.dtype)


def rope(t, pos):
    d = t.shape[-1]
    inv = ROPE_THETA ** (-jnp.arange(0, d, 2, dtype=jnp.float32) / d)
    ang = pos.astype(jnp.float32)[..., None] * inv
    cos = jnp.cos(ang)[:, :, None, :]
    sin = jnp.sin(ang)[:, :, None, :]
    tf = t.astype(jnp.float32)
    t1, t2 = tf[..., : d // 2], tf[..., d // 2:]
    return jnp.concatenate([t1 * cos - t2 * sin, t2 * cos + t1 * sin], axis=-1).astype(t.dtype)


def centred_depthwise_conv(u, w, b):
    y = lax.conv_general_dilated(
        u, w[:, None, :].astype(u.dtype), window_strides=(1,),
        padding=[(CONV_K // 2, CONV_K // 2)],
        dimension_numbers=('NWC', 'WIO', 'NWC'),
        feature_group_count=u.shape[-1])
    return y + b.astype(u.dtype)


def ssd_scan(x, dt, A, Bm, Cm):
    b, S, H, P = x.shape
    G, N = Bm.shape[-2], Bm.shape[-1]
    R = H // G
    nc = S // CHUNK
    a = (dt * A).reshape(b, nc, CHUNK, G, R)
    xdt = (x.astype(jnp.float32) * dt[..., None]).reshape(b, nc, CHUNK, G, R, P)
    Bc = Bm.astype(jnp.float32).reshape(b, nc, CHUNK, G, N)
    Cc = Cm.astype(jnp.float32).reshape(b, nc, CHUNK, G, N)
    a_cs = jnp.cumsum(a, axis=2)
    seg = a_cs[:, :, :, None] - a_cs[:, :, None]
    tri = jnp.tril(jnp.ones((CHUNK, CHUNK), dtype=bool))[:, :, None, None]
    Lm = jnp.exp(jnp.where(tri, seg, -jnp.inf))
    cb = jnp.einsum('bclgn,bcsgn->bclsg', Cc, Bc)
    y_diag = jnp.einsum('bclsgr,bcsgrp->bclgrp', Lm * cb[..., None], xdt)
    decay_in = jnp.exp(a_cs[:, :, -1:] - a_cs)
    states = jnp.einsum('bclgn,bclgrp->bcgrpn', Bc, xdt * decay_in[..., None])
    chunk_decay = jnp.exp(a_cs[:, :, -1])

    def step(h, inp):
        st, dec = inp
        return h * dec[..., None, None] + st, h

    h0 = jnp.zeros((b, G, R, P, N), jnp.float32)
    _, h_in = lax.scan(step, h0, (jnp.moveaxis(states, 1, 0), jnp.moveaxis(chunk_decay, 1, 0)))
    h_in = jnp.moveaxis(h_in, 0, 1)
    y_off = jnp.einsum('bclgn,bcgrpn->bclgrp', Cc, h_in) * jnp.exp(a_cs)[..., None]
    return (y_diag + y_off).reshape(b, S, H, P)


def diff_attention(q, k, v, lam):
    b, S, H2, d = q.shape
    H = H2 // 2
    nb = S // Q_BLOCK
    scale = d ** -0.5
    qb = jnp.moveaxis(q.reshape(b, nb, Q_BLOCK, H2, d), 1, 0)

    def block(qi):
        s = jnp.einsum('bqhd,bkhd->bhqk', qi, k).astype(jnp.float32) * scale
        pr = jax.nn.softmax(s, axis=-1).reshape(b, H, 2, Q_BLOCK, S)
        att = pr[:, :, 0] - lam * pr[:, :, 1]
        return jnp.einsum('bhqk,bkhe->bqhe', att.astype(v.dtype), v)

    o = lax.map(block, qb)
    return jnp.moveaxis(o, 0, 1).reshape(b, S, H, v.shape[-1])


def peer(h, wq, keys, u, v):
    b, S, D = h.shape
    T = b * S
    ht = h.reshape(T, D)
    q = (ht @ wq.astype(ht.dtype)).reshape(T, PEER_HEADS, 2, PEER_HALF)
    s = jnp.einsum('thid,hind->thin', q, keys.astype(q.dtype)).astype(jnp.float32)
    v_top, i_top = lax.top_k(s, PEER_TOPK)
    cand = (v_top[:, :, 0, :, None] + v_top[:, :, 1, None, :]).reshape(T, PEER_HEADS, PEER_TOPK * PEER_TOPK)
    cidx = (i_top[:, :, 0, :, None] * PEER_NKEYS + i_top[:, :, 1, None, :]).reshape(T, PEER_HEADS, PEER_TOPK * PEER_TOPK)
    sc, pos = lax.top_k(cand, PEER_TOPK)
    eidx = jnp.take_along_axis(cidx, pos, axis=-1)
    gate = jax.nn.softmax(sc, axis=-1)
    nb = T // TOK_BLOCK

    def block(args):
        hb, ib, gb = args
        ub = jnp.take(u, ib, axis=0).astype(hb.dtype)
        a = jnp.einsum('td,thkd->thk', hb, ub).astype(jnp.float32)
        w = jax.nn.gelu(a, approximate=False) * gb
        vb = jnp.take(v, ib, axis=0).astype(hb.dtype)
        return jnp.einsum('thk,thkd->td', w.astype(hb.dtype), vb)

    out = lax.map(block, (ht.reshape(nb, TOK_BLOCK, D),
                          eidx.reshape(nb, TOK_BLOCK, PEER_HEADS, PEER_TOPK),
                          gate.reshape(nb, TOK_BLOCK, PEER_HEADS, PEER_TOPK)))
    return out.reshape(b, S, D)


def _normal(k, shape, scale):
    return jax.random.normal(k, shape, jnp.float32) * scale


def setup_inputs(seed: int = 0) -> dict:
    key = jax.random.key(seed)
    k = jax.random.split(key, 32)
    L = DEPTH
    x = _normal(k[0], (BATCH, SEQ, D_MODEL), 1.0)
    p = _normal(k[1], (DEPTH, BATCH, SEQ, PLE_DIM), 1.0)
    positions = jnp.broadcast_to(jnp.arange(SEQ, dtype=jnp.int32), (BATCH, SEQ))
    norm1_g = 1.0 + _normal(k[2], (L, D_MODEL), 0.02)
    w_in = _normal(k[3], (L, D_MODEL, IN_COLS), D_MODEL ** -0.5)
    conv_w = _normal(k[4], (L, CONV_K, CONV_CH), CONV_K ** -0.5)
    conv_b = _normal(k[5], (L, CONV_CH), 0.02)
    a_log_f = jnp.log(jax.random.uniform(k[6], (L, SSD_HEADS), jnp.float32, 1.0, 16.0))
    a_log_b = jnp.log(jax.random.uniform(k[7], (L, SSD_HEADS), jnp.float32, 1.0, 16.0))
    dt0_f = jnp.exp(jax.random.uniform(k[8], (L, SSD_HEADS), jnp.float32, math.log(1e-3), math.log(1e-1)))
    dt0_b = jnp.exp(jax.random.uniform(k[9], (L, SSD_HEADS), jnp.float32, math.log(1e-3), math.log(1e-1)))
    dt_bias_f = dt0_f + jnp.log(-jnp.expm1(-dt0_f))
    dt_bias_b = dt0_b + jnp.log(-jnp.expm1(-dt0_b))
    d_skip = 1.0 + _normal(k[10], (L, SSD_HEADS), 0.02)
    ssd_norm_g = 1.0 + _normal(k[11], (L, D_SSD), 0.02)
    q_norm_g = 1.0 + _normal(k[12], (L, ATT_HEAD_DIM), 0.02)
    k_norm_g = 1.0 + _normal(k[13], (L, ATT_HEAD_DIM), 0.02)
    lam_q1 = _normal(k[14], (L, ATT_HEAD_DIM), 0.1)
    lam_k1 = _normal(k[15], (L, ATT_HEAD_DIM), 0.1)
    lam_q2 = _normal(k[16], (L, ATT_HEAD_DIM), 0.1)
    lam_k2 = _normal(k[17], (L, ATT_HEAD_DIM), 0.1)
    subln_g = 1.0 + _normal(k[18], (L, ATT_V_DIM), 0.02)
    w_ssd_br = _normal(k[19], (L, D_SSD, D_MODEL), D_SSD ** -0.5)
    w_att_br = _normal(k[20], (L, V_COLS, D_MODEL), V_COLS ** -0.5)
    w_out = _normal(k[21], (L, D_MODEL, D_MODEL), D_MODEL ** -0.5)
    norm2_g = 1.0 + _normal(k[22], (L, D_MODEL), 0.02)
    peer_wq = _normal(k[23], (L, D_MODEL, PEER_HEADS * PEER_KEY_DIM), D_MODEL ** -0.5)
    peer_keys = _normal(k[24], (L, PEER_HEADS, 2, PEER_NKEYS, PEER_HALF), PEER_HALF ** -0.5)
    peer_u = _normal(k[25], (L, PEER_EXPERTS, D_MODEL), D_MODEL ** -0.5)
    peer_v = _normal(k[26], (L, PEER_EXPERTS, D_MODEL), PEER_HEADS ** -0.5)
    norm3_g = 1.0 + _normal(k[27], (L, D_MODEL), 0.02)
    ple_gate_w = _normal(k[28], (L, D_MODEL, D_MODEL), D_MODEL ** -0.5)
    ple_up_w = _normal(k[29], (L, PLE_DIM, D_MODEL), PLE_DIM ** -0.5)
    return {'x': x, 'p': p, 'positions': positions, 'norm1_g': norm1_g, 'w_in': w_in,
            'conv_w': conv_w, 'conv_b': conv_b, 'a_log_f': a_log_f, 'a_log_b': a_log_b,
            'dt_bias_f': dt_bias_f, 'dt_bias_b': dt_bias_b, 'd_skip': d_skip,
            'ssd_norm_g': ssd_norm_g, 'q_norm_g': q_norm_g, 'k_norm_g': k_norm_g,
            'lam_q1': lam_q1, 'lam_k1': lam_k1, 'lam_q2': lam_q2, 'lam_k2': lam_k2,
            'subln_g': subln_g, 'w_ssd_br': w_ssd_br, 'w_att_br': w_att_br, 'w_out': w_out,
            'norm2_g': norm2_g, 'peer_wq': peer_wq, 'peer_keys': peer_keys, 'peer_u': peer_u,
            'peer_v': peer_v, 'norm3_g': norm3_g, 'ple_gate_w': ple_gate_w, 'ple_up_w': ple_up_w}


def reference(x, p, positions, norm1_g, w_in, conv_w, conv_b, a_log_f, a_log_b,
              dt_bias_f, dt_bias_b, d_skip, ssd_norm_g, q_norm_g, k_norm_g,
              lam_q1, lam_k1, lam_q2, lam_k2, subln_g, w_ssd_br, w_att_br, w_out,
              norm2_g, peer_wq, peer_keys, peer_u, peer_v, norm3_g, ple_gate_w, ple_up_w):
    f32 = jnp.float32
    b, S, _ = x.shape

    def flip(t):
        return jnp.flip(t, axis=1)

    for i in range(DEPTH):
        h = rms_norm(x, norm1_g[i])
        proj = h @ w_in[i].astype(h.dtype)
        z, xbc, dt_raw, q, k, v, gates = jnp.split(proj, SPLIT_POINTS, axis=-1)

        xbc = jax.nn.silu(centred_depthwise_conv(xbc, conv_w[i], conv_b[i]))
        xs, bm, cm = jnp.split(xbc, [D_SSD, D_SSD + SSD_GROUPS * SSD_STATE], axis=-1)
        xs = xs.reshape(b, S, SSD_HEADS, SSD_HEAD_DIM)
        bm = bm.reshape(b, S, SSD_GROUPS, SSD_STATE)
        cm = cm.reshape(b, S, SSD_GROUPS, SSD_STATE)
        dt_raw = dt_raw.astype(f32)
        dt_f = jax.nn.softplus(dt_raw[..., :SSD_HEADS] + dt_bias_f[i].astype(f32))
        dt_b = jax.nn.softplus(dt_raw[..., SSD_HEADS:] + dt_bias_b[i].astype(f32))
        y_f = ssd_scan(xs, dt_f, -jnp.exp(a_log_f[i].astype(f32)), bm, cm)
        y_b = flip(ssd_scan(flip(xs), flip(dt_b), -jnp.exp(a_log_b[i].astype(f32)), flip(bm), flip(cm)))
        y_ssd = y_f + y_b + d_skip[i].astype(f32)[:, None] * xs.astype(f32)
        y_ssd = y_ssd.reshape(b, S, D_SSD) * jax.nn.silu(z.astype(f32))
        y_ssd = rms_norm(y_ssd.astype(x.dtype), ssd_norm_g[i])

        q = rope(rms_norm(q.reshape(b, S, 2 * ATT_HEADS, ATT_HEAD_DIM), q_norm_g[i]), positions)
        k = rope(rms_norm(k.reshape(b, S, 2 * ATT_HEADS, ATT_HEAD_DIM), k_norm_g[i]), positions)
        v = v.reshape(b, S, ATT_HEADS, ATT_V_DIM)
        lam_init = 0.8 - 0.6 * math.exp(-0.3 * i)
        lam = (jnp.exp(jnp.sum(lam_q1[i].astype(f32) * lam_k1[i].astype(f32)))
               - jnp.exp(jnp.sum(lam_q2[i].astype(f32) * lam_k2[i].astype(f32))) + lam_init)
        o = diff_attention(q, k, v, lam)
        o = rms_norm(o, subln_g[i]) * (1.0 - lam_init)
        y_att = o.reshape(b, S, V_COLS)

        g_ssd, g_att = jnp.split(jax.nn.sigmoid(gates.astype(f32)), N_BRANCHES, axis=-1)
        mixed = (g_ssd * (y_ssd @ w_ssd_br[i].astype(y_ssd.dtype)).astype(f32)
                 + g_att * (y_att @ w_att_br[i].astype(y_att.dtype)).astype(f32))
        x = x + (mixed.astype(x.dtype) @ w_out[i].astype(x.dtype))

        x = x + peer(rms_norm(x, norm2_g[i]), peer_wq[i], peer_keys[i], peer_u[i], peer_v[i])

        ple_gate = jax.nn.sigmoid((rms_norm(x, norm3_g[i]) @ ple_gate_w[i].astype(x.dtype)).astype(f32))
        ple = (p[i] @ ple_up_w[i].astype(p.dtype)).astype(f32)
        x = x + (ple_gate * ple).astype(x.dtype)
    return x
```

```python
import functools
import math

import jax
import jax.numpy as jnp
from jax import lax
from jax.experimental import pallas as pl
from jax.experimental.pallas import tpu as pltpu

f32 = jnp.float32
bf16 = jnp.bfloat16

D_MODEL = 2048
D_SSD = 2048
SSD_HEAD_DIM = 64
SSD_HEADS = 32
SSD_GROUPS = 4
SSD_STATE = 128
GROUP_COLS = D_SSD // SSD_GROUPS
CONV_K = 5
CONV_CH = 3072
CHUNK = 128
ATT_HEADS = 16
ATT_HEAD_DIM = 64
ATT_V_DIM = 128
ROPE_THETA = 10000.0
PEER_HEADS = 8
PEER_NKEYS = 128
PEER_EXPERTS = PEER_NKEYS * PEER_NKEYS
PEER_TOPK = 16
PLE_DIM = 256
EPS = 1e-6

ZX_COLS = D_SSD + CONV_CH
DT_COL0 = ZX_COLS
REST_COL0 = ZX_COLS + 2 * SSD_HEADS
LANES = 128
BF16_ROWS = 16

VMEM_LIMIT = 56 * 1024 * 1024

NT = (((1,), (1,)), ((), ()))
TN = (((0,), (0,)), ((), ()))


def _params(*sem):
    return pltpu.CompilerParams(dimension_semantics=sem, vmem_limit_bytes=VMEM_LIMIT)


def _sigmoid(x):
    return 1.0 / (1.0 + jnp.exp(-x))


def _softplus(x):
    return jnp.maximum(x, 0.0) + jnp.log1p(jnp.exp(-jnp.abs(x)))


def _norm_kernel(x_ref, g_ref, h_ref):
    x = x_ref[...]
    ms = jnp.mean(x * x, axis=-1, keepdims=True)
    h_ref[...] = (x * lax.rsqrt(ms + EPS) * g_ref[...]).astype(h_ref.dtype)


def _rms_norm(x, g, tb=512):
    s, d = x.shape
    tb = min(tb, s)
    return pl.pallas_call(
        _norm_kernel,
        grid=(s // tb,),
        in_specs=[pl.BlockSpec((tb, d), lambda i: (i, 0)),
                  pl.BlockSpec((1, d), lambda i: (0, 0))],
        out_specs=pl.BlockSpec((tb, d), lambda i: (i, 0)),
        out_shape=jax.ShapeDtypeStruct((s, d), bf16),
        compiler_params=_params("parallel"),
        name="rms_norm",
    )(x, g.reshape(1, d))


def _add_norm_kernel(x_ref, d_ref, g_ref, xo_ref, h_ref):
    x = x_ref[...] + d_ref[...]
    xo_ref[...] = x
    ms = jnp.mean(x * x, axis=-1, keepdims=True)
    h_ref[...] = (x * lax.rsqrt(ms + EPS) * g_ref[...]).astype(h_ref.dtype)


def _add_rms_norm(x, delta, g, tb=512):
    s, d = x.shape
    tb = min(tb, s)
    row = pl.BlockSpec((tb, d), lambda i: (i, 0))
    return pl.pallas_call(
        _add_norm_kernel,
        grid=(s // tb,),
        in_specs=[row, row, pl.BlockSpec((1, d), lambda i: (0, 0))],
        out_specs=[row, row],
        out_shape=[jax.ShapeDtypeStruct((s, d), f32), jax.ShapeDtypeStruct((s, d), bf16)],
        compiler_params=_params("parallel"),
        name="add_rms_norm",
    )(x, delta, g.reshape(1, d))


def _mm_kernel(h_ref, w_ref, o_ref):
    o_ref[...] = jnp.dot(h_ref[...], w_ref[...].astype(bf16),
                         preferred_element_type=f32).astype(o_ref.dtype)


def _mm_res_kernel(h_ref, w_ref, r_ref, o_ref):
    o_ref[...] = r_ref[...] + jnp.dot(h_ref[...], w_ref[...].astype(bf16),
                                      preferred_element_type=f32)


def _matmul(h, w, col0, n, out_dtype, residual=None, tm=1024, tn=512):
    s, k = h.shape
    tm = min(tm, s)
    assert col0 % tn == 0 and n % tn == 0 and s % tm == 0
    cb = col0 // tn
    in_specs = [pl.BlockSpec((tm, k), lambda i, j: (i, 0)),
                pl.BlockSpec((k, tn), lambda i, j: (0, j + cb))]
    args = [h, w]
    kern = _mm_kernel
    if residual is not None:
        in_specs.append(pl.BlockSpec((tm, tn), lambda i, j: (i, j)))
        args.append(residual)
        kern = _mm_res_kernel
    return pl.pallas_call(
        kern,
        grid=(s // tm, n // tn),
        in_specs=in_specs,
        out_specs=pl.BlockSpec((tm, tn), lambda i, j: (i, j)),
        out_shape=jax.ShapeDtypeStruct((s, n), out_dtype),
        compiler_params=_params("parallel", "arbitrary"),
        name="matmul",
    )(*args)


def _dt_kernel(h_ref, w_ref, wt_ref, dt_ref, dtt_ref):
    h = h_ref[...]
    dt_ref[...] = jnp.dot(h, w_ref[...].astype(bf16), preferred_element_type=f32)
    dtt_ref[...] = lax.dot_general(wt_ref[...].astype(bf16), h, NT, preferred_element_type=f32)


def _dt_proj(h, w_dt, tm=1024):
    s, k = h.shape
    tm = min(tm, s)
    n = w_dt.shape[1]
    return pl.pallas_call(
        _dt_kernel,
        grid=(s // tm,),
        in_specs=[pl.BlockSpec((tm, k), lambda i: (i, 0)),
                  pl.BlockSpec((k, n), lambda i: (0, 0)),
                  pl.BlockSpec((n, k), lambda i: (0, 0))],
        out_specs=[pl.BlockSpec((tm, n), lambda i: (i, 0)),
                   pl.BlockSpec((n, tm), lambda i: (0, i))],
        out_shape=[jax.ShapeDtypeStruct((s, n), f32), jax.ShapeDtypeStruct((n, s), f32)],
        compiler_params=_params("parallel"),
        name="dt_proj",
    )(h, w_dt, w_dt.T)


def _conv_kernel(prev_ref, main_ref, next_ref, w_ref, b_ref, o_ref):
    i = pl.program_id(0)
    last = pl.num_programs(0) - 1
    tb = main_ref.shape[0]
    halo = prev_ref.shape[0]
    pv = jnp.where(i > 0, prev_ref[...].astype(f32), 0.0)
    nx = jnp.where(i < last, next_ref[...].astype(f32), 0.0)
    ext = jnp.concatenate([pv, main_ref[...].astype(f32), nx], axis=0)
    acc = b_ref[...] + jnp.zeros((tb, main_ref.shape[1]), f32)
    for k in range(CONV_K):
        off = halo + k - CONV_K // 2
        acc = acc + w_ref[k:k + 1, :] * ext[off:off + tb]
    o_ref[...] = (acc * _sigmoid(acc)).astype(o_ref.dtype)


def _conv_silu(zx, conv_w, conv_b, tb=512, tc=1024):
    s = zx.shape[0]
    tb = min(tb, s)
    halo = BF16_ROWS
    rb = tb // halo
    nhalo = s // halo
    cb = D_SSD // tc
    return pl.pallas_call(
        _conv_kernel,
        grid=(s // tb, CONV_CH // tc),
        in_specs=[
            pl.BlockSpec((halo, tc), lambda i, j: (jnp.maximum(i * rb - 1, 0), j + cb)),
            pl.BlockSpec((tb, tc), lambda i, j: (i, j + cb)),
            pl.BlockSpec((halo, tc), lambda i, j: (jnp.minimum((i + 1) * rb, nhalo - 1), j + cb)),
            pl.BlockSpec((CONV_K, tc), lambda i, j: (0, j)),
            pl.BlockSpec((1, tc), lambda i, j: (0, j)),
        ],
        out_specs=pl.BlockSpec((tb, tc), lambda i, j: (i, j)),
        out_shape=jax.ShapeDtypeStruct((s, CONV_CH), bf16),
        compiler_params=_params("parallel", "parallel"),
        name="conv_silu",
    )(zx, zx, zx, conv_w, conv_b.reshape(1, CONV_CH))


def _ssd_direction(xs, bm, cm, dt_raw, dtt_raw, bias, bias_t, a_log, a_log_t, expand,
                   state_ref, reverse):
    L = CHUNK
    dt = _softplus(dt_raw + bias)
    dtt = _softplus(dtt_raw + bias_t)
    a = dt * (-jnp.exp(a_log))
    at = dtt * (-jnp.exp(a_log_t))
    ri = lax.broadcasted_iota(jnp.int32, (L, L), 0)
    ci = lax.broadcasted_iota(jnp.int32, (L, L), 1)
    causal = (ri <= ci) if reverse else (ri >= ci)
    cum_l = jnp.where(causal, 1.0, 0.0).astype(f32)
    cum_r = jnp.where((ri >= ci) if reverse else (ri <= ci), 1.0, 0.0).astype(f32)
    acs = jnp.dot(cum_l, a, preferred_element_type=f32, precision=lax.Precision.HIGHEST)
    acst = jnp.dot(at, cum_r, preferred_element_type=f32, precision=lax.Precision.HIGHEST)
    edge = 0 if reverse else L - 1
    acs_end = acs[edge:edge + 1, :]
    small = jnp.concatenate([dt, jnp.exp(acs_end - acs), jnp.exp(acs)], axis=0)
    wide = jnp.dot(small, expand, preferred_element_type=f32, precision=lax.Precision.HIGHEST)
    dtx, dinx, eacsx = wide[:L], wide[L:2 * L], wide[2 * L:]
    xdt = xs * dtx
    xdt_b = xdt.astype(bf16)
    xdec_b = (xdt * dinx).astype(bf16)
    cdec = eacsx[edge:edge + 1, :]
    lane = lax.broadcasted_iota(jnp.int32, (L, LANES), 1)
    ys = []
    for g in range(SSD_GROUPS):
        bg = bm[:, g * SSD_STATE:(g + 1) * SSD_STATE]
        cg = cm[:, g * SSD_STATE:(g + 1) * SSD_STATE]
        gs = slice(g * GROUP_COLS, (g + 1) * GROUP_COLS)
        cb = lax.dot_general(cg, bg, NT, preferred_element_type=f32)
        h_in = state_ref[:, gs]
        y_off = jnp.dot(cg, h_in.astype(bf16), preferred_element_type=f32) * eacsx[:, gs]
        st = lax.dot_general(bg, xdec_b[:, gs], TN, preferred_element_type=f32)
        state_ref[:, gs] = h_in * cdec[:, gs] + st
        tiles = []
        for pair in range(GROUP_COLS // LANES):
            ms = []
            for sub in range(2):
                h = g * (SSD_HEADS // SSD_GROUPS) + pair * 2 + sub
                seg = acs[:, h:h + 1] - acst[h:h + 1, :]
                ms.append((jnp.where(causal, jnp.exp(seg), 0.0) * cb).astype(bf16))
            col = g * GROUP_COLS + pair * LANES
            xp = xdt_b[:, col:col + LANES]
            rhs = jnp.concatenate([jnp.where(lane < SSD_HEAD_DIM, xp, jnp.zeros_like(xp)),
                                   jnp.where(lane >= SSD_HEAD_DIM, xp, jnp.zeros_like(xp))], axis=0)
            tiles.append(jnp.dot(jnp.concatenate(ms, axis=1), rhs, preferred_element_type=f32))
        ys.append(jnp.concatenate(tiles, axis=1) + y_off)
    return jnp.concatenate(ys, axis=1)


def _ssd_kernel(xf_ref, bf_ref, cf_ref, dtf_ref, dttf_ref,
                xb_ref, bb_ref, cb_ref, dtb_ref, dttb_ref,
                bias_ref, biast_ref, alog_ref, alogt_ref, expand_ref,
                yf_ref, yb_ref, sf_ref, sb_ref):
    @pl.when(pl.program_id(0) == 0)
    def _():
        sf_ref[...] = jnp.zeros_like(sf_ref)
        sb_ref[...] = jnp.zeros_like(sb_ref)

    H = SSD_HEADS
    expand = expand_ref[...]
    yf_ref[...] = _ssd_direction(
        xf_ref[...].astype(f32), bf_ref[...], cf_ref[...],
        dtf_ref[:, :H], dttf_ref[:H, :], bias_ref[:, :H], biast_ref[:H, :],
        alog_ref[:, :H], alogt_ref[:H, :], expand, sf_ref, reverse=False)
    yb_ref[...] = _ssd_direction(
        xb_ref[...].astype(f32), bb_ref[...], cb_ref[...],
        dtb_ref[:, H:], dttb_ref[H:, :], bias_ref[:, H:], biast_ref[H:, :],
        alog_ref[:, H:], alogt_ref[H:, :], expand, sb_ref, reverse=True)


def _ssd_scan(xbc, dt, dtt, dt_bias, a_log):
    s = xbc.shape[0]
    nc = s // CHUNK
    gn = SSD_GROUPS * SSD_STATE
    bcol = D_SSD // gn
    fwd = lambda c: c
    bwd = lambda c: nc - 1 - c

    def chunk_specs(sel):
        return [pl.BlockSpec((CHUNK, D_SSD), lambda c: (sel(c), 0)),
                pl.BlockSpec((CHUNK, gn), lambda c: (sel(c), bcol)),
                pl.BlockSpec((CHUNK, gn), lambda c: (sel(c), bcol + 1)),
                pl.BlockSpec((CHUNK, 2 * SSD_HEADS), lambda c: (sel(c), 0)),
                pl.BlockSpec((2 * SSD_HEADS, CHUNK), lambda c: (0, sel(c)))]

    const = lambda shape: pl.BlockSpec(shape, lambda c: (0, 0))
    expand = (jnp.arange(D_SSD)[None, :] // SSD_HEAD_DIM == jnp.arange(SSD_HEADS)[:, None]).astype(f32)
    return pl.pallas_call(
        _ssd_kernel,
        grid=(nc,),
        in_specs=chunk_specs(fwd) + chunk_specs(bwd) + [
            const((1, 2 * SSD_HEADS)), const((2 * SSD_HEADS, 1)),
            const((1, 2 * SSD_HEADS)), const((2 * SSD_HEADS, 1)),
            const((SSD_HEADS, D_SSD))],
        out_specs=[pl.BlockSpec((CHUNK, D_SSD), lambda c: (fwd(c), 0)),
                   pl.BlockSpec((CHUNK, D_SSD), lambda c: (bwd(c), 0))],
        out_shape=[jax.ShapeDtypeStruct((s, D_SSD), f32)] * 2,
        scratch_shapes=[pltpu.VMEM((SSD_STATE, D_SSD), f32)] * 2,
        compiler_params=_params("arbitrary"),
        name="ssd_scan",
    )(xbc, xbc, xbc, dt, dtt, xbc, xbc, xbc, dt, dtt,
      dt_bias.reshape(1, -1), dt_bias.reshape(-1, 1), a_log.reshape(1, -1), a_log.reshape(-1, 1),
      expand)


def _ssd_post_kernel(yf_ref, yb_ref, xs_ref, z_ref, d_ref, g_ref, o_ref):
    z = z_ref[...].astype(f32)
    y = (yf_ref[...] + yb_ref[...] + d_ref[...] * xs_ref[...].astype(f32)) * (z * _sigmoid(z))
    ms = jnp.mean(y * y, axis=-1, keepdims=True)
    o_ref[...] = (y * lax.rsqrt(ms + EPS) * g_ref[...]).astype(o_ref.dtype)


def _ssd_post(y_f, y_b, xbc, zx, d_skip, g, tb=512):
    s = y_f.shape[0]
    tb = min(tb, s)
    row = pl.BlockSpec((tb, D_SSD), lambda i: (i, 0))
    vec = pl.BlockSpec((1, D_SSD), lambda i: (0, 0))
    return pl.pallas_call(
        _ssd_post_kernel,
        grid=(s // tb,),
        in_specs=[row, row, row, row, vec, vec],
        out_specs=row,
        out_shape=jax.ShapeDtypeStruct((s, D_SSD), bf16),
        compiler_params=_params("parallel"),
        name="ssd_post",
    )(y_f, y_b, xbc, zx, jnp.repeat(d_skip, SSD_HEAD_DIM).reshape(1, D_SSD), g.reshape(1, D_SSD))


def _rope_table_kernel(pos_ref, inv_ref, sign_ref, cos_ref, sin_ref):
    ang = pos_ref[...].astype(f32) * inv_ref[...]
    cos_ref[...] = jnp.cos(ang)
    sin_ref[...] = jnp.sin(ang) * sign_ref[...]


def _rope_tables(positions, tb=512):
    s = positions.shape[0]
    tb = min(tb, s)
    half = ATT_HEAD_DIM // 2
    inv = ROPE_THETA ** (-jnp.arange(0, ATT_HEAD_DIM, 2, dtype=f32) / ATT_HEAD_DIM)
    inv_t = jnp.tile(inv, LANES // half).reshape(1, LANES)
    sign = jnp.tile(jnp.concatenate([-jnp.ones((half,), f32), jnp.ones((half,), f32)]),
                    LANES // ATT_HEAD_DIM).reshape(1, LANES)
    vec = pl.BlockSpec((1, LANES), lambda i: (0, 0))
    row = pl.BlockSpec((tb, LANES), lambda i: (i, 0))
    return pl.pallas_call(
        _rope_table_kernel,
        grid=(s // tb,),
        in_specs=[pl.BlockSpec((tb, 1), lambda i: (i, 0)), vec, vec],
        out_specs=[row, row],
        out_shape=[jax.ShapeDtypeStruct((s, LANES), f32)] * 2,
        compiler_params=_params("parallel"),
        name="rope_tables",
    )(positions.reshape(s, 1), inv_t, sign)


def _qk_kernel(x_ref, g_ref, cos_ref, sin_ref, o_ref):
    which = pl.program_id(1)
    scale = jnp.where(which == 0, ATT_HEAD_DIM ** -0.5, 1.0).astype(f32)
    g = g_ref[0] * scale
    cos = cos_ref[...]
    sin = sin_ref[...]
    r = lax.broadcasted_iota(jnp.int32, (LANES, LANES), 0) // ATT_HEAD_DIM
    c = lax.broadcasted_iota(jnp.int32, (LANES, LANES), 1) // ATT_HEAD_DIM
    seg = jnp.where(r == c, 1.0, 0.0).astype(bf16)
    lane = lax.broadcasted_iota(jnp.int32, cos.shape, 1)
    first_half = (lane % ATT_HEAD_DIM) < ATT_HEAD_DIM // 2
    for t in range(x_ref.shape[1] // LANES):
        x = x_ref[:, t * LANES:(t + 1) * LANES].astype(f32)
        sq = x * x
        hi = sq.astype(bf16)
        lo = (sq - hi.astype(f32)).astype(bf16)
        ss = (jnp.dot(hi, seg, preferred_element_type=f32)
              + jnp.dot(lo, seg, preferred_element_type=f32))
        xn = x * lax.rsqrt(ss * (1.0 / ATT_HEAD_DIM) + EPS) * g
        swapped = jnp.where(first_half,
                            pltpu.roll(xn, LANES - ATT_HEAD_DIM // 2, axis=1),
                            pltpu.roll(xn, ATT_HEAD_DIM // 2, axis=1))
        o_ref[:, t * LANES:(t + 1) * LANES] = (xn * cos + swapped * sin).astype(o_ref.dtype)


def _qk_prep(qkvg, q_g, k_g, cos_t, sin_t, tb=512):
    s = qkvg.shape[0]
    tb = min(tb, s)
    cols = 2 * ATT_HEADS * ATT_HEAD_DIM
    g2 = jnp.stack([jnp.tile(q_g, 2), jnp.tile(k_g, 2)]).reshape(2, 1, LANES)
    tab = pl.BlockSpec((tb, LANES), lambda i, j: (i, 0))
    return pl.pallas_call(
        _qk_kernel,
        grid=(s // tb, 2),
        in_specs=[pl.BlockSpec((tb, cols), lambda i, j: (i, j)),
                  pl.BlockSpec((1, 1, LANES), lambda i, j: (j, 0, 0)),
                  tab, tab],
        out_specs=pl.BlockSpec((tb, cols), lambda i, j: (i, j)),
        out_shape=jax.ShapeDtypeStruct((s, 2 * cols), bf16),
        compiler_params=_params("parallel", "parallel"),
        name="qk_prep",
    )(qkvg, g2, cos_t, sin_t)


def _attn_kernel(q_ref, k_ref, v_ref, lq1_ref, lk1_ref, lq2_ref, lk2_ref, g_ref, o_ref,
                 s_ref, *, tk, lam_init):
    tq = q_ref.shape[0]
    nk = k_ref.shape[0] // tk
    lam = (jnp.exp(jnp.sum(lq1_ref[...] * lk1_ref[...], axis=-1, keepdims=True))
           - jnp.exp(jnp.sum(lq2_ref[...] * lk2_ref[...], axis=-1, keepdims=True)) + lam_init)
    q = q_ref[...]
    lane = lax.broadcasted_iota(jnp.int32, q.shape, 1)
    outs = []
    for half in range(2):
        in_half = (lane < ATT_HEAD_DIM) if half == 0 else (lane >= ATT_HEAD_DIM)
        qm = jnp.where(in_half, q, jnp.zeros_like(q))

        def scores(c, mx):
            kc = k_ref[pl.ds(pl.multiple_of(c * tk, tk), tk), :]
            sc = lax.dot_general(qm, kc, NT, preferred_element_type=f32)
            s_ref[c] = sc
            for t in range(tk // LANES):
                mx = jnp.maximum(mx, sc[:, t * LANES:(t + 1) * LANES])
            return mx

        mx = lax.fori_loop(0, nk, scores, jnp.full((tq, LANES), -jnp.inf, f32))
        m = jnp.max(mx, axis=-1, keepdims=True)

        def accum(c, carry):
            ls, acc = carry
            p = jnp.exp(s_ref[c] - m)
            for t in range(tk // LANES):
                ls = ls + p[:, t * LANES:(t + 1) * LANES]
            vc = v_ref[pl.ds(pl.multiple_of(c * tk, tk), tk), :]
            return ls, acc + jnp.dot(p.astype(bf16), vc, preferred_element_type=f32)

        ls, acc = lax.fori_loop(0, nk, accum, (jnp.zeros((tq, LANES), f32),
                                               jnp.zeros((tq, ATT_V_DIM), f32)))
        outs.append(acc / jnp.sum(ls, axis=-1, keepdims=True))
    o = outs[0] - lam * outs[1]
    ms = jnp.mean(o * o, axis=-1, keepdims=True)
    o_ref[...] = (o * lax.rsqrt(ms + EPS) * (g_ref[...] * (1.0 - lam_init))).astype(o_ref.dtype)


def _diff_attention(qk, qkvg, lam_q1, lam_k1, lam_q2, lam_k2, subln_g, lam_init, tq=512, tk=512):
    s = qk.shape[0]
    tq = min(tq, s)
    tk = min(tk, s)
    kcol = 2 * ATT_HEADS * ATT_HEAD_DIM // LANES
    vcol = 2 * kcol
    vec = lambda n: pl.BlockSpec((1, n), lambda h, i: (0, 0))
    return pl.pallas_call(
        functools.partial(_attn_kernel, tk=tk, lam_init=lam_init),
        grid=(ATT_HEADS, s // tq),
        in_specs=[pl.BlockSpec((tq, LANES), lambda h, i: (i, h)),
                  pl.BlockSpec((s, LANES), lambda h, i: (0, kcol + h)),
                  pl.BlockSpec((s, LANES), lambda h, i: (0, vcol + h)),
                  vec(ATT_HEAD_DIM), vec(ATT_HEAD_DIM), vec(ATT_HEAD_DIM), vec(ATT_HEAD_DIM),
                  vec(ATT_V_DIM)],
        out_specs=pl.BlockSpec((tq, LANES), lambda h, i: (i, h)),
        out_shape=jax.ShapeDtypeStruct((s, ATT_HEADS * ATT_V_DIM), bf16),
        scratch_shapes=[pltpu.VMEM((s // tk, tq, tk), f32)],
        compiler_params=_params("parallel", "arbitrary"),
        name="diff_attention",
    )(qk, qk, qkvg, lam_q1.reshape(1, -1), lam_k1.reshape(1, -1), lam_q2.reshape(1, -1),
      lam_k2.reshape(1, -1), subln_g.reshape(1, -1))


def _merge_kernel(ys_ref, ya_ref, wa_ref, wb_ref, gs_ref, ga_ref, o_ref):
    a = jnp.dot(ys_ref[...], wa_ref[...].astype(bf16), preferred_element_type=f32)
    b = jnp.dot(ya_ref[...], wb_ref[...].astype(bf16), preferred_element_type=f32)
    mixed = _sigmoid(gs_ref[...].astype(f32)) * a + _sigmoid(ga_ref[...].astype(f32)) * b
    o_ref[...] = mixed.astype(o_ref.dtype)


def _merge(y_ssd, y_att, w_a, w_b, qkvg, tm=1024, tn=512):
    s = y_ssd.shape[0]
    tm = min(tm, s)
    gcol = 3 * 2 * ATT_HEADS * ATT_HEAD_DIM // tn
    nj = D_MODEL // tn
    row = pl.BlockSpec((tm, D_MODEL), lambda i, j: (i, 0))
    wcol = pl.BlockSpec((D_MODEL, tn), lambda i, j: (0, j))
    return pl.pallas_call(
        _merge_kernel,
        grid=(s // tm, nj),
        in_specs=[row, row, wcol, wcol,
                  pl.BlockSpec((tm, tn), lambda i, j: (i, gcol + j)),
                  pl.BlockSpec((tm, tn), lambda i, j: (i, gcol + nj + j))],
        out_specs=pl.BlockSpec((tm, tn), lambda i, j: (i, j)),
        out_shape=jax.ShapeDtypeStruct((s, D_MODEL), bf16),
        compiler_params=_params("parallel", "arbitrary"),
        name="merge",
    )(y_ssd, y_att, w_a, w_b, qkvg, qkvg)


def _take_topk(s, k, payloads):
    rows = s.shape[0]
    riota = lax.broadcasted_iota(jnp.int32, s.shape, 0)
    vals, picked = [], [[] for _ in payloads]
    for _ in range(k):
        m = jnp.max(s, axis=0, keepdims=True)
        pos = jnp.min(jnp.where(s == m, riota, rows), axis=0, keepdims=True)
        hit = riota == pos
        vals.append(m)
        for out, pay in zip(picked, payloads):
            out.append(jnp.max(jnp.where(hit, pay, -1.0), axis=0, keepdims=True))
        s = jnp.where(hit, -jnp.inf, s)
    return jnp.concatenate(vals, axis=0), [jnp.concatenate(p, axis=0) for p in picked]


def _topk_kernel(q_ref, keys_ref, i1_ref, i2_ref, gate_ref):
    tb = q_ref.shape[0]
    kk = PEER_TOPK
    key_iota = lax.broadcasted_iota(jnp.int32, (PEER_NKEYS, tb), 0).astype(f32)
    i1s, i2s, gates = [], [], []
    for h in range(PEER_HEADS):
        tops = []
        for half in range(2):
            c0 = (h * 2 + half) * PEER_NKEYS
            qh = q_ref[:, c0:c0 + PEER_NKEYS].astype(bf16)
            keys = keys_ref[h, half].astype(bf16)
            s = lax.dot_general(keys, qh, NT, preferred_element_type=f32)
            tops.append(_take_topk(s, kk, [key_iota]))
        (v1, (i1,)), (v2, (i2,)) = tops
        cand = jnp.concatenate([v1[a:a + 1] + v2 for a in range(kk)], axis=0)
        c1 = jnp.concatenate([jnp.broadcast_to(i1[a:a + 1], (kk, tb)) for a in range(kk)], axis=0)
        c2 = jnp.concatenate([i2] * kk, axis=0)
        sc, (e1, e2) = _take_topk(cand, kk, [c1, c2])
        e = jnp.exp(sc - jnp.max(sc, axis=0, keepdims=True))
        gates.append(e / jnp.sum(e, axis=0, keepdims=True))
        i1s.append(e1)
        i2s.append(e2)
    i1_ref[...] = jnp.concatenate(i1s, axis=0).T
    i2_ref[...] = jnp.concatenate(i2s, axis=0).T
    gate_ref[...] = jnp.concatenate(gates, axis=0).T


def _peer_topk(qp, keys, tb=256):
    s = qp.shape[0]
    tb = min(tb, s)
    nsel = PEER_HEADS * PEER_TOPK
    row = pl.BlockSpec((tb, nsel), lambda i: (i, 0))
    return pl.pallas_call(
        _topk_kernel,
        grid=(s // tb,),
        in_specs=[pl.BlockSpec((tb, qp.shape[1]), lambda i: (i, 0)),
                  pl.BlockSpec(keys.shape, lambda i: (0, 0, 0, 0))],
        out_specs=[row, row, row],
        out_shape=[jax.ShapeDtypeStruct((s, nsel), f32)] * 3,
        compiler_params=_params("parallel"),
        name="peer_topk",
    )(qp, keys)


def _route_kernel(i1_ref, i2_ref, gate_ref, w_ref):
    n = PEER_NKEYS
    sub = lax.broadcasted_iota(jnp.int32, (n, i1_ref.shape[1]), 0).astype(f32)

    def body(t, carry):
        i1 = i1_ref[pl.ds(t, 1), :]
        i2 = i2_ref[pl.ds(t, 1), :]
        g = gate_ref[pl.ds(t, 1), :]
        a = jnp.where(sub == i1, g, 0.0).astype(bf16)
        b = jnp.where(sub == i2, 1.0, 0.0).astype(bf16)
        w_ref[t] = lax.dot_general(a, b, NT, preferred_element_type=f32).astype(w_ref.dtype)
        return carry

    lax.fori_loop(0, i1_ref.shape[0], body, 0)


def _peer_route(i1, i2, gate, tb=256):
    s, nsel = i1.shape
    tb = min(tb, s)
    row = pl.BlockSpec((tb, nsel), lambda i: (i, 0))
    return pl.pallas_call(
        _route_kernel,
        grid=(s // tb,),
        in_specs=[row, row, row],
        out_specs=pl.BlockSpec((tb, PEER_NKEYS, PEER_NKEYS), lambda i: (i, 0, 0)),
        out_shape=jax.ShapeDtypeStruct((s, PEER_NKEYS, PEER_NKEYS), bf16),
        compiler_params=_params("parallel"),
        name="peer_route",
    )(i1, i2, gate)


def _peer_dense_kernel(h_ref, u_ref, v_ref, w_ref, o_ref):
    e = pl.program_id(1)
    a = lax.dot_general(h_ref[...], u_ref[...], NT, preferred_element_type=f32)
    act = 0.5 * a * (1.0 + lax.erf(a * (2.0 ** -0.5)))
    wg = (act * w_ref[...].astype(f32)).astype(bf16)
    part = jnp.dot(wg, v_ref[...], preferred_element_type=f32)

    @pl.when(e == 0)
    def _():
        o_ref[...] = part

    @pl.when(e > 0)
    def _():
        o_ref[...] += part


def _peer_dense(h, u, v, w, tm=1024, te=512):
    s, d = h.shape
    tm = min(tm, s)
    ne = u.shape[0]
    return pl.pallas_call(
        _peer_dense_kernel,
        grid=(s // tm, ne // te),
        in_specs=[pl.BlockSpec((tm, d), lambda i, e: (i, 0)),
                  pl.BlockSpec((te, d), lambda i, e: (e, 0)),
                  pl.BlockSpec((te, d), lambda i, e: (e, 0)),
                  pl.BlockSpec((tm, te), lambda i, e: (i, e))],
        out_specs=pl.BlockSpec((tm, d), lambda i, e: (i, 0)),
        out_shape=jax.ShapeDtypeStruct((s, d), f32),
        compiler_params=_params("parallel", "arbitrary"),
        name="peer_dense",
    )(h, u, v, w)


def _ple_kernel(h_ref, wg_ref, p_ref, wp_ref, x_ref, o_ref):
    gate = _sigmoid(jnp.dot(h_ref[...], wg_ref[...].astype(bf16), preferred_element_type=f32))
    up = jnp.dot(p_ref[...].astype(bf16), wp_ref[...].astype(bf16), preferred_element_type=f32)
    o_ref[...] = x_ref[...] + gate * up


def _ple(h, w_gate, p, w_up, x, tm=1024, tn=512):
    s = h.shape[0]
    tm = min(tm, s)
    return pl.pallas_call(
        _ple_kernel,
        grid=(s // tm, D_MODEL // tn),
        in_specs=[pl.BlockSpec((tm, D_MODEL), lambda i, j: (i, 0)),
                  pl.BlockSpec((D_MODEL, tn), lambda i, j: (0, j)),
                  pl.BlockSpec((tm, PLE_DIM), lambda i, j: (i, 0)),
                  pl.BlockSpec((PLE_DIM, tn), lambda i, j: (0, j)),
                  pl.BlockSpec((tm, tn), lambda i, j: (i, j))],
        out_specs=pl.BlockSpec((tm, tn), lambda i, j: (i, j)),
        out_shape=jax.ShapeDtypeStruct((s, D_MODEL), f32),
        compiler_params=_params("parallel", "arbitrary"),
        name="ple",
    )(h, w_gate, p, w_up, x)


def _layer(i, x, p, pos, norm1_g, w_in, conv_w, conv_b, a_log_f, a_log_b, dt_bias_f, dt_bias_b,
           d_skip, ssd_norm_g, q_norm_g, k_norm_g, lam_q1, lam_k1, lam_q2, lam_k2, subln_g,
           w_ssd_br, w_att_br, w_out, norm2_g, peer_wq, peer_keys, peer_u, peer_v, norm3_g,
           ple_gate_w, ple_up_w):
    h = _rms_norm(x, norm1_g)
    zx = _matmul(h, w_in, 0, ZX_COLS, bf16)
    dt, dtt = _dt_proj(h, w_in[:, DT_COL0:REST_COL0])
    qkvg = _matmul(h, w_in[:, REST_COL0:], 0, w_in.shape[1] - REST_COL0, bf16)

    xbc = _conv_silu(zx, conv_w, conv_b)
    y_f, y_b = _ssd_scan(xbc, dt, dtt, jnp.concatenate([dt_bias_f, dt_bias_b]),
                         jnp.concatenate([a_log_f, a_log_b]))
    y_ssd = _ssd_post(y_f, y_b, xbc, zx, d_skip, ssd_norm_g)

    cos_t, sin_t = _rope_tables(pos)
    qk = _qk_prep(qkvg, q_norm_g, k_norm_g, cos_t, sin_t)
    lam_init = 0.8 - 0.6 * math.exp(-0.3 * i)
    y_att = _diff_attention(qk, qkvg, lam_q1, lam_k1, lam_q2, lam_k2, subln_g, lam_init)

    mixed = _merge(y_ssd, y_att, w_ssd_br, w_att_br, qkvg)
    x = _matmul(mixed, w_out, 0, D_MODEL, f32, residual=x)

    h2 = _rms_norm(x, norm2_g)
    qp = _matmul(h2, peer_wq, 0, peer_wq.shape[1], f32)
    i1, i2, gate = _peer_topk(qp, peer_keys)
    w = _peer_route(i1, i2, gate).reshape(x.shape[0], PEER_EXPERTS)
    mix = _peer_dense(h2, peer_u.astype(bf16), peer_v.astype(bf16), w)
    x, h3 = _add_rms_norm(x, mix, norm3_g)

    return _ple(h3, ple_gate_w, p, ple_up_w, x)


def kernel(x, p, positions, norm1_g, w_in, conv_w, conv_b, a_log_f, a_log_b, dt_bias_f, dt_bias_b, d_skip, ssd_norm_g, q_norm_g, k_norm_g, lam_q1, lam_k1, lam_q2, lam_k2, subln_g, w_ssd_br, w_att_br, w_out, norm2_g, peer_wq, peer_keys, peer_u, peer_v, norm3_g, ple_gate_w, ple_up_w):
    batch, seq, d = x.shape
    depth = w_in.shape[0]
    outs = []
    for b in range(batch):
        xb = x[b]
        for i in range(depth):
            xb = _layer(i, xb, p[i, b], positions[b], norm1_g[i], w_in[i], conv_w[i], conv_b[i],
                        a_log_f[i], a_log_b[i], dt_bias_f[i], dt_bias_b[i], d_skip[i],
                        ssd_norm_g[i], q_norm_g[i], k_norm_g[i], lam_q1[i], lam_k1[i], lam_q2[i],
                        lam_k2[i], subln_g[i], w_ssd_br[i], w_att_br[i], w_out[i], norm2_g[i],
                        peer_wq[i], peer_keys[i], peer_u[i], peer_v[i], norm3_g[i],
                        ple_gate_w[i], ple_up_w[i])
        outs.append(xb)
    return jnp.stack(outs)
```

```python
import functools
import math

import jax
import jax.numpy as jnp
from jax import lax
from jax.experimental import pallas as pl
from jax.experimental.pallas import tpu as pltpu

f32 = jnp.float32
bf16 = jnp.bfloat16

D_MODEL = 2048
D_SSD = 2048
SSD_HEAD_DIM = 64
SSD_HEADS = 32
SSD_GROUPS = 4
SSD_STATE = 128
GROUP_COLS = D_SSD // SSD_GROUPS
CONV_K = 5
CONV_CH = 3072
CHUNK = 128
ATT_HEADS = 16
ATT_HEAD_DIM = 64
ATT_V_DIM = 128
ROPE_THETA = 10000.0
PEER_HEADS = 8
PEER_NKEYS = 128
PEER_EXPERTS = PEER_NKEYS * PEER_NKEYS
PEER_TOPK = 16
PLE_DIM = 256
EPS = 1e-6
LOG2E = math.log2(math.e)

ZX_COLS = D_SSD + CONV_CH
DT_COL0 = ZX_COLS
REST_COL0 = ZX_COLS + 2 * SSD_HEADS
LANES = 128
BF16_ROWS = 16

VMEM_LIMIT = 56 * 1024 * 1024

NT = (((1,), (1,)), ((), ()))
TN = (((0,), (0,)), ((), ()))


def _params(*sem):
    return pltpu.CompilerParams(dimension_semantics=sem, vmem_limit_bytes=VMEM_LIMIT)


def _sigmoid(x):
    return 1.0 / (1.0 + jnp.exp(-x))


def _softplus(x):
    return jnp.maximum(x, 0.0) + jnp.log1p(jnp.exp(-jnp.abs(x)))


def _norm_kernel(x_ref, g_ref, h_ref):
    x = x_ref[...]
    ms = jnp.mean(x * x, axis=-1, keepdims=True)
    h_ref[...] = (x * lax.rsqrt(ms + EPS) * g_ref[...]).astype(h_ref.dtype)


def _rms_norm(x, g, tb=512):
    s, d = x.shape
    tb = min(tb, s)
    return pl.pallas_call(
        _norm_kernel,
        grid=(s // tb,),
        in_specs=[pl.BlockSpec((tb, d), lambda i: (i, 0)),
                  pl.BlockSpec((1, d), lambda i: (0, 0))],
        out_specs=pl.BlockSpec((tb, d), lambda i: (i, 0)),
        out_shape=jax.ShapeDtypeStruct((s, d), bf16),
        compiler_params=_params("parallel"),
        name="rms_norm",
    )(x, g.reshape(1, d))


def _add_norm_kernel(x_ref, d_ref, g_ref, xo_ref, h_ref):
    x = x_ref[...] + d_ref[...]
    xo_ref[...] = x
    ms = jnp.mean(x * x, axis=-1, keepdims=True)
    h_ref[...] = (x * lax.rsqrt(ms + EPS) * g_ref[...]).astype(h_ref.dtype)


def _add_rms_norm(x, delta, g, tb=512):
    s, d = x.shape
    tb = min(tb, s)
    row = pl.BlockSpec((tb, d), lambda i: (i, 0))
    return pl.pallas_call(
        _add_norm_kernel,
        grid=(s // tb,),
        in_specs=[row, row, pl.BlockSpec((1, d), lambda i: (0, 0))],
        out_specs=[row, row],
        out_shape=[jax.ShapeDtypeStruct((s, d), f32), jax.ShapeDtypeStruct((s, d), bf16)],
        compiler_params=_params("parallel"),
        name="add_rms_norm",
    )(x, delta, g.reshape(1, d))


def _mm_kernel(h_ref, w_ref, o_ref):
    o_ref[...] = jnp.dot(h_ref[...], w_ref[...].astype(bf16),
                         preferred_element_type=f32).astype(o_ref.dtype)


def _mm_res_kernel(h_ref, w_ref, r_ref, o_ref):
    o_ref[...] = r_ref[...] + jnp.dot(h_ref[...], w_ref[...].astype(bf16),
                                      preferred_element_type=f32)


def _matmul(h, w, col0, n, out_dtype, residual=None, tm=1024, tn=512):
    s, k = h.shape
    tm = min(tm, s)
    assert col0 % tn == 0 and n % tn == 0 and s % tm == 0
    cb = col0 // tn
    in_specs = [pl.BlockSpec((tm, k), lambda i, j: (i, 0)),
                pl.BlockSpec((k, tn), lambda i, j: (0, j + cb))]
    args = [h, w]
    kern = _mm_kernel
    if residual is not None:
        in_specs.append(pl.BlockSpec((tm, tn), lambda i, j: (i, j)))
        args.append(residual)
        kern = _mm_res_kernel
    return pl.pallas_call(
        kern,
        grid=(s // tm, n // tn),
        in_specs=in_specs,
        out_specs=pl.BlockSpec((tm, tn), lambda i, j: (i, j)),
        out_shape=jax.ShapeDtypeStruct((s, n), out_dtype),
        compiler_params=_params("parallel", "arbitrary"),
        name="matmul",
    )(*args)


def _dt_kernel(h_ref, w_ref, wt_ref, dt_ref, dtt_ref):
    h = h_ref[...]
    dt_ref[...] = jnp.dot(h, w_ref[...].astype(bf16), preferred_element_type=f32)
    dtt_ref[...] = lax.dot_general(wt_ref[...].astype(bf16), h, NT, preferred_element_type=f32)


def _dt_proj(h, w_dt, tm=1024):
    s, k = h.shape
    tm = min(tm, s)
    n = w_dt.shape[1]
    return pl.pallas_call(
        _dt_kernel,
        grid=(s // tm,),
        in_specs=[pl.BlockSpec((tm, k), lambda i: (i, 0)),
                  pl.BlockSpec((k, n), lambda i: (0, 0)),
                  pl.BlockSpec((n, k), lambda i: (0, 0))],
        out_specs=[pl.BlockSpec((tm, n), lambda i: (i, 0)),
                   pl.BlockSpec((n, tm), lambda i: (0, i))],
        out_shape=[jax.ShapeDtypeStruct((s, n), f32), jax.ShapeDtypeStruct((n, s), f32)],
        compiler_params=_params("parallel"),
        name="dt_proj",
    )(h, w_dt, w_dt.T)


def _conv_kernel(prev_ref, main_ref, next_ref, w_ref, b_ref, o_ref):
    i = pl.program_id(0)
    last = pl.num_programs(0) - 1
    tb = main_ref.shape[0]
    halo = prev_ref.shape[0]
    pv = jnp.where(i > 0, prev_ref[...].astype(f32), 0.0)
    nx = jnp.where(i < last, next_ref[...].astype(f32), 0.0)
    ext = jnp.concatenate([pv, main_ref[...].astype(f32), nx], axis=0)
    acc = b_ref[...] + jnp.zeros((tb, main_ref.shape[1]), f32)
    for k in range(CONV_K):
        off = halo + k - CONV_K // 2
        acc = acc + w_ref[k:k + 1, :] * ext[off:off + tb]
    o_ref[...] = (acc * _sigmoid(acc)).astype(o_ref.dtype)


def _conv_silu(zx, conv_w, conv_b, tb=512, tc=1024):
    s = zx.shape[0]
    tb = min(tb, s)
    halo = BF16_ROWS
    rb = tb // halo
    nhalo = s // halo
    cb = D_SSD // tc
    return pl.pallas_call(
        _conv_kernel,
        grid=(s // tb, CONV_CH // tc),
        in_specs=[
            pl.BlockSpec((halo, tc), lambda i, j: (jnp.maximum(i * rb - 1, 0), j + cb)),
            pl.BlockSpec((tb, tc), lambda i, j: (i, j + cb)),
            pl.BlockSpec((halo, tc), lambda i, j: (jnp.minimum((i + 1) * rb, nhalo - 1), j + cb)),
            pl.BlockSpec((CONV_K, tc), lambda i, j: (0, j)),
            pl.BlockSpec((1, tc), lambda i, j: (0, j)),
        ],
        out_specs=pl.BlockSpec((tb, tc), lambda i, j: (i, j)),
        out_shape=jax.ShapeDtypeStruct((s, CONV_CH), bf16),
        compiler_params=_params("parallel", "parallel"),
        name="conv_silu",
    )(zx, zx, zx, conv_w, conv_b.reshape(1, CONV_CH))


def _ssd_direction(xs, bm, cm, dt_raw, dtt_raw, bias, bias_t, a_log, a_log_t, expand,
                   state_ref, reverse):
    L = CHUNK
    dt = _softplus(dt_raw + bias)
    dtt = _softplus(dtt_raw + bias_t)
    a = dt * (-jnp.exp(a_log))
    at = dtt * (-jnp.exp(a_log_t))
    ri = lax.broadcasted_iota(jnp.int32, (L, L), 0)
    ci = lax.broadcasted_iota(jnp.int32, (L, L), 1)
    causal = (ri <= ci) if reverse else (ri >= ci)
    cum_l = jnp.where(causal, 1.0, 0.0).astype(f32)
    cum_r = jnp.where((ri >= ci) if reverse else (ri <= ci), 1.0, 0.0).astype(f32)
    acs = jnp.dot(cum_l, a, preferred_element_type=f32, precision=lax.Precision.HIGHEST)
    acst = jnp.dot(at, cum_r, preferred_element_type=f32, precision=lax.Precision.HIGHEST)
    edge = 0 if reverse else L - 1
    acs_end = acs[edge:edge + 1, :]
    small = jnp.concatenate([dt, jnp.exp(acs_end - acs), jnp.exp(acs)], axis=0)
    wide = jnp.dot(small, expand, preferred_element_type=f32, precision=lax.Precision.HIGHEST)
    dtx, dinx, eacsx = wide[:L], wide[L:2 * L], wide[2 * L:]
    xdt = xs * dtx
    xdt_b = xdt.astype(bf16)
    xdec_b = (xdt * dinx).astype(bf16)
    cdec = eacsx[edge:edge + 1, :]
    lane = lax.broadcasted_iota(jnp.int32, (L, LANES), 1)
    ys = []
    for g in range(SSD_GROUPS):
        bg = bm[:, g * SSD_STATE:(g + 1) * SSD_STATE]
        cg = cm[:, g * SSD_STATE:(g + 1) * SSD_STATE]
        gs = slice(g * GROUP_COLS, (g + 1) * GROUP_COLS)
        cb = lax.dot_general(cg, bg, NT, preferred_element_type=f32)
        h_in = state_ref[:, gs]
        y_off = jnp.dot(cg, h_in.astype(bf16), preferred_element_type=f32) * eacsx[:, gs]
        st = lax.dot_general(bg, xdec_b[:, gs], TN, preferred_element_type=f32)
        state_ref[:, gs] = h_in * cdec[:, gs] + st
        tiles = []
        for pair in range(GROUP_COLS // LANES):
            ms = []
            for sub in range(2):
                h = g * (SSD_HEADS // SSD_GROUPS) + pair * 2 + sub
                seg = acs[:, h:h + 1] - acst[h:h + 1, :]
                ms.append((jnp.where(causal, jnp.exp(seg), 0.0) * cb).astype(bf16))
            col = g * GROUP_COLS + pair * LANES
            xp = xdt_b[:, col:col + LANES]
            rhs = jnp.concatenate([jnp.where(lane < SSD_HEAD_DIM, xp, jnp.zeros_like(xp)),
                                   jnp.where(lane >= SSD_HEAD_DIM, xp, jnp.zeros_like(xp))], axis=0)
            tiles.append(jnp.dot(jnp.concatenate(ms, axis=1), rhs, preferred_element_type=f32))
        ys.append(jnp.concatenate(tiles, axis=1) + y_off)
    return jnp.concatenate(ys, axis=1)


def _ssd_kernel(xf_ref, bf_ref, cf_ref, dtf_ref, dttf_ref,
                xb_ref, bb_ref, cb_ref, dtb_ref, dttb_ref,
                bias_ref, biast_ref, alog_ref, alogt_ref, expand_ref,
                yf_ref, yb_ref, sf_ref, sb_ref):
    @pl.when(pl.program_id(0) == 0)
    def _():
        sf_ref[...] = jnp.zeros_like(sf_ref)
        sb_ref[...] = jnp.zeros_like(sb_ref)

    H = SSD_HEADS
    expand = expand_ref[...]
    yf_ref[...] = _ssd_direction(
        xf_ref[...].astype(f32), bf_ref[...], cf_ref[...],
        dtf_ref[:, :H], dttf_ref[:H, :], bias_ref[:, :H], biast_ref[:H, :],
        alog_ref[:, :H], alogt_ref[:H, :], expand, sf_ref, reverse=False)
    yb_ref[...] = _ssd_direction(
        xb_ref[...].astype(f32), bb_ref[...], cb_ref[...],
        dtb_ref[:, H:], dttb_ref[H:, :], bias_ref[:, H:], biast_ref[H:, :],
        alog_ref[:, H:], alogt_ref[H:, :], expand, sb_ref, reverse=True)


def _ssd_scan(xbc, dt, dtt, dt_bias, a_log):
    s = xbc.shape[0]
    nc = s // CHUNK
    gn = SSD_GROUPS * SSD_STATE
    bcol = D_SSD // gn
    fwd = lambda c: c
    bwd = lambda c: nc - 1 - c

    def chunk_specs(sel):
        return [pl.BlockSpec((CHUNK, D_SSD), lambda c: (sel(c), 0)),
                pl.BlockSpec((CHUNK, gn), lambda c: (sel(c), bcol)),
                pl.BlockSpec((CHUNK, gn), lambda c: (sel(c), bcol + 1)),
                pl.BlockSpec((CHUNK, 2 * SSD_HEADS), lambda c: (sel(c), 0)),
                pl.BlockSpec((2 * SSD_HEADS, CHUNK), lambda c: (0, sel(c)))]

    const = lambda shape: pl.BlockSpec(shape, lambda c: (0, 0))
    expand = (jnp.arange(D_SSD)[None, :] // SSD_HEAD_DIM == jnp.arange(SSD_HEADS)[:, None]).astype(f32)
    return pl.pallas_call(
        _ssd_kernel,
        grid=(nc,),
        in_specs=chunk_specs(fwd) + chunk_specs(bwd) + [
            const((1, 2 * SSD_HEADS)), const((2 * SSD_HEADS, 1)),
            const((1, 2 * SSD_HEADS)), const((2 * SSD_HEADS, 1)),
            const((SSD_HEADS, D_SSD))],
        out_specs=[pl.BlockSpec((CHUNK, D_SSD), lambda c: (fwd(c), 0)),
                   pl.BlockSpec((CHUNK, D_SSD), lambda c: (bwd(c), 0))],
        out_shape=[jax.ShapeDtypeStruct((s, D_SSD), f32)] * 2,
        scratch_shapes=[pltpu.VMEM((SSD_STATE, D_SSD), f32)] * 2,
        compiler_params=_params("arbitrary"),
        name="ssd_scan",
    )(xbc, xbc, xbc, dt, dtt, xbc, xbc, xbc, dt, dtt,
      dt_bias.reshape(1, -1), dt_bias.reshape(-1, 1), a_log.reshape(1, -1), a_log.reshape(-1, 1),
      expand)


def _ssd_post_kernel(yf_ref, yb_ref, xs_ref, z_ref, d_ref, g_ref, o_ref):
    z = z_ref[...].astype(f32)
    y = (yf_ref[...] + yb_ref[...] + d_ref[...] * xs_ref[...].astype(f32)) * (z * _sigmoid(z))
    ms = jnp.mean(y * y, axis=-1, keepdims=True)
    o_ref[...] = (y * lax.rsqrt(ms + EPS) * g_ref[...]).astype(o_ref.dtype)


def _ssd_post(y_f, y_b, xbc, zx, d_skip, g, tb=512):
    s = y_f.shape[0]
    tb = min(tb, s)
    row = pl.BlockSpec((tb, D_SSD), lambda i: (i, 0))
    vec = pl.BlockSpec((1, D_SSD), lambda i: (0, 0))
    return pl.pallas_call(
        _ssd_post_kernel,
        grid=(s // tb,),
        in_specs=[row, row, row, row, vec, vec],
        out_specs=row,
        out_shape=jax.ShapeDtypeStruct((s, D_SSD), bf16),
        compiler_params=_params("parallel"),
        name="ssd_post",
    )(y_f, y_b, xbc, zx, jnp.repeat(d_skip, SSD_HEAD_DIM).reshape(1, D_SSD), g.reshape(1, D_SSD))


def _rope_table_kernel(pos_ref, inv_ref, sign_ref, cos_ref, sin_ref):
    ang = pos_ref[...].astype(f32) * inv_ref[...]
    cos_ref[...] = jnp.cos(ang)
    sin_ref[...] = jnp.sin(ang) * sign_ref[...]


def _rope_tables(positions, tb=512):
    s = positions.shape[0]
    tb = min(tb, s)
    half = ATT_HEAD_DIM // 2
    inv = ROPE_THETA ** (-jnp.arange(0, ATT_HEAD_DIM, 2, dtype=f32) / ATT_HEAD_DIM)
    inv_t = jnp.tile(inv, LANES // half).reshape(1, LANES)
    sign = jnp.tile(jnp.concatenate([-jnp.ones((half,), f32), jnp.ones((half,), f32)]),
                    LANES // ATT_HEAD_DIM).reshape(1, LANES)
    vec = pl.BlockSpec((1, LANES), lambda i: (0, 0))
    row = pl.BlockSpec((tb, LANES), lambda i: (i, 0))
    return pl.pallas_call(
        _rope_table_kernel,
        grid=(s // tb,),
        in_specs=[pl.BlockSpec((tb, 1), lambda i: (i, 0)), vec, vec],
        out_specs=[row, row],
        out_shape=[jax.ShapeDtypeStruct((s, LANES), f32)] * 2,
        compiler_params=_params("parallel"),
        name="rope_tables",
    )(positions.reshape(s, 1), inv_t, sign)


def _qk_kernel(x_ref, g_ref, cos_ref, sin_ref, o_ref):
    which = pl.program_id(1)
    scale = jnp.where(which == 0, ATT_HEAD_DIM ** -0.5 * LOG2E, 1.0).astype(f32)
    g = g_ref[0] * scale
    cos = cos_ref[...]
    sin = sin_ref[...]
    r = lax.broadcasted_iota(jnp.int32, (LANES, LANES), 0) // ATT_HEAD_DIM
    c = lax.broadcasted_iota(jnp.int32, (LANES, LANES), 1) // ATT_HEAD_DIM
    seg = jnp.where(r == c, 1.0, 0.0).astype(bf16)
    lane = lax.broadcasted_iota(jnp.int32, cos.shape, 1)
    first_half = (lane % ATT_HEAD_DIM) < ATT_HEAD_DIM // 2
    for t in range(x_ref.shape[1] // LANES):
        x = x_ref[:, t * LANES:(t + 1) * LANES].astype(f32)
        sq = x * x
        hi = sq.astype(bf16)
        lo = (sq - hi.astype(f32)).astype(bf16)
        ss = (jnp.dot(hi, seg, preferred_element_type=f32)
              + jnp.dot(lo, seg, preferred_element_type=f32))
        xn = x * lax.rsqrt(ss * (1.0 / ATT_HEAD_DIM) + EPS) * g
        swapped = jnp.where(first_half,
                            pltpu.roll(xn, LANES - ATT_HEAD_DIM // 2, axis=1),
                            pltpu.roll(xn, ATT_HEAD_DIM // 2, axis=1))
        o_ref[:, t * LANES:(t + 1) * LANES] = (xn * cos + swapped * sin).astype(o_ref.dtype)


def _qk_prep(qkvg, q_g, k_g, cos_t, sin_t, tb=512):
    s = qkvg.shape[0]
    tb = min(tb, s)
    cols = 2 * ATT_HEADS * ATT_HEAD_DIM
    g2 = jnp.stack([jnp.tile(q_g, 2), jnp.tile(k_g, 2)]).reshape(2, 1, LANES)
    tab = pl.BlockSpec((tb, LANES), lambda i, j: (i, 0))
    return pl.pallas_call(
        _qk_kernel,
        grid=(s // tb, 2),
        in_specs=[pl.BlockSpec((tb, cols), lambda i, j: (i, j)),
                  pl.BlockSpec((1, 1, LANES), lambda i, j: (j, 0, 0)),
                  tab, tab],
        out_specs=pl.BlockSpec((tb, cols), lambda i, j: (i, j)),
        out_shape=jax.ShapeDtypeStruct((s, 2 * cols), bf16),
        compiler_params=_params("parallel", "parallel"),
        name="qk_prep",
    )(qkvg, g2, cos_t, sin_t)


def _attn_kernel(q_ref, k_ref, v_ref, lq1_ref, lk1_ref, lq2_ref, lk2_ref, g_ref, o_ref,
                 qm_ref, s_ref, p_ref, m_ref, l_ref, acc_ref, *, tk, strip, lam_init):
    tq = q_ref.shape[0]
    nk = k_ref.shape[0] // tk
    q = q_ref[...]
    lane = lax.broadcasted_iota(jnp.int32, q.shape, 1)
    zero = jnp.zeros_like(q)
    qm_ref[0] = jnp.where(lane < ATT_HEAD_DIM, q, zero)
    qm_ref[1] = jnp.where(lane >= ATT_HEAD_DIM, q, zero)
    m_ref[...] = jnp.full(m_ref.shape, -1e30, f32)
    l_ref[...] = jnp.zeros_like(l_ref)
    acc_ref[...] = jnp.zeros_like(acc_ref)

    def scores(c, slot):
        kc = k_ref[pl.ds(pl.multiple_of(c * tk, tk), tk), :]
        for half in range(2):
            s_ref[slot, half] = lax.dot_general(qm_ref[half], kc, NT, preferred_element_type=f32)

    def absorb(c, slot):
        vc = v_ref[pl.ds(pl.multiple_of(c * tk, tk), tk), :]
        for half in range(2):
            for r0 in range(0, tq, strip):
                rows = slice(r0, r0 + strip)
                s = s_ref[slot, half, rows, :]
                mx = s[:, :LANES]
                for t in range(1, tk // LANES):
                    mx = jnp.maximum(mx, s[:, t * LANES:(t + 1) * LANES])
                m_old = m_ref[half, rows, :]
                m_new = jnp.maximum(m_old, jnp.max(mx, axis=-1, keepdims=True))
                alpha = jnp.exp2(m_old - m_new)
                p = jnp.exp2(s - m_new)
                ps = p[:, :LANES]
                for t in range(1, tk // LANES):
                    ps = ps + p[:, t * LANES:(t + 1) * LANES]
                l_ref[half, rows, :] = alpha * l_ref[half, rows, :] + ps
                acc_ref[half, rows, :] = alpha * acc_ref[half, rows, :]
                m_ref[half, rows, :] = m_new
                p_ref[half, rows, :] = p.astype(bf16)
            acc_ref[half] += jnp.dot(p_ref[half], vc, preferred_element_type=f32)

    scores(0, 0)

    def pair(i, carry):
        c = 2 * i
        scores(c + 1, 1)
        absorb(c, 0)
        scores(jnp.minimum(c + 2, nk - 1), 0)
        absorb(c + 1, 1)
        return carry

    lax.fori_loop(0, nk // 2, pair, 0)
    lam = (jnp.exp(jnp.sum(lq1_ref[...] * lk1_ref[...], axis=-1, keepdims=True))
           - jnp.exp(jnp.sum(lq2_ref[...] * lk2_ref[...], axis=-1, keepdims=True)) + lam_init)
    o1 = acc_ref[0] / jnp.sum(l_ref[0], axis=-1, keepdims=True)
    o2 = acc_ref[1] / jnp.sum(l_ref[1], axis=-1, keepdims=True)
    o = o1 - lam * o2
    ms = jnp.mean(o * o, axis=-1, keepdims=True)
    o_ref[...] = (o * lax.rsqrt(ms + EPS) * (g_ref[...] * (1.0 - lam_init))).astype(o_ref.dtype)


def _diff_attention(qk, qkvg, lam_q1, lam_k1, lam_q2, lam_k2, subln_g, lam_init,
                    tq=512, tk=512, strip=64):
    s = qk.shape[0]
    tq = min(tq, s)
    tk = min(tk, s // 2)
    assert (s // tk) % 2 == 0 and tq % strip == 0
    kcol = 2 * ATT_HEADS * ATT_HEAD_DIM // LANES
    vcol = 2 * kcol
    vec = lambda n: pl.BlockSpec((1, n), lambda h, i: (0, 0))
    return pl.pallas_call(
        functools.partial(_attn_kernel, tk=tk, strip=strip, lam_init=lam_init),
        grid=(ATT_HEADS, s // tq),
        in_specs=[pl.BlockSpec((tq, LANES), lambda h, i: (i, h)),
                  pl.BlockSpec((s, LANES), lambda h, i: (0, kcol + h)),
                  pl.BlockSpec((s, LANES), lambda h, i: (0, vcol + h)),
                  vec(ATT_HEAD_DIM), vec(ATT_HEAD_DIM), vec(ATT_HEAD_DIM), vec(ATT_HEAD_DIM),
                  vec(ATT_V_DIM)],
        out_specs=pl.BlockSpec((tq, LANES), lambda h, i: (i, h)),
        out_shape=jax.ShapeDtypeStruct((s, ATT_HEADS * ATT_V_DIM), bf16),
        scratch_shapes=[pltpu.VMEM((2, tq, LANES), bf16),
                        pltpu.VMEM((2, 2, tq, tk), f32),
                        pltpu.VMEM((2, tq, tk), bf16),
                        pltpu.VMEM((2, tq, 1), f32),
                        pltpu.VMEM((2, tq, LANES), f32),
                        pltpu.VMEM((2, tq, ATT_V_DIM), f32)],
        compiler_params=_params("parallel", "arbitrary"),
        name="diff_attention",
    )(qk, qk, qkvg, lam_q1.reshape(1, -1), lam_k1.reshape(1, -1), lam_q2.reshape(1, -1),
      lam_k2.reshape(1, -1), subln_g.reshape(1, -1))


def _merge_kernel(ys_ref, ya_ref, wa_ref, wb_ref, gs_ref, ga_ref, o_ref):
    a = jnp.dot(ys_ref[...], wa_ref[...].astype(bf16), preferred_element_type=f32)
    b = jnp.dot(ya_ref[...], wb_ref[...].astype(bf16), preferred_element_type=f32)
    mixed = _sigmoid(gs_ref[...].astype(f32)) * a + _sigmoid(ga_ref[...].astype(f32)) * b
    o_ref[...] = mixed.astype(o_ref.dtype)


def _merge(y_ssd, y_att, w_a, w_b, qkvg, tm=1024, tn=512):
    s = y_ssd.shape[0]
    tm = min(tm, s)
    gcol = 3 * 2 * ATT_HEADS * ATT_HEAD_DIM // tn
    nj = D_MODEL // tn
    row = pl.BlockSpec((tm, D_MODEL), lambda i, j: (i, 0))
    wcol = pl.BlockSpec((D_MODEL, tn), lambda i, j: (0, j))
    return pl.pallas_call(
        _merge_kernel,
        grid=(s // tm, nj),
        in_specs=[row, row, wcol, wcol,
                  pl.BlockSpec((tm, tn), lambda i, j: (i, gcol + j)),
                  pl.BlockSpec((tm, tn), lambda i, j: (i, gcol + nj + j))],
        out_specs=pl.BlockSpec((tm, tn), lambda i, j: (i, j)),
        out_shape=jax.ShapeDtypeStruct((s, D_MODEL), bf16),
        compiler_params=_params("parallel", "arbitrary"),
        name="merge",
    )(y_ssd, y_att, w_a, w_b, qkvg, qkvg)


def _take_topk(s, k, payloads):
    rows = s.shape[0]
    riota = lax.broadcasted_iota(jnp.int32, s.shape, 0)
    vals, picked = [], [[] for _ in payloads]
    for _ in range(k):
        m = jnp.max(s, axis=0, keepdims=True)
        pos = jnp.min(jnp.where(s == m, riota, rows), axis=0, keepdims=True)
        hit = riota == pos
        vals.append(m)
        for out, pay in zip(picked, payloads):
            out.append(jnp.max(jnp.where(hit, pay, -1.0), axis=0, keepdims=True))
        s = jnp.where(hit, -jnp.inf, s)
    return jnp.concatenate(vals, axis=0), [jnp.concatenate(p, axis=0) for p in picked]


def _topk_kernel(q_ref, keys_ref, i1_ref, i2_ref, gate_ref):
    tb = q_ref.shape[0]
    kk = PEER_TOPK
    key_iota = lax.broadcasted_iota(jnp.int32, (PEER_NKEYS, tb), 0).astype(f32)
    i1s, i2s, gates = [], [], []
    for h in range(PEER_HEADS):
        tops = []
        for half in range(2):
            c0 = (h * 2 + half) * PEER_NKEYS
            qh = q_ref[:, c0:c0 + PEER_NKEYS].astype(bf16)
            keys = keys_ref[h, half].astype(bf16)
            s = lax.dot_general(keys, qh, NT, preferred_element_type=f32)
            tops.append(_take_topk(s, kk, [key_iota]))
        (v1, (i1,)), (v2, (i2,)) = tops
        cand = jnp.concatenate([v1[a:a + 1] + v2 for a in range(kk)], axis=0)
        c1 = jnp.concatenate([jnp.broadcast_to(i1[a:a + 1], (kk, tb)) for a in range(kk)], axis=0)
        c2 = jnp.concatenate([i2] * kk, axis=0)
        sc, (e1, e2) = _take_topk(cand, kk, [c1, c2])
        e = jnp.exp(sc - jnp.max(sc, axis=0, keepdims=True))
        gates.append(e / jnp.sum(e, axis=0, keepdims=True))
        i1s.append(e1)
        i2s.append(e2)
    i1_ref[...] = jnp.concatenate(i1s, axis=0).T
    i2_ref[...] = jnp.concatenate(i2s, axis=0).T
    gate_ref[...] = jnp.concatenate(gates, axis=0).T


def _peer_topk(qp, keys, tb=256):
    s = qp.shape[0]
    tb = min(tb, s)
    nsel = PEER_HEADS * PEER_TOPK
    row = pl.BlockSpec((tb, nsel), lambda i: (i, 0))
    return pl.pallas_call(
        _topk_kernel,
        grid=(s // tb,),
        in_specs=[pl.BlockSpec((tb, qp.shape[1]), lambda i: (i, 0)),
                  pl.BlockSpec(keys.shape, lambda i: (0, 0, 0, 0))],
        out_specs=[row, row, row],
        out_shape=[jax.ShapeDtypeStruct((s, nsel), f32)] * 3,
        compiler_params=_params("parallel"),
        name="peer_topk",
    )(qp, keys)


def _route_kernel(i1_ref, i2_ref, gate_ref, w_ref):
    n = PEER_NKEYS
    sub = lax.broadcasted_iota(jnp.int32, (n, i1_ref.shape[1]), 0).astype(f32)

    def body(t, carry):
        i1 = i1_ref[pl.ds(t, 1), :]
        i2 = i2_ref[pl.ds(t, 1), :]
        g = gate_ref[pl.ds(t, 1), :]
        a = jnp.where(sub == i1, g, 0.0).astype(bf16)
        b = jnp.where(sub == i2, 1.0, 0.0).astype(bf16)
        w_ref[t] = lax.dot_general(a, b, NT, preferred_element_type=f32).astype(w_ref.dtype)
        return carry

    lax.fori_loop(0, i1_ref.shape[0], body, 0)


def _peer_route(i1, i2, gate, tb=256):
    s, nsel = i1.shape
    tb = min(tb, s)
    row = pl.BlockSpec((tb, nsel), lambda i: (i, 0))
    return pl.pallas_call(
        _route_kernel,
        grid=(s // tb,),
        in_specs=[row, row, row],
        out_specs=pl.BlockSpec((tb, PEER_NKEYS, PEER_NKEYS), lambda i: (i, 0, 0)),
        out_shape=jax.ShapeDtypeStruct((s, PEER_NKEYS, PEER_NKEYS), bf16),
        compiler_params=_params("parallel"),
        name="peer_route",
    )(i1, i2, gate)


def _peer_dense_kernel(h_ref, u_ref, v_ref, w_ref, o_ref):
    e = pl.program_id(1)
    a = lax.dot_general(h_ref[...], u_ref[...], NT, preferred_element_type=f32)
    act = 0.5 * a * (1.0 + lax.erf(a * (2.0 ** -0.5)))
    wg = (act * w_ref[...].astype(f32)).astype(bf16)
    part = jnp.dot(wg, v_ref[...], preferred_element_type=f32)

    @pl.when(e == 0)
    def _():
        o_ref[...] = part

    @pl.when(e > 0)
    def _():
        o_ref[...] += part


def _peer_dense(h, u, v, w, tm=1024, te=512):
    s, d = h.shape
    tm = min(tm, s)
    ne = u.shape[0]
    return pl.pallas_call(
        _peer_dense_kernel,
        grid=(s // tm, ne // te),
        in_specs=[pl.BlockSpec((tm, d), lambda i, e: (i, 0)),
                  pl.BlockSpec((te, d), lambda i, e: (e, 0)),
                  pl.BlockSpec((te, d), lambda i, e: (e, 0)),
                  pl.BlockSpec((tm, te), lambda i, e: (i, e))],
        out_specs=pl.BlockSpec((tm, d), lambda i, e: (i, 0)),
        out_shape=jax.ShapeDtypeStruct((s, d), f32),
        compiler_params=_params("parallel", "arbitrary"),
        name="peer_dense",
    )(h, u, v, w)


def _ple_kernel(h_ref, wg_ref, p_ref, wp_ref, x_ref, o_ref):
    gate = _sigmoid(jnp.dot(h_ref[...], wg_ref[...].astype(bf16), preferred_element_type=f32))
    up = jnp.dot(p_ref[...].astype(bf16), wp_ref[...].astype(bf16), preferred_element_type=f32)
    o_ref[...] = x_ref[...] + gate * up


def _ple(h, w_gate, p, w_up, x, tm=1024, tn=512):
    s = h.shape[0]
    tm = min(tm, s)
    return pl.pallas_call(
        _ple_kernel,
        grid=(s // tm, D_MODEL // tn),
        in_specs=[pl.BlockSpec((tm, D_MODEL), lambda i, j: (i, 0)),
                  pl.BlockSpec((D_MODEL, tn), lambda i, j: (0, j)),
                  pl.BlockSpec((tm, PLE_DIM), lambda i, j: (i, 0)),
                  pl.BlockSpec((PLE_DIM, tn), lambda i, j: (0, j)),
                  pl.BlockSpec((tm, tn), lambda i, j: (i, j))],
        out_specs=pl.BlockSpec((tm, tn), lambda i, j: (i, j)),
        out_shape=jax.ShapeDtypeStruct((s, D_MODEL), f32),
        compiler_params=_params("parallel", "arbitrary"),
        name="ple",
    )(h, w_gate, p, w_up, x)


def _layer(i, x, p, pos, norm1_g, w_in, conv_w, conv_b, a_log_f, a_log_b, dt_bias_f, dt_bias_b,
           d_skip, ssd_norm_g, q_norm_g, k_norm_g, lam_q1, lam_k1, lam_q2, lam_k2, subln_g,
           w_ssd_br, w_att_br, w_out, norm2_g, peer_wq, peer_keys, peer_u, peer_v, norm3_g,
           ple_gate_w, ple_up_w):
    h = _rms_norm(x, norm1_g)
    zx = _matmul(h, w_in, 0, ZX_COLS, bf16)
    dt, dtt = _dt_proj(h, w_in[:, DT_COL0:REST_COL0])
    qkvg = _matmul(h, w_in[:, REST_COL0:], 0, w_in.shape[1] - REST_COL0, bf16)

    xbc = _conv_silu(zx, conv_w, conv_b)
    y_f, y_b = _ssd_scan(xbc, dt, dtt, jnp.concatenate([dt_bias_f, dt_bias_b]),
                         jnp.concatenate([a_log_f, a_log_b]))
    y_ssd = _ssd_post(y_f, y_b, xbc, zx, d_skip, ssd_norm_g)

    cos_t, sin_t = _rope_tables(pos)
    qk = _qk_prep(qkvg, q_norm_g, k_norm_g, cos_t, sin_t)
    lam_init = 0.8 - 0.6 * math.exp(-0.3 * i)
    y_att = _diff_attention(qk, qkvg, lam_q1, lam_k1, lam_q2, lam_k2, subln_g, lam_init)

    mixed = _merge(y_ssd, y_att, w_ssd_br, w_att_br, qkvg)
    x = _matmul(mixed, w_out, 0, D_MODEL, f32, residual=x)

    h2 = _rms_norm(x, norm2_g)
    qp = _matmul(h2, peer_wq, 0, peer_wq.shape[1], f32)
    i1, i2, gate = _peer_topk(qp, peer_keys)
    w = _peer_route(i1, i2, gate).reshape(x.shape[0], PEER_EXPERTS)
    mix = _peer_dense(h2, peer_u.astype(bf16), peer_v.astype(bf16), w)
    x, h3 = _add_rms_norm(x, mix, norm3_g)

    return _ple(h3, ple_gate_w, p, ple_up_w, x)


def kernel(x, p, positions, norm1_g, w_in, conv_w, conv_b, a_log_f, a_log_b, dt_bias_f, dt_bias_b, d_skip, ssd_norm_g, q_norm_g, k_norm_g, lam_q1, lam_k1, lam_q2, lam_k2, subln_g, w_ssd_br, w_att_br, w_out, norm2_g, peer_wq, peer_keys, peer_u, peer_v, norm3_g, ple_gate_w, ple_up_w):
    batch, seq, d = x.shape
    depth = w_in.shape[0]
    outs = []
    for b in range(batch):
        xb = x[b]
        for i in range(depth):
            xb = _layer(i, xb, p[i, b], positions[b], norm1_g[i], w_in[i], conv_w[i], conv_b[i],
                        a_log_f[i], a_log_b[i], dt_bias_f[i], dt_bias_b[i], d_skip[i],
                        ssd_norm_g[i], q_norm_g[i], k_norm_g[i], lam_q1[i], lam_k1[i], lam_q2[i],
                        lam_k2[i], subln_g[i], w_ssd_br[i], w_att_br[i], w_out[i], norm2_g[i],
                        peer_wq[i], peer_keys[i], peer_u[i], peer_v[i], norm3_g[i],
                        ple_gate_w[i], ple_up_w[i])
        outs.append(xb)
    return jnp.stack(outs)
```

```python
import functools
import math

import jax
import jax.numpy as jnp
from jax import lax
from jax.experimental import pallas as pl
from jax.experimental.pallas import tpu as pltpu

f32 = jnp.float32
bf16 = jnp.bfloat16

D_MODEL = 2048
D_SSD = 2048
SSD_HEAD_DIM = 64
SSD_HEADS = 32
SSD_GROUPS = 4
SSD_STATE = 128
GROUP_COLS = D_SSD // SSD_GROUPS
CONV_K = 5
CONV_CH = 3072
CHUNK = 128
ATT_HEADS = 16
ATT_HEAD_DIM = 64
ATT_V_DIM = 128
ROPE_THETA = 10000.0
PEER_HEADS = 8
PEER_NKEYS = 128
PEER_EXPERTS = PEER_NKEYS * PEER_NKEYS
PEER_TOPK = 16
PLE_DIM = 256
EPS = 1e-6
LOG2E = math.log2(math.e)

ZX_COLS = D_SSD + CONV_CH
DT_COL0 = ZX_COLS
REST_COL0 = ZX_COLS + 2 * SSD_HEADS
LANES = 128
BF16_ROWS = 16

VMEM_LIMIT = 56 * 1024 * 1024

NT = (((1,), (1,)), ((), ()))
TN = (((0,), (0,)), ((), ()))


def _params(*sem):
    return pltpu.CompilerParams(dimension_semantics=sem, vmem_limit_bytes=VMEM_LIMIT)


def _sigmoid(x):
    return 1.0 / (1.0 + jnp.exp(-x))


def _softplus(x):
    return jnp.maximum(x, 0.0) + jnp.log1p(jnp.exp(-jnp.abs(x)))


def _norm_kernel(x_ref, g_ref, h_ref):
    x = x_ref[...]
    ms = jnp.mean(x * x, axis=-1, keepdims=True)
    h_ref[...] = (x * lax.rsqrt(ms + EPS) * g_ref[...]).astype(h_ref.dtype)


def _rms_norm(x, g, tb=512):
    s, d = x.shape
    tb = min(tb, s)
    return pl.pallas_call(
        _norm_kernel,
        grid=(s // tb,),
        in_specs=[pl.BlockSpec((tb, d), lambda i: (i, 0)),
                  pl.BlockSpec((1, d), lambda i: (0, 0))],
        out_specs=pl.BlockSpec((tb, d), lambda i: (i, 0)),
        out_shape=jax.ShapeDtypeStruct((s, d), bf16),
        compiler_params=_params("parallel"),
        name="rms_norm",
    )(x, g.reshape(1, d))


def _add_norm_kernel(x_ref, d_ref, g_ref, xo_ref, h_ref):
    x = x_ref[...] + d_ref[...]
    xo_ref[...] = x
    ms = jnp.mean(x * x, axis=-1, keepdims=True)
    h_ref[...] = (x * lax.rsqrt(ms + EPS) * g_ref[...]).astype(h_ref.dtype)


def _add_rms_norm(x, delta, g, tb=512):
    s, d = x.shape
    tb = min(tb, s)
    row = pl.BlockSpec((tb, d), lambda i: (i, 0))
    return pl.pallas_call(
        _add_norm_kernel,
        grid=(s // tb,),
        in_specs=[row, row, pl.BlockSpec((1, d), lambda i: (0, 0))],
        out_specs=[row, row],
        out_shape=[jax.ShapeDtypeStruct((s, d), f32), jax.ShapeDtypeStruct((s, d), bf16)],
        compiler_params=_params("parallel"),
        name="add_rms_norm",
    )(x, delta, g.reshape(1, d))


def _mm_kernel(h_ref, w_ref, o_ref):
    o_ref[...] = jnp.dot(h_ref[...], w_ref[...].astype(bf16),
                         preferred_element_type=f32).astype(o_ref.dtype)


def _mm_res_kernel(h_ref, w_ref, r_ref, o_ref):
    o_ref[...] = r_ref[...] + jnp.dot(h_ref[...], w_ref[...].astype(bf16),
                                      preferred_element_type=f32)


def _matmul(h, w, col0, n, out_dtype, residual=None, tm=1024, tn=512):
    s, k = h.shape
    tm = min(tm, s)
    assert col0 % tn == 0 and n % tn == 0 and s % tm == 0
    cb = col0 // tn
    in_specs = [pl.BlockSpec((tm, k), lambda i, j: (i, 0)),
                pl.BlockSpec((k, tn), lambda i, j: (0, j + cb))]
    args = [h, w]
    kern = _mm_kernel
    if residual is not None:
        in_specs.append(pl.BlockSpec((tm, tn), lambda i, j: (i, j)))
        args.append(residual)
        kern = _mm_res_kernel
    return pl.pallas_call(
        kern,
        grid=(s // tm, n // tn),
        in_specs=in_specs,
        out_specs=pl.BlockSpec((tm, tn), lambda i, j: (i, j)),
        out_shape=jax.ShapeDtypeStruct((s, n), out_dtype),
        compiler_params=_params("parallel", "arbitrary"),
        name="matmul",
    )(*args)


def _dt_kernel(h_ref, w_ref, wt_ref, dt_ref, dtt_ref):
    h = h_ref[...]
    dt_ref[...] = jnp.dot(h, w_ref[...].astype(bf16), preferred_element_type=f32)
    dtt_ref[...] = lax.dot_general(wt_ref[...].astype(bf16), h, NT, preferred_element_type=f32)


def _dt_proj(h, w_dt, tm=1024):
    s, k = h.shape
    tm = min(tm, s)
    n = w_dt.shape[1]
    return pl.pallas_call(
        _dt_kernel,
        grid=(s // tm,),
        in_specs=[pl.BlockSpec((tm, k), lambda i: (i, 0)),
                  pl.BlockSpec((k, n), lambda i: (0, 0)),
                  pl.BlockSpec((n, k), lambda i: (0, 0))],
        out_specs=[pl.BlockSpec((tm, n), lambda i: (i, 0)),
                   pl.BlockSpec((n, tm), lambda i: (0, i))],
        out_shape=[jax.ShapeDtypeStruct((s, n), f32), jax.ShapeDtypeStruct((n, s), f32)],
        compiler_params=_params("parallel"),
        name="dt_proj",
    )(h, w_dt, w_dt.T)


def _conv_kernel(prev_ref, main_ref, next_ref, w_ref, b_ref, o_ref):
    i = pl.program_id(0)
    last = pl.num_programs(0) - 1
    tb = main_ref.shape[0]
    halo = prev_ref.shape[0]
    pv = jnp.where(i > 0, prev_ref[...].astype(f32), 0.0)
    nx = jnp.where(i < last, next_ref[...].astype(f32), 0.0)
    ext = jnp.concatenate([pv, main_ref[...].astype(f32), nx], axis=0)
    acc = b_ref[...] + jnp.zeros((tb, main_ref.shape[1]), f32)
    for k in range(CONV_K):
        off = halo + k - CONV_K // 2
        acc = acc + w_ref[k:k + 1, :] * ext[off:off + tb]
    o_ref[...] = (acc * _sigmoid(acc)).astype(o_ref.dtype)


def _conv_silu(zx, conv_w, conv_b, tb=512, tc=1024):
    s = zx.shape[0]
    tb = min(tb, s)
    halo = BF16_ROWS
    rb = tb // halo
    nhalo = s // halo
    cb = D_SSD // tc
    return pl.pallas_call(
        _conv_kernel,
        grid=(s // tb, CONV_CH // tc),
        in_specs=[
            pl.BlockSpec((halo, tc), lambda i, j: (jnp.maximum(i * rb - 1, 0), j + cb)),
            pl.BlockSpec((tb, tc), lambda i, j: (i, j + cb)),
            pl.BlockSpec((halo, tc), lambda i, j: (jnp.minimum((i + 1) * rb, nhalo - 1), j + cb)),
            pl.BlockSpec((CONV_K, tc), lambda i, j: (0, j)),
            pl.BlockSpec((1, tc), lambda i, j: (0, j)),
        ],
        out_specs=pl.BlockSpec((tb, tc), lambda i, j: (i, j)),
        out_shape=jax.ShapeDtypeStruct((s, CONV_CH), bf16),
        compiler_params=_params("parallel", "parallel"),
        name="conv_silu",
    )(zx, zx, zx, conv_w, conv_b.reshape(1, CONV_CH))


def _ssd_direction(xs, bm, cm, dt_raw, dtt_raw, bias, bias_t, a_log, a_log_t, expand,
                   state_ref, reverse):
    L = CHUNK
    dt = _softplus(dt_raw + bias)
    dtt = _softplus(dtt_raw + bias_t)
    a = dt * (-jnp.exp(a_log))
    at = dtt * (-jnp.exp(a_log_t))
    ri = lax.broadcasted_iota(jnp.int32, (L, L), 0)
    ci = lax.broadcasted_iota(jnp.int32, (L, L), 1)
    causal = (ri <= ci) if reverse else (ri >= ci)
    cum_l = jnp.where(causal, 1.0, 0.0).astype(f32)
    cum_r = jnp.where((ri >= ci) if reverse else (ri <= ci), 1.0, 0.0).astype(f32)
    acs = jnp.dot(cum_l, a, preferred_element_type=f32, precision=lax.Precision.HIGHEST)
    acst = jnp.dot(at, cum_r, preferred_element_type=f32, precision=lax.Precision.HIGHEST)
    edge = 0 if reverse else L - 1
    acs_end = acs[edge:edge + 1, :]
    small = jnp.concatenate([dt, jnp.exp(acs_end - acs), jnp.exp(acs)], axis=0)
    small_hi = small.astype(bf16)
    small_lo = (small - small_hi.astype(f32)).astype(bf16)
    wide = (jnp.dot(small_hi, expand, preferred_element_type=f32)
            + jnp.dot(small_lo, expand, preferred_element_type=f32))
    dtx, dinx, eacsx = wide[:L], wide[L:2 * L], wide[2 * L:]
    xdt = xs * dtx
    xdt_b = xdt.astype(bf16)
    xdec_b = (xdt * dinx).astype(bf16)
    cdec = eacsx[edge:edge + 1, :]
    lane = lax.broadcasted_iota(jnp.int32, (L, LANES), 1)
    ys = []
    for g in range(SSD_GROUPS):
        bg = bm[:, g * SSD_STATE:(g + 1) * SSD_STATE]
        cg = cm[:, g * SSD_STATE:(g + 1) * SSD_STATE]
        gs = slice(g * GROUP_COLS, (g + 1) * GROUP_COLS)
        cb = lax.dot_general(cg, bg, NT, preferred_element_type=f32)
        h_in = state_ref[:, gs]
        y_off = jnp.dot(cg, h_in.astype(bf16), preferred_element_type=f32) * eacsx[:, gs]
        st = lax.dot_general(bg, xdec_b[:, gs], TN, preferred_element_type=f32)
        state_ref[:, gs] = h_in * cdec[:, gs] + st
        tiles = []
        for pair in range(GROUP_COLS // LANES):
            ms = []
            for sub in range(2):
                h = g * (SSD_HEADS // SSD_GROUPS) + pair * 2 + sub
                seg = acs[:, h:h + 1] - acst[h:h + 1, :]
                ms.append((jnp.where(causal, jnp.exp(seg), 0.0) * cb).astype(bf16))
            col = g * GROUP_COLS + pair * LANES
            xp = xdt_b[:, col:col + LANES]
            rhs = jnp.concatenate([jnp.where(lane < SSD_HEAD_DIM, xp, jnp.zeros_like(xp)),
                                   jnp.where(lane >= SSD_HEAD_DIM, xp, jnp.zeros_like(xp))], axis=0)
            tiles.append(jnp.dot(jnp.concatenate(ms, axis=1), rhs, preferred_element_type=f32))
        ys.append(jnp.concatenate(tiles, axis=1) + y_off)
    return jnp.concatenate(ys, axis=1)


def _ssd_kernel(xf_ref, bf_ref, cf_ref, dtf_ref, dttf_ref,
                xb_ref, bb_ref, cb_ref, dtb_ref, dttb_ref,
                bias_ref, biast_ref, alog_ref, alogt_ref, expand_ref,
                yf_ref, yb_ref, sf_ref, sb_ref):
    @pl.when(pl.program_id(0) == 0)
    def _():
        sf_ref[...] = jnp.zeros_like(sf_ref)
        sb_ref[...] = jnp.zeros_like(sb_ref)

    H = SSD_HEADS
    expand = expand_ref[...]
    yf_ref[...] = _ssd_direction(
        xf_ref[...].astype(f32), bf_ref[...], cf_ref[...],
        dtf_ref[:, :H], dttf_ref[:H, :], bias_ref[:, :H], biast_ref[:H, :],
        alog_ref[:, :H], alogt_ref[:H, :], expand, sf_ref, reverse=False)
    yb_ref[...] = _ssd_direction(
        xb_ref[...].astype(f32), bb_ref[...], cb_ref[...],
        dtb_ref[:, H:], dttb_ref[H:, :], bias_ref[:, H:], biast_ref[H:, :],
        alog_ref[:, H:], alogt_ref[H:, :], expand, sb_ref, reverse=True)


def _ssd_scan(xbc, dt, dtt, dt_bias, a_log):
    s = xbc.shape[0]
    nc = s // CHUNK
    gn = SSD_GROUPS * SSD_STATE
    bcol = D_SSD // gn
    fwd = lambda c: c
    bwd = lambda c: nc - 1 - c

    def chunk_specs(sel):
        return [pl.BlockSpec((CHUNK, D_SSD), lambda c: (sel(c), 0)),
                pl.BlockSpec((CHUNK, gn), lambda c: (sel(c), bcol)),
                pl.BlockSpec((CHUNK, gn), lambda c: (sel(c), bcol + 1)),
                pl.BlockSpec((CHUNK, 2 * SSD_HEADS), lambda c: (sel(c), 0)),
                pl.BlockSpec((2 * SSD_HEADS, CHUNK), lambda c: (0, sel(c)))]

    const = lambda shape: pl.BlockSpec(shape, lambda c: (0, 0))
    expand = (jnp.arange(D_SSD)[None, :] // SSD_HEAD_DIM == jnp.arange(SSD_HEADS)[:, None]).astype(bf16)
    return pl.pallas_call(
        _ssd_kernel,
        grid=(nc,),
        in_specs=chunk_specs(fwd) + chunk_specs(bwd) + [
            const((1, 2 * SSD_HEADS)), const((2 * SSD_HEADS, 1)),
            const((1, 2 * SSD_HEADS)), const((2 * SSD_HEADS, 1)),
            const((SSD_HEADS, D_SSD))],
        out_specs=[pl.BlockSpec((CHUNK, D_SSD), lambda c: (fwd(c), 0)),
                   pl.BlockSpec((CHUNK, D_SSD), lambda c: (bwd(c), 0))],
        out_shape=[jax.ShapeDtypeStruct((s, D_SSD), f32)] * 2,
        scratch_shapes=[pltpu.VMEM((SSD_STATE, D_SSD), f32)] * 2,
        compiler_params=_params("arbitrary"),
        name="ssd_scan",
    )(xbc, xbc, xbc, dt, dtt, xbc, xbc, xbc, dt, dtt,
      dt_bias.reshape(1, -1), dt_bias.reshape(-1, 1), a_log.reshape(1, -1), a_log.reshape(-1, 1),
      expand)


def _ssd_post_kernel(yf_ref, yb_ref, xs_ref, z_ref, d_ref, g_ref, o_ref):
    z = z_ref[...].astype(f32)
    y = (yf_ref[...] + yb_ref[...] + d_ref[...] * xs_ref[...].astype(f32)) * (z * _sigmoid(z))
    ms = jnp.mean(y * y, axis=-1, keepdims=True)
    o_ref[...] = (y * lax.rsqrt(ms + EPS) * g_ref[...]).astype(o_ref.dtype)


def _ssd_post(y_f, y_b, xbc, zx, d_skip, g, tb=512):
    s = y_f.shape[0]
    tb = min(tb, s)
    row = pl.BlockSpec((tb, D_SSD), lambda i: (i, 0))
    vec = pl.BlockSpec((1, D_SSD), lambda i: (0, 0))
    return pl.pallas_call(
        _ssd_post_kernel,
        grid=(s // tb,),
        in_specs=[row, row, row, row, vec, vec],
        out_specs=row,
        out_shape=jax.ShapeDtypeStruct((s, D_SSD), bf16),
        compiler_params=_params("parallel"),
        name="ssd_post",
    )(y_f, y_b, xbc, zx, jnp.repeat(d_skip, SSD_HEAD_DIM).reshape(1, D_SSD), g.reshape(1, D_SSD))


def _rope_table_kernel(pos_ref, inv_ref, sign_ref, cos_ref, sin_ref):
    ang = pos_ref[...].astype(f32) * inv_ref[...]
    cos_ref[...] = jnp.cos(ang)
    sin_ref[...] = jnp.sin(ang) * sign_ref[...]


def _rope_tables(positions, tb=512):
    s = positions.shape[0]
    tb = min(tb, s)
    half = ATT_HEAD_DIM // 2
    inv = ROPE_THETA ** (-jnp.arange(0, ATT_HEAD_DIM, 2, dtype=f32) / ATT_HEAD_DIM)
    inv_t = jnp.tile(inv, LANES // half).reshape(1, LANES)
    sign = jnp.tile(jnp.concatenate([-jnp.ones((half,), f32), jnp.ones((half,), f32)]),
                    LANES // ATT_HEAD_DIM).reshape(1, LANES)
    vec = pl.BlockSpec((1, LANES), lambda i: (0, 0))
    row = pl.BlockSpec((tb, LANES), lambda i: (i, 0))
    return pl.pallas_call(
        _rope_table_kernel,
        grid=(s // tb,),
        in_specs=[pl.BlockSpec((tb, 1), lambda i: (i, 0)), vec, vec],
        out_specs=[row, row],
        out_shape=[jax.ShapeDtypeStruct((s, LANES), f32)] * 2,
        compiler_params=_params("parallel"),
        name="rope_tables",
    )(positions.reshape(s, 1), inv_t, sign)


def _qk_kernel(x_ref, g_ref, cos_ref, sin_ref, o_ref):
    which = pl.program_id(1)
    scale = jnp.where(which == 0, ATT_HEAD_DIM ** -0.5 * LOG2E, 1.0).astype(f32)
    g = g_ref[0] * scale
    cos = cos_ref[...]
    sin = sin_ref[...]
    r = lax.broadcasted_iota(jnp.int32, (LANES, LANES), 0) // ATT_HEAD_DIM
    c = lax.broadcasted_iota(jnp.int32, (LANES, LANES), 1) // ATT_HEAD_DIM
    seg = jnp.where(r == c, 1.0, 0.0).astype(bf16)
    lane = lax.broadcasted_iota(jnp.int32, cos.shape, 1)
    first_half = (lane % ATT_HEAD_DIM) < ATT_HEAD_DIM // 2
    for t in range(x_ref.shape[1] // LANES):
        x = x_ref[:, t * LANES:(t + 1) * LANES].astype(f32)
        sq = x * x
        hi = sq.astype(bf16)
        lo = (sq - hi.astype(f32)).astype(bf16)
        ss = (jnp.dot(hi, seg, preferred_element_type=f32)
              + jnp.dot(lo, seg, preferred_element_type=f32))
        xn = x * lax.rsqrt(ss * (1.0 / ATT_HEAD_DIM) + EPS) * g
        swapped = jnp.where(first_half,
                            pltpu.roll(xn, LANES - ATT_HEAD_DIM // 2, axis=1),
                            pltpu.roll(xn, ATT_HEAD_DIM // 2, axis=1))
        o_ref[:, t * LANES:(t + 1) * LANES] = (xn * cos + swapped * sin).astype(o_ref.dtype)


def _qk_prep(qkvg, q_g, k_g, cos_t, sin_t, tb=512):
    s = qkvg.shape[0]
    tb = min(tb, s)
    cols = 2 * ATT_HEADS * ATT_HEAD_DIM
    g2 = jnp.stack([jnp.tile(q_g, 2), jnp.tile(k_g, 2)]).reshape(2, 1, LANES)
    tab = pl.BlockSpec((tb, LANES), lambda i, j: (i, 0))
    return pl.pallas_call(
        _qk_kernel,
        grid=(s // tb, 2),
        in_specs=[pl.BlockSpec((tb, cols), lambda i, j: (i, j)),
                  pl.BlockSpec((1, 1, LANES), lambda i, j: (j, 0, 0)),
                  tab, tab],
        out_specs=pl.BlockSpec((tb, cols), lambda i, j: (i, j)),
        out_shape=jax.ShapeDtypeStruct((s, 2 * cols), bf16),
        compiler_params=_params("parallel", "parallel"),
        name="qk_prep",
    )(qkvg, g2, cos_t, sin_t)


def _attn_kernel(q_ref, k_ref, v_ref, lq1_ref, lk1_ref, lq2_ref, lk2_ref, g_ref, o_ref,
                 qm_ref, s_ref, p_ref, m_ref, l_ref, acc_ref, *, tk, strip, lam_init):
    tq = q_ref.shape[0]
    nk = k_ref.shape[0] // tk
    q = q_ref[...]
    lane = lax.broadcasted_iota(jnp.int32, q.shape, 1)
    zero = jnp.zeros_like(q)
    qm_ref[0] = jnp.where(lane < ATT_HEAD_DIM, q, zero)
    qm_ref[1] = jnp.where(lane >= ATT_HEAD_DIM, q, zero)
    m_ref[...] = jnp.full(m_ref.shape, -1e30, f32)
    l_ref[...] = jnp.zeros_like(l_ref)
    acc_ref[...] = jnp.zeros_like(acc_ref)

    def scores(c, slot):
        kc = k_ref[pl.ds(pl.multiple_of(c * tk, tk), tk), :]
        for half in range(2):
            s_ref[slot, half] = lax.dot_general(qm_ref[half], kc, NT, preferred_element_type=f32)

    def absorb(c, slot):
        vc = v_ref[pl.ds(pl.multiple_of(c * tk, tk), tk), :]
        for half in range(2):
            for r0 in range(0, tq, strip):
                rows = slice(r0, r0 + strip)
                s = s_ref[slot, half, rows, :]
                mx = s[:, :LANES]
                for t in range(1, tk // LANES):
                    mx = jnp.maximum(mx, s[:, t * LANES:(t + 1) * LANES])
                m_old = m_ref[half, rows, :]
                m_new = jnp.maximum(m_old, jnp.max(mx, axis=-1, keepdims=True))
                alpha = jnp.exp2(m_old - m_new)
                p = jnp.exp2(s - m_new)
                ps = p[:, :LANES]
                for t in range(1, tk // LANES):
                    ps = ps + p[:, t * LANES:(t + 1) * LANES]
                l_ref[half, rows, :] = alpha * l_ref[half, rows, :] + ps
                acc_ref[half, rows, :] = alpha * acc_ref[half, rows, :]
                m_ref[half, rows, :] = m_new
                p_ref[half, rows, :] = p.astype(bf16)
            acc_ref[half] += jnp.dot(p_ref[half], vc, preferred_element_type=f32)

    scores(0, 0)

    def pair(i, carry):
        c = 2 * i
        scores(c + 1, 1)
        absorb(c, 0)
        scores(jnp.minimum(c + 2, nk - 1), 0)
        absorb(c + 1, 1)
        return carry

    lax.fori_loop(0, nk // 2, pair, 0)
    lam = (jnp.exp(jnp.sum(lq1_ref[...] * lk1_ref[...], axis=-1, keepdims=True))
           - jnp.exp(jnp.sum(lq2_ref[...] * lk2_ref[...], axis=-1, keepdims=True)) + lam_init)
    o1 = acc_ref[0] / jnp.sum(l_ref[0], axis=-1, keepdims=True)
    o2 = acc_ref[1] / jnp.sum(l_ref[1], axis=-1, keepdims=True)
    o = o1 - lam * o2
    ms = jnp.mean(o * o, axis=-1, keepdims=True)
    o_ref[...] = (o * lax.rsqrt(ms + EPS) * (g_ref[...] * (1.0 - lam_init))).astype(o_ref.dtype)


def _diff_attention(qk, qkvg, lam_q1, lam_k1, lam_q2, lam_k2, subln_g, lam_init,
                    tq=512, tk=512, strip=64):
    s = qk.shape[0]
    tq = min(tq, s)
    tk = min(tk, s // 2)
    assert (s // tk) % 2 == 0 and tq % strip == 0
    kcol = 2 * ATT_HEADS * ATT_HEAD_DIM // LANES
    vcol = 2 * kcol
    vec = lambda n: pl.BlockSpec((1, n), lambda h, i: (0, 0))
    return pl.pallas_call(
        functools.partial(_attn_kernel, tk=tk, strip=strip, lam_init=lam_init),
        grid=(ATT_HEADS, s // tq),
        in_specs=[pl.BlockSpec((tq, LANES), lambda h, i: (i, h)),
                  pl.BlockSpec((s, LANES), lambda h, i: (0, kcol + h)),
                  pl.BlockSpec((s, LANES), lambda h, i: (0, vcol + h)),
                  vec(ATT_HEAD_DIM), vec(ATT_HEAD_DIM), vec(ATT_HEAD_DIM), vec(ATT_HEAD_DIM),
                  vec(ATT_V_DIM)],
        out_specs=pl.BlockSpec((tq, LANES), lambda h, i: (i, h)),
        out_shape=jax.ShapeDtypeStruct((s, ATT_HEADS * ATT_V_DIM), bf16),
        scratch_shapes=[pltpu.VMEM((2, tq, LANES), bf16),
                        pltpu.VMEM((2, 2, tq, tk), f32),
                        pltpu.VMEM((2, tq, tk), bf16),
                        pltpu.VMEM((2, tq, 1), f32),
                        pltpu.VMEM((2, tq, LANES), f32),
                        pltpu.VMEM((2, tq, ATT_V_DIM), f32)],
        compiler_params=_params("parallel", "arbitrary"),
        name="diff_attention",
    )(qk, qk, qkvg, lam_q1.reshape(1, -1), lam_k1.reshape(1, -1), lam_q2.reshape(1, -1),
      lam_k2.reshape(1, -1), subln_g.reshape(1, -1))


def _merge_kernel(ys_ref, ya_ref, wa_ref, wb_ref, gs_ref, ga_ref, o_ref):
    a = jnp.dot(ys_ref[...], wa_ref[...].astype(bf16), preferred_element_type=f32)
    b = jnp.dot(ya_ref[...], wb_ref[...].astype(bf16), preferred_element_type=f32)
    mixed = _sigmoid(gs_ref[...].astype(f32)) * a + _sigmoid(ga_ref[...].astype(f32)) * b
    o_ref[...] = mixed.astype(o_ref.dtype)


def _merge(y_ssd, y_att, w_a, w_b, qkvg, tm=1024, tn=512):
    s = y_ssd.shape[0]
    tm = min(tm, s)
    gcol = 3 * 2 * ATT_HEADS * ATT_HEAD_DIM // tn
    nj = D_MODEL // tn
    row = pl.BlockSpec((tm, D_MODEL), lambda i, j: (i, 0))
    wcol = pl.BlockSpec((D_MODEL, tn), lambda i, j: (0, j))
    return pl.pallas_call(
        _merge_kernel,
        grid=(s // tm, nj),
        in_specs=[row, row, wcol, wcol,
                  pl.BlockSpec((tm, tn), lambda i, j: (i, gcol + j)),
                  pl.BlockSpec((tm, tn), lambda i, j: (i, gcol + nj + j))],
        out_specs=pl.BlockSpec((tm, tn), lambda i, j: (i, j)),
        out_shape=jax.ShapeDtypeStruct((s, D_MODEL), bf16),
        compiler_params=_params("parallel", "arbitrary"),
        name="merge",
    )(y_ssd, y_att, w_a, w_b, qkvg, qkvg)


def _take_topk(s, k, rank, payloads):
    big = jnp.int32(2 ** 30)
    vals, ranks, picked = [], [], [[] for _ in payloads]
    for _ in range(k):
        m = jnp.max(s, axis=0, keepdims=True)
        pos = jnp.min(jnp.where(s == m, rank, big), axis=0, keepdims=True)
        hit = rank == pos
        vals.append(m)
        ranks.append(pos)
        for out, pay in zip(picked, payloads):
            out.append(jnp.max(jnp.where(hit, pay, -1.0), axis=0, keepdims=True))
        s = jnp.where(hit, -jnp.inf, s)
    cat = lambda xs: jnp.concatenate(xs, axis=0)
    return cat(vals), cat(ranks), [cat(p) for p in picked]


def _pair_candidates(v1, i1, v2, i2):
    k = PEER_TOPK
    tb = v1.shape[1]
    row8 = lax.broadcasted_iota(jnp.int32, (8, tb), 0)
    neg = jnp.float32(-jnp.inf)
    sums, flat, c1, c2 = [], [], [], []
    for a in range(k // 2):
        nb = k // (a + 1)
        for b0 in range(0, nb, 8):
            piece = v1[a:a + 1] + v2[b0:b0 + 8]
            if nb - b0 < 8:
                piece = jnp.where(row8 < nb - b0, piece, neg)
            sums.append(piece)
            flat.append(row8 + (a * k + b0))
            c1.append(jnp.broadcast_to(i1[a:a + 1], (8, tb)))
            c2.append(i2[b0:b0 + 8])
    sums.append(v1[k // 2:] + v2[0:1])
    flat.append((row8 + k // 2) * k)
    c1.append(i1[k // 2:])
    c2.append(jnp.broadcast_to(i2[0:1], (8, tb)))
    cat = lambda xs: jnp.concatenate(xs, axis=0)
    return cat(sums), cat(flat), cat(c1), cat(c2)


def _topk_kernel(q_ref, keys_ref, i1_ref, i2_ref, gate_ref):
    tb = q_ref.shape[0]
    kk = PEER_TOPK
    key_iota = lax.broadcasted_iota(jnp.int32, (PEER_NKEYS, tb), 0)
    i1s, i2s, gates = [], [], []
    for h in range(PEER_HEADS):
        tops = []
        for half in range(2):
            c0 = (h * 2 + half) * PEER_NKEYS
            qh = q_ref[:, c0:c0 + PEER_NKEYS].astype(bf16)
            keys = keys_ref[h, half].astype(bf16)
            s = lax.dot_general(keys, qh, NT, preferred_element_type=f32)
            vals, idx, _ = _take_topk(s, kk, key_iota, [])
            tops.append((vals, idx.astype(f32)))
        (v1, i1), (v2, i2) = tops
        cand, flat, c1, c2 = _pair_candidates(v1, i1, v2, i2)
        sc, _, (e1, e2) = _take_topk(cand, kk, flat, [c1, c2])
        e = jnp.exp(sc - jnp.max(sc, axis=0, keepdims=True))
        gates.append(e / jnp.sum(e, axis=0, keepdims=True))
        i1s.append(e1)
        i2s.append(e2)
    i1_ref[...] = jnp.concatenate(i1s, axis=0).T
    i2_ref[...] = jnp.concatenate(i2s, axis=0).T
    gate_ref[...] = jnp.concatenate(gates, axis=0).T


def _peer_topk(qp, keys, tb=256):
    s = qp.shape[0]
    tb = min(tb, s)
    nsel = PEER_HEADS * PEER_TOPK
    row = pl.BlockSpec((tb, nsel), lambda i: (i, 0))
    return pl.pallas_call(
        _topk_kernel,
        grid=(s // tb,),
        in_specs=[pl.BlockSpec((tb, qp.shape[1]), lambda i: (i, 0)),
                  pl.BlockSpec(keys.shape, lambda i: (0, 0, 0, 0))],
        out_specs=[row, row, row],
        out_shape=[jax.ShapeDtypeStruct((s, nsel), f32)] * 3,
        compiler_params=_params("parallel"),
        name="peer_topk",
    )(qp, keys)


ROUTE_PITCH = PEER_NKEYS + 8


def _route_kernel(i1_ref, i2_ref, gate_ref, w_ref, tile_ref):
    n = PEER_NKEYS
    tb = i1_ref.shape[0]
    sub = lax.broadcasted_iota(jnp.int32, (n, i1_ref.shape[1]), 0).astype(f32)

    def body(t, carry):
        i1 = i1_ref[pl.ds(t, 1), :]
        i2 = i2_ref[pl.ds(t, 1), :]
        g = gate_ref[pl.ds(t, 1), :]
        a = jnp.where(sub == i1, g, 0.0).astype(bf16)
        b = jnp.where(sub == i2, 1.0, 0.0).astype(bf16)
        tile_ref[pl.ds(pl.multiple_of(t * ROUTE_PITCH, 8), n), :] = lax.dot_general(
            a, b, NT, preferred_element_type=f32)
        return carry

    lax.fori_loop(0, tb, body, 0, unroll=8)
    for a in range(n):
        w_ref[:, a * n:(a + 1) * n] = tile_ref[pl.ds(a, tb, stride=ROUTE_PITCH), :].astype(w_ref.dtype)


def _peer_route(i1, i2, gate, tb=256):
    s, nsel = i1.shape
    tb = min(tb, s)
    row = pl.BlockSpec((tb, nsel), lambda i: (i, 0))
    return pl.pallas_call(
        _route_kernel,
        grid=(s // tb,),
        in_specs=[row, row, row],
        out_specs=pl.BlockSpec((tb, PEER_EXPERTS), lambda i: (i, 0)),
        out_shape=jax.ShapeDtypeStruct((s, PEER_EXPERTS), bf16),
        scratch_shapes=[pltpu.VMEM((tb * ROUTE_PITCH, PEER_NKEYS), f32)],
        compiler_params=_params("parallel"),
        name="peer_route",
    )(i1, i2, gate)


def _peer_dense_kernel(h_ref, u_ref, v_ref, w_ref, o_ref):
    e = pl.program_id(1)
    a = lax.dot_general(h_ref[...], u_ref[...], NT, preferred_element_type=f32)
    act = 0.5 * a * (1.0 + lax.erf(a * (2.0 ** -0.5)))
    wg = (act * w_ref[...].astype(f32)).astype(bf16)
    part = jnp.dot(wg, v_ref[...], preferred_element_type=f32)

    @pl.when(e == 0)
    def _():
        o_ref[...] = part

    @pl.when(e > 0)
    def _():
        o_ref[...] += part


def _peer_dense(h, u, v, w, tm=1024, te=512):
    s, d = h.shape
    tm = min(tm, s)
    ne = u.shape[0]
    return pl.pallas_call(
        _peer_dense_kernel,
        grid=(s // tm, ne // te),
        in_specs=[pl.BlockSpec((tm, d), lambda i, e: (i, 0)),
                  pl.BlockSpec((te, d), lambda i, e: (e, 0)),
                  pl.BlockSpec((te, d), lambda i, e: (e, 0)),
                  pl.BlockSpec((tm, te), lambda i, e: (i, e))],
        out_specs=pl.BlockSpec((tm, d), lambda i, e: (i, 0)),
        out_shape=jax.ShapeDtypeStruct((s, d), f32),
        compiler_params=_params("parallel", "arbitrary"),
        name="peer_dense",
    )(h, u, v, w)


def _ple_kernel(h_ref, wg_ref, p_ref, wp_ref, x_ref, o_ref):
    gate = _sigmoid(jnp.dot(h_ref[...], wg_ref[...].astype(bf16), preferred_element_type=f32))
    up = jnp.dot(p_ref[...].astype(bf16), wp_ref[...].astype(bf16), preferred_element_type=f32)
    o_ref[...] = x_ref[...] + gate * up


def _ple(h, w_gate, p, w_up, x, tm=1024, tn=512):
    s = h.shape[0]
    tm = min(tm, s)
    return pl.pallas_call(
        _ple_kernel,
        grid=(s // tm, D_MODEL // tn),
        in_specs=[pl.BlockSpec((tm, D_MODEL), lambda i, j: (i, 0)),
                  pl.BlockSpec((D_MODEL, tn), lambda i, j: (0, j)),
                  pl.BlockSpec((tm, PLE_DIM), lambda i, j: (i, 0)),
                  pl.BlockSpec((PLE_DIM, tn), lambda i, j: (0, j)),
                  pl.BlockSpec((tm, tn), lambda i, j: (i, j))],
        out_specs=pl.BlockSpec((tm, tn), lambda i, j: (i, j)),
        out_shape=jax.ShapeDtypeStruct((s, D_MODEL), f32),
        compiler_params=_params("parallel", "arbitrary"),
        name="ple",
    )(h, w_gate, p, w_up, x)


def _layer(i, x, p, pos, norm1_g, w_in, conv_w, conv_b, a_log_f, a_log_b, dt_bias_f, dt_bias_b,
           d_skip, ssd_norm_g, q_norm_g, k_norm_g, lam_q1, lam_k1, lam_q2, lam_k2, subln_g,
           w_ssd_br, w_att_br, w_out, norm2_g, peer_wq, peer_keys, peer_u, peer_v, norm3_g,
           ple_gate_w, ple_up_w):
    h = _rms_norm(x, norm1_g)
    zx = _matmul(h, w_in, 0, ZX_COLS, bf16)
    dt, dtt = _dt_proj(h, w_in[:, DT_COL0:REST_COL0])
    qkvg = _matmul(h, w_in[:, REST_COL0:], 0, w_in.shape[1] - REST_COL0, bf16)

    xbc = _conv_silu(zx, conv_w, conv_b)
    y_f, y_b = _ssd_scan(xbc, dt, dtt, jnp.concatenate([dt_bias_f, dt_bias_b]),
                         jnp.concatenate([a_log_f, a_log_b]))
    y_ssd = _ssd_post(y_f, y_b, xbc, zx, d_skip, ssd_norm_g)

    cos_t, sin_t = _rope_tables(pos)
    qk = _qk_prep(qkvg, q_norm_g, k_norm_g, cos_t, sin_t)
    lam_init = 0.8 - 0.6 * math.exp(-0.3 * i)
    y_att = _diff_attention(qk, qkvg, lam_q1, lam_k1, lam_q2, lam_k2, subln_g, lam_init)

    mixed = _merge(y_ssd, y_att, w_ssd_br, w_att_br, qkvg)
    x = _matmul(mixed, w_out, 0, D_MODEL, f32, residual=x)

    h2 = _rms_norm(x, norm2_g)
    qp = _matmul(h2, peer_wq, 0, peer_wq.shape[1], f32)
    i1, i2, gate = _peer_topk(qp, peer_keys)
    w = _peer_route(i1, i2, gate)
    mix = _peer_dense(h2, peer_u.astype(bf16), peer_v.astype(bf16), w)
    x, h3 = _add_rms_norm(x, mix, norm3_g)

    return _ple(h3, ple_gate_w, p, ple_up_w, x)


def kernel(x, p, positions, norm1_g, w_in, conv_w, conv_b, a_log_f, a_log_b, dt_bias_f, dt_bias_b, d_skip, ssd_norm_g, q_norm_g, k_norm_g, lam_q1, lam_k1, lam_q2, lam_k2, subln_g, w_ssd_br, w_att_br, w_out, norm2_g, peer_wq, peer_keys, peer_u, peer_v, norm3_g, ple_gate_w, ple_up_w):
    batch, seq, d = x.shape
    depth = w_in.shape[0]
    outs = []
    for b in range(batch):
        xb = x[b]
        for i in range(depth):
            xb = _layer(i, xb, p[i, b], positions[b], norm1_g[i], w_in[i], conv_w[i], conv_b[i],
                        a_log_f[i], a_log_b[i], dt_bias_f[i], dt_bias_b[i], d_skip[i],
                        ssd_norm_g[i], q_norm_g[i], k_norm_g[i], lam_q1[i], lam_k1[i], lam_q2[i],
                        lam_k2[i], subln_g[i], w_ssd_br[i], w_att_br[i], w_out[i], norm2_g[i],
                        peer_wq[i], peer_keys[i], peer_u[i], peer_v[i], norm3_g[i],
                        ple_gate_w[i], ple_up_w[i])
        outs.append(xb)
    return jnp.stack(outs)
```

```python
import functools
import math

import jax
import jax.numpy as jnp
from jax import lax
from jax.experimental import pallas as pl
from jax.experimental.pallas import tpu as pltpu

f32 = jnp.float32
bf16 = jnp.bfloat16

D_MODEL = 2048
D_SSD = 2048
SSD_HEAD_DIM = 64
SSD_HEADS = 32
SSD_GROUPS = 4
SSD_STATE = 128
GROUP_COLS = D_SSD // SSD_GROUPS
CONV_K = 5
CONV_CH = 3072
CHUNK = 128
ATT_HEADS = 16
ATT_HEAD_DIM = 64
ATT_V_DIM = 128
ROPE_THETA = 10000.0
PEER_HEADS = 8
PEER_NKEYS = 128
PEER_EXPERTS = PEER_NKEYS * PEER_NKEYS
PEER_TOPK = 16
PLE_DIM = 256
EPS = 1e-6
LOG2E = math.log2(math.e)

ZX_COLS = D_SSD + CONV_CH
DT_COL0 = ZX_COLS
REST_COL0 = ZX_COLS + 2 * SSD_HEADS
LANES = 128
BF16_ROWS = 16

VMEM_LIMIT = 56 * 1024 * 1024

NT = (((1,), (1,)), ((), ()))
TN = (((0,), (0,)), ((), ()))


def _params(*sem):
    return pltpu.CompilerParams(dimension_semantics=sem, vmem_limit_bytes=VMEM_LIMIT)


def _sigmoid(x):
    return 1.0 / (1.0 + jnp.exp(-x))


def _softplus(x):
    return jnp.maximum(x, 0.0) + jnp.log1p(jnp.exp(-jnp.abs(x)))


def _norm_kernel(x_ref, g_ref, h_ref):
    x = x_ref[...]
    ms = jnp.mean(x * x, axis=-1, keepdims=True)
    h_ref[...] = (x * lax.rsqrt(ms + EPS) * g_ref[...]).astype(h_ref.dtype)


def _rms_norm(x, g, tb=512):
    s, d = x.shape
    tb = min(tb, s)
    return pl.pallas_call(
        _norm_kernel,
        grid=(s // tb,),
        in_specs=[pl.BlockSpec((tb, d), lambda i: (i, 0)),
                  pl.BlockSpec((1, d), lambda i: (0, 0))],
        out_specs=pl.BlockSpec((tb, d), lambda i: (i, 0)),
        out_shape=jax.ShapeDtypeStruct((s, d), bf16),
        compiler_params=_params("parallel"),
        name="rms_norm",
    )(x, g.reshape(1, d))


def _add_norm_kernel(x_ref, d_ref, g_ref, xo_ref, h_ref):
    x = x_ref[...] + d_ref[...]
    xo_ref[...] = x
    ms = jnp.mean(x * x, axis=-1, keepdims=True)
    h_ref[...] = (x * lax.rsqrt(ms + EPS) * g_ref[...]).astype(h_ref.dtype)


def _add_rms_norm(x, delta, g, tb=512):
    s, d = x.shape
    tb = min(tb, s)
    row = pl.BlockSpec((tb, d), lambda i: (i, 0))
    return pl.pallas_call(
        _add_norm_kernel,
        grid=(s // tb,),
        in_specs=[row, row, pl.BlockSpec((1, d), lambda i: (0, 0))],
        out_specs=[row, row],
        out_shape=[jax.ShapeDtypeStruct((s, d), f32), jax.ShapeDtypeStruct((s, d), bf16)],
        compiler_params=_params("parallel"),
        name="add_rms_norm",
    )(x, delta, g.reshape(1, d))


def _mm_kernel(h_ref, w_ref, o_ref):
    o_ref[...] = jnp.dot(h_ref[...], w_ref[...].astype(bf16),
                         preferred_element_type=f32).astype(o_ref.dtype)


def _mm_res_kernel(h_ref, w_ref, r_ref, o_ref):
    o_ref[...] = r_ref[...] + jnp.dot(h_ref[...], w_ref[...].astype(bf16),
                                      preferred_element_type=f32)


def _matmul(h, w, col0, n, out_dtype, residual=None, tm=2048, tn=512):
    s, k = h.shape
    tm = min(tm, s)
    assert col0 % tn == 0 and n % tn == 0 and s % tm == 0
    cb = col0 // tn
    in_specs = [pl.BlockSpec((tm, k), lambda i, j: (i, 0)),
                pl.BlockSpec((k, tn), lambda i, j: (0, j + cb))]
    args = [h, w]
    kern = _mm_kernel
    if residual is not None:
        in_specs.append(pl.BlockSpec((tm, tn), lambda i, j: (i, j)))
        args.append(residual)
        kern = _mm_res_kernel
    return pl.pallas_call(
        kern,
        grid=(s // tm, n // tn),
        in_specs=in_specs,
        out_specs=pl.BlockSpec((tm, tn), lambda i, j: (i, j)),
        out_shape=jax.ShapeDtypeStruct((s, n), out_dtype),
        compiler_params=_params("parallel", "arbitrary"),
        name="matmul",
    )(*args)


def _mm_shift_kernel(h_ref, wa_ref, wb_ref, o_ref, w_ref, *, shift):
    @pl.when(pl.program_id(1) == 0)
    def _():
        w_ref[...] = jnp.concatenate([wa_ref[:, shift:], wb_ref[:, :shift]], axis=1).astype(bf16)

    o_ref[...] = jnp.dot(h_ref[...], w_ref[...], preferred_element_type=f32).astype(o_ref.dtype)


def _matmul_unaligned(h, w, col0, n, out_dtype, tm=2048, tn=512):
    s, k = h.shape
    tm = min(tm, s)
    cb, shift = divmod(col0, tn)
    last = pl.cdiv(w.shape[1], tn) - 1
    assert shift > 0 and n % tn == 0 and s % tm == 0 and col0 + n <= w.shape[1]
    return pl.pallas_call(
        functools.partial(_mm_shift_kernel, shift=shift),
        grid=(n // tn, s // tm),
        in_specs=[pl.BlockSpec((tm, k), lambda j, i: (i, 0)),
                  pl.BlockSpec((k, tn), lambda j, i: (0, cb + j)),
                  pl.BlockSpec((k, tn), lambda j, i: (0, jnp.minimum(cb + j + 1, last)))],
        out_specs=pl.BlockSpec((tm, tn), lambda j, i: (i, j)),
        out_shape=jax.ShapeDtypeStruct((s, n), out_dtype),
        scratch_shapes=[pltpu.VMEM((k, tn), bf16)],
        compiler_params=_params("parallel", "arbitrary"),
        name="matmul_unaligned",
    )(h, w, w)


def _dt_kernel(h_ref, w_ref, dt_ref, dtt_ref):
    n = dt_ref.shape[1]
    h = h_ref[...]
    w = w_ref[...].astype(bf16)
    dt_ref[...] = jnp.dot(h, w, preferred_element_type=f32)[:, :n]
    dtt_ref[...] = lax.dot_general(w, h, (((0,), (1,)), ((), ())), preferred_element_type=f32)[:n, :]


def _dt_proj(h, w, col0, n, tm=1024):
    s, k = h.shape
    tm = min(tm, s)
    assert col0 % LANES == 0 and n <= LANES
    return pl.pallas_call(
        _dt_kernel,
        grid=(s // tm,),
        in_specs=[pl.BlockSpec((tm, k), lambda i: (i, 0)),
                  pl.BlockSpec((k, LANES), lambda i: (0, col0 // LANES))],
        out_specs=[pl.BlockSpec((tm, n), lambda i: (i, 0)),
                   pl.BlockSpec((n, tm), lambda i: (0, i))],
        out_shape=[jax.ShapeDtypeStruct((s, n), f32), jax.ShapeDtypeStruct((n, s), f32)],
        compiler_params=_params("parallel"),
        name="dt_proj",
    )(h, w)


def _conv_kernel(prev_ref, main_ref, next_ref, w_ref, b_ref, o_ref):
    i = pl.program_id(0)
    last = pl.num_programs(0) - 1
    tb = main_ref.shape[0]
    halo = prev_ref.shape[0]
    pv = jnp.where(i > 0, prev_ref[...].astype(f32), 0.0)
    nx = jnp.where(i < last, next_ref[...].astype(f32), 0.0)
    ext = jnp.concatenate([pv, main_ref[...].astype(f32), nx], axis=0)
    acc = b_ref[...] + jnp.zeros((tb, main_ref.shape[1]), f32)
    for k in range(CONV_K):
        off = halo + k - CONV_K // 2
        acc = acc + w_ref[k:k + 1, :] * ext[off:off + tb]
    o_ref[...] = (acc * _sigmoid(acc)).astype(o_ref.dtype)


def _conv_silu(zx, conv_w, conv_b, tb=512, tc=1024):
    s = zx.shape[0]
    tb = min(tb, s)
    halo = BF16_ROWS
    rb = tb // halo
    nhalo = s // halo
    cb = D_SSD // tc
    return pl.pallas_call(
        _conv_kernel,
        grid=(s // tb, CONV_CH // tc),
        in_specs=[
            pl.BlockSpec((halo, tc), lambda i, j: (jnp.maximum(i * rb - 1, 0), j + cb)),
            pl.BlockSpec((tb, tc), lambda i, j: (i, j + cb)),
            pl.BlockSpec((halo, tc), lambda i, j: (jnp.minimum((i + 1) * rb, nhalo - 1), j + cb)),
            pl.BlockSpec((CONV_K, tc), lambda i, j: (0, j)),
            pl.BlockSpec((1, tc), lambda i, j: (0, j)),
        ],
        out_specs=pl.BlockSpec((tb, tc), lambda i, j: (i, j)),
        out_shape=jax.ShapeDtypeStruct((s, CONV_CH), bf16),
        compiler_params=_params("parallel", "parallel"),
        name="conv_silu",
    )(zx, zx, zx, conv_w, conv_b.reshape(1, CONV_CH))


def _ssd_direction(xs, bm, cm, dt_raw, dtt_raw, bias, bias_t, a_log, a_log_t, expand,
                   state_ref, reverse):
    L = CHUNK
    dt = _softplus(dt_raw + bias)
    dtt = _softplus(dtt_raw + bias_t)
    a = dt * (-jnp.exp(a_log))
    at = dtt * (-jnp.exp(a_log_t))
    ri = lax.broadcasted_iota(jnp.int32, (L, L), 0)
    ci = lax.broadcasted_iota(jnp.int32, (L, L), 1)
    causal = (ri <= ci) if reverse else (ri >= ci)
    cum_l = jnp.where(causal, 1.0, 0.0).astype(f32)
    cum_r = jnp.where((ri >= ci) if reverse else (ri <= ci), 1.0, 0.0).astype(f32)
    acs = jnp.dot(cum_l, a, preferred_element_type=f32, precision=lax.Precision.HIGHEST)
    acst = jnp.dot(at, cum_r, preferred_element_type=f32, precision=lax.Precision.HIGHEST)
    edge = 0 if reverse else L - 1
    acs_end = acs[edge:edge + 1, :]
    small = jnp.concatenate([dt, jnp.exp(acs_end - acs), jnp.exp(acs)], axis=0)
    small_hi = small.astype(bf16)
    small_lo = (small - small_hi.astype(f32)).astype(bf16)
    wide = (jnp.dot(small_hi, expand, preferred_element_type=f32)
            + jnp.dot(small_lo, expand, preferred_element_type=f32))
    dtx, dinx, eacsx = wide[:L], wide[L:2 * L], wide[2 * L:]
    xdt = xs * dtx
    xdt_b = xdt.astype(bf16)
    xdec_b = (xdt * dinx).astype(bf16)
    cdec = eacsx[edge:edge + 1, :]
    lane = lax.broadcasted_iota(jnp.int32, (L, LANES), 1)
    ys = []
    for g in range(SSD_GROUPS):
        bg = bm[:, g * SSD_STATE:(g + 1) * SSD_STATE]
        cg = cm[:, g * SSD_STATE:(g + 1) * SSD_STATE]
        gs = slice(g * GROUP_COLS, (g + 1) * GROUP_COLS)
        cb = lax.dot_general(cg, bg, NT, preferred_element_type=f32)
        h_in = state_ref[:, gs]
        y_off = jnp.dot(cg, h_in.astype(bf16), preferred_element_type=f32) * eacsx[:, gs]
        st = lax.dot_general(bg, xdec_b[:, gs], TN, preferred_element_type=f32)
        state_ref[:, gs] = h_in * cdec[:, gs] + st
        tiles = []
        for pair in range(GROUP_COLS // LANES):
            ms = []
            for sub in range(2):
                h = g * (SSD_HEADS // SSD_GROUPS) + pair * 2 + sub
                seg = acs[:, h:h + 1] - acst[h:h + 1, :]
                ms.append((jnp.where(causal, jnp.exp(seg), 0.0) * cb).astype(bf16))
            col = g * GROUP_COLS + pair * LANES
            xp = xdt_b[:, col:col + LANES]
            rhs = jnp.concatenate([jnp.where(lane < SSD_HEAD_DIM, xp, jnp.zeros_like(xp)),
                                   jnp.where(lane >= SSD_HEAD_DIM, xp, jnp.zeros_like(xp))], axis=0)
            tiles.append(jnp.dot(jnp.concatenate(ms, axis=1), rhs, preferred_element_type=f32))
        ys.append(jnp.concatenate(tiles, axis=1) + y_off)
    return jnp.concatenate(ys, axis=1)


def _ssd_kernel(xf_ref, bf_ref, cf_ref, dtf_ref, dttf_ref,
                xb_ref, bb_ref, cb_ref, dtb_ref, dttb_ref,
                bias_ref, biast_ref, alog_ref, alogt_ref, expand_ref,
                yf_ref, yb_ref, sf_ref, sb_ref):
    @pl.when(pl.program_id(0) == 0)
    def _():
        sf_ref[...] = jnp.zeros_like(sf_ref)
        sb_ref[...] = jnp.zeros_like(sb_ref)

    H = SSD_HEADS
    expand = expand_ref[...]
    yf_ref[...] = _ssd_direction(
        xf_ref[...].astype(f32), bf_ref[...], cf_ref[...],
        dtf_ref[:, :H], dttf_ref[:H, :], bias_ref[:, :H], biast_ref[:H, :],
        alog_ref[:, :H], alogt_ref[:H, :], expand, sf_ref, reverse=False)
    yb_ref[...] = _ssd_direction(
        xb_ref[...].astype(f32), bb_ref[...], cb_ref[...],
        dtb_ref[:, H:], dttb_ref[H:, :], bias_ref[:, H:], biast_ref[H:, :],
        alog_ref[:, H:], alogt_ref[H:, :], expand, sb_ref, reverse=True)


def _ssd_scan(xbc, dt, dtt, dt_bias, a_log):
    s = xbc.shape[0]
    nc = s // CHUNK
    gn = SSD_GROUPS * SSD_STATE
    bcol = D_SSD // gn
    fwd = lambda c: c
    bwd = lambda c: nc - 1 - c

    def chunk_specs(sel):
        return [pl.BlockSpec((CHUNK, D_SSD), lambda c: (sel(c), 0)),
                pl.BlockSpec((CHUNK, gn), lambda c: (sel(c), bcol)),
                pl.BlockSpec((CHUNK, gn), lambda c: (sel(c), bcol + 1)),
                pl.BlockSpec((CHUNK, 2 * SSD_HEADS), lambda c: (sel(c), 0)),
                pl.BlockSpec((2 * SSD_HEADS, CHUNK), lambda c: (0, sel(c)))]

    const = lambda shape: pl.BlockSpec(shape, lambda c: (0, 0))
    expand = (jnp.arange(D_SSD)[None, :] // SSD_HEAD_DIM == jnp.arange(SSD_HEADS)[:, None]).astype(bf16)
    return pl.pallas_call(
        _ssd_kernel,
        grid=(nc,),
        in_specs=chunk_specs(fwd) + chunk_specs(bwd) + [
            const((1, 2 * SSD_HEADS)), const((2 * SSD_HEADS, 1)),
            const((1, 2 * SSD_HEADS)), const((2 * SSD_HEADS, 1)),
            const((SSD_HEADS, D_SSD))],
        out_specs=[pl.BlockSpec((CHUNK, D_SSD), lambda c: (fwd(c), 0)),
                   pl.BlockSpec((CHUNK, D_SSD), lambda c: (bwd(c), 0))],
        out_shape=[jax.ShapeDtypeStruct((s, D_SSD), f32)] * 2,
        scratch_shapes=[pltpu.VMEM((SSD_STATE, D_SSD), f32)] * 2,
        compiler_params=_params("arbitrary"),
        name="ssd_scan",
    )(xbc, xbc, xbc, dt, dtt, xbc, xbc, xbc, dt, dtt,
      dt_bias.reshape(1, -1), dt_bias.reshape(-1, 1), a_log.reshape(1, -1), a_log.reshape(-1, 1),
      expand)


def _ssd_post_kernel(yf_ref, yb_ref, xs_ref, z_ref, d_ref, g_ref, o_ref):
    z = z_ref[...].astype(f32)
    y = (yf_ref[...] + yb_ref[...] + d_ref[...] * xs_ref[...].astype(f32)) * (z * _sigmoid(z))
    ms = jnp.mean(y * y, axis=-1, keepdims=True)
    o_ref[...] = (y * lax.rsqrt(ms + EPS) * g_ref[...]).astype(o_ref.dtype)


def _ssd_post(y_f, y_b, xbc, zx, d_skip, g, tb=512):
    s = y_f.shape[0]
    tb = min(tb, s)
    row = pl.BlockSpec((tb, D_SSD), lambda i: (i, 0))
    vec = pl.BlockSpec((1, D_SSD), lambda i: (0, 0))
    return pl.pallas_call(
        _ssd_post_kernel,
        grid=(s // tb,),
        in_specs=[row, row, row, row, vec, vec],
        out_specs=row,
        out_shape=jax.ShapeDtypeStruct((s, D_SSD), bf16),
        compiler_params=_params("parallel"),
        name="ssd_post",
    )(y_f, y_b, xbc, zx, jnp.repeat(d_skip, SSD_HEAD_DIM).reshape(1, D_SSD), g.reshape(1, D_SSD))


def _rope_table_kernel(pos_ref, inv_ref, sign_ref, cos_ref, sin_ref):
    ang = pos_ref[...].astype(f32) * inv_ref[...]
    cos_ref[...] = jnp.cos(ang)
    sin_ref[...] = jnp.sin(ang) * sign_ref[...]


def _rope_tables(positions, tb=512):
    s = positions.shape[0]
    tb = min(tb, s)
    half = ATT_HEAD_DIM // 2
    inv = ROPE_THETA ** (-jnp.arange(0, ATT_HEAD_DIM, 2, dtype=f32) / ATT_HEAD_DIM)
    inv_t = jnp.tile(inv, LANES // half).reshape(1, LANES)
    sign = jnp.tile(jnp.concatenate([-jnp.ones((half,), f32), jnp.ones((half,), f32)]),
                    LANES // ATT_HEAD_DIM).reshape(1, LANES)
    vec = pl.BlockSpec((1, LANES), lambda i: (0, 0))
    row = pl.BlockSpec((tb, LANES), lambda i: (i, 0))
    return pl.pallas_call(
        _rope_table_kernel,
        grid=(s // tb,),
        in_specs=[pl.BlockSpec((tb, 1), lambda i: (i, 0)), vec, vec],
        out_specs=[row, row],
        out_shape=[jax.ShapeDtypeStruct((s, LANES), f32)] * 2,
        compiler_params=_params("parallel"),
        name="rope_tables",
    )(positions.reshape(s, 1), inv_t, sign)


def _qk_kernel(x_ref, g_ref, cos_ref, sin_ref, o_ref):
    which = pl.program_id(1)
    scale = jnp.where(which == 0, ATT_HEAD_DIM ** -0.5 * LOG2E, 1.0).astype(f32)
    g = g_ref[0] * scale
    cos = cos_ref[...]
    sin = sin_ref[...]
    r = lax.broadcasted_iota(jnp.int32, (LANES, LANES), 0) // ATT_HEAD_DIM
    c = lax.broadcasted_iota(jnp.int32, (LANES, LANES), 1) // ATT_HEAD_DIM
    seg = jnp.where(r == c, 1.0, 0.0).astype(bf16)
    lane = lax.broadcasted_iota(jnp.int32, cos.shape, 1)
    first_half = (lane % ATT_HEAD_DIM) < ATT_HEAD_DIM // 2
    for t in range(x_ref.shape[1] // LANES):
        x = x_ref[:, t * LANES:(t + 1) * LANES].astype(f32)
        sq = x * x
        hi = sq.astype(bf16)
        lo = (sq - hi.astype(f32)).astype(bf16)
        ss = (jnp.dot(hi, seg, preferred_element_type=f32)
              + jnp.dot(lo, seg, preferred_element_type=f32))
        xn = x * lax.rsqrt(ss * (1.0 / ATT_HEAD_DIM) + EPS) * g
        swapped = jnp.where(first_half,
                            pltpu.roll(xn, LANES - ATT_HEAD_DIM // 2, axis=1),
                            pltpu.roll(xn, ATT_HEAD_DIM // 2, axis=1))
        o_ref[:, t * LANES:(t + 1) * LANES] = (xn * cos + swapped * sin).astype(o_ref.dtype)


def _qk_prep(qkvg, q_g, k_g, cos_t, sin_t, tb=512):
    s = qkvg.shape[0]
    tb = min(tb, s)
    cols = 2 * ATT_HEADS * ATT_HEAD_DIM
    g2 = jnp.stack([jnp.tile(q_g, 2), jnp.tile(k_g, 2)]).reshape(2, 1, LANES)
    tab = pl.BlockSpec((tb, LANES), lambda i, j: (i, 0))
    return pl.pallas_call(
        _qk_kernel,
        grid=(s // tb, 2),
        in_specs=[pl.BlockSpec((tb, cols), lambda i, j: (i, j)),
                  pl.BlockSpec((1, 1, LANES), lambda i, j: (j, 0, 0)),
                  tab, tab],
        out_specs=pl.BlockSpec((tb, cols), lambda i, j: (i, j)),
        out_shape=jax.ShapeDtypeStruct((s, 2 * cols), bf16),
        compiler_params=_params("parallel", "parallel"),
        name="qk_prep",
    )(qkvg, g2, cos_t, sin_t)


def _attn_kernel(q_ref, k_ref, v_ref, lq1_ref, lk1_ref, lq2_ref, lk2_ref, g_ref, o_ref,
                 qm_ref, s_ref, p_ref, m_ref, l_ref, acc_ref, *, tk, strip, lam_init):
    tq = q_ref.shape[0]
    nk = k_ref.shape[0] // tk
    q = q_ref[...]
    lane = lax.broadcasted_iota(jnp.int32, q.shape, 1)
    zero = jnp.zeros_like(q)
    qm_ref[0] = jnp.where(lane < ATT_HEAD_DIM, q, zero)
    qm_ref[1] = jnp.where(lane >= ATT_HEAD_DIM, q, zero)
    m_ref[...] = jnp.full(m_ref.shape, -1e30, f32)
    l_ref[...] = jnp.zeros_like(l_ref)
    acc_ref[...] = jnp.zeros_like(acc_ref)

    def scores(c, slot):
        kc = k_ref[pl.ds(pl.multiple_of(c * tk, tk), tk), :]
        for half in range(2):
            s_ref[slot, half] = lax.dot_general(qm_ref[half], kc, NT, preferred_element_type=f32)

    def absorb(c, slot):
        vc = v_ref[pl.ds(pl.multiple_of(c * tk, tk), tk), :]
        for half in range(2):
            for r0 in range(0, tq, strip):
                rows = slice(r0, r0 + strip)
                s = s_ref[slot, half, rows, :]
                mx = s[:, :LANES]
                for t in range(1, tk // LANES):
                    mx = jnp.maximum(mx, s[:, t * LANES:(t + 1) * LANES])
                m_old = m_ref[half, rows, :]
                m_new = jnp.maximum(m_old, jnp.max(mx, axis=-1, keepdims=True))
                alpha = jnp.exp2(m_old - m_new)
                p = jnp.exp2(s - m_new)
                ps = p[:, :LANES]
                for t in range(1, tk // LANES):
                    ps = ps + p[:, t * LANES:(t + 1) * LANES]
                l_ref[half, rows, :] = alpha * l_ref[half, rows, :] + ps
                acc_ref[half, rows, :] = alpha * acc_ref[half, rows, :]
                m_ref[half, rows, :] = m_new
                p_ref[half, rows, :] = p.astype(bf16)
            acc_ref[half] += jnp.dot(p_ref[half], vc, preferred_element_type=f32)

    scores(0, 0)

    def pair(i, carry):
        c = 2 * i
        scores(c + 1, 1)
        absorb(c, 0)
        scores(jnp.minimum(c + 2, nk - 1), 0)
        absorb(c + 1, 1)
        return carry

    lax.fori_loop(0, nk // 2, pair, 0)
    lam = (jnp.exp(jnp.sum(lq1_ref[...] * lk1_ref[...], axis=-1, keepdims=True))
           - jnp.exp(jnp.sum(lq2_ref[...] * lk2_ref[...], axis=-1, keepdims=True)) + lam_init)
    o1 = acc_ref[0] / jnp.sum(l_ref[0], axis=-1, keepdims=True)
    o2 = acc_ref[1] / jnp.sum(l_ref[1], axis=-1, keepdims=True)
    o = o1 - lam * o2
    ms = jnp.mean(o * o, axis=-1, keepdims=True)
    o_ref[...] = (o * lax.rsqrt(ms + EPS) * (g_ref[...] * (1.0 - lam_init))).astype(o_ref.dtype)


def _diff_attention(qk, qkvg, lam_q1, lam_k1, lam_q2, lam_k2, subln_g, lam_init,
                    tq=512, tk=512, strip=64):
    s = qk.shape[0]
    tq = min(tq, s)
    tk = min(tk, s // 2)
    assert (s // tk) % 2 == 0 and tq % strip == 0
    kcol = 2 * ATT_HEADS * ATT_HEAD_DIM // LANES
    vcol = 2 * kcol
    vec = lambda n: pl.BlockSpec((1, n), lambda h, i: (0, 0))
    return pl.pallas_call(
        functools.partial(_attn_kernel, tk=tk, strip=strip, lam_init=lam_init),
        grid=(ATT_HEADS, s // tq),
        in_specs=[pl.BlockSpec((tq, LANES), lambda h, i: (i, h)),
                  pl.BlockSpec((s, LANES), lambda h, i: (0, kcol + h)),
                  pl.BlockSpec((s, LANES), lambda h, i: (0, vcol + h)),
                  vec(ATT_HEAD_DIM), vec(ATT_HEAD_DIM), vec(ATT_HEAD_DIM), vec(ATT_HEAD_DIM),
                  vec(ATT_V_DIM)],
        out_specs=pl.BlockSpec((tq, LANES), lambda h, i: (i, h)),
        out_shape=jax.ShapeDtypeStruct((s, ATT_HEADS * ATT_V_DIM), bf16),
        scratch_shapes=[pltpu.VMEM((2, tq, LANES), bf16),
                        pltpu.VMEM((2, 2, tq, tk), f32),
                        pltpu.VMEM((2, tq, tk), bf16),
                        pltpu.VMEM((2, tq, 1), f32),
                        pltpu.VMEM((2, tq, LANES), f32),
                        pltpu.VMEM((2, tq, ATT_V_DIM), f32)],
        compiler_params=_params("parallel", "arbitrary"),
        name="diff_attention",
    )(qk, qk, qkvg, lam_q1.reshape(1, -1), lam_k1.reshape(1, -1), lam_q2.reshape(1, -1),
      lam_k2.reshape(1, -1), subln_g.reshape(1, -1))


def _merge_kernel(ys_ref, ya_ref, wa_ref, wb_ref, gs_ref, ga_ref, o_ref):
    a = jnp.dot(ys_ref[...], wa_ref[...].astype(bf16), preferred_element_type=f32)
    b = jnp.dot(ya_ref[...], wb_ref[...].astype(bf16), preferred_element_type=f32)
    mixed = _sigmoid(gs_ref[...].astype(f32)) * a + _sigmoid(ga_ref[...].astype(f32)) * b
    o_ref[...] = mixed.astype(o_ref.dtype)


def _merge(y_ssd, y_att, w_a, w_b, qkvg, tm=2048, tn=256):
    s = y_ssd.shape[0]
    tm = min(tm, s)
    gcol = 3 * 2 * ATT_HEADS * ATT_HEAD_DIM // tn
    nj = D_MODEL // tn
    row = pl.BlockSpec((tm, D_MODEL), lambda i, j: (i, 0))
    wcol = pl.BlockSpec((D_MODEL, tn), lambda i, j: (0, j))
    return pl.pallas_call(
        _merge_kernel,
        grid=(s // tm, nj),
        in_specs=[row, row, wcol, wcol,
                  pl.BlockSpec((tm, tn), lambda i, j: (i, gcol + j)),
                  pl.BlockSpec((tm, tn), lambda i, j: (i, gcol + nj + j))],
        out_specs=pl.BlockSpec((tm, tn), lambda i, j: (i, j)),
        out_shape=jax.ShapeDtypeStruct((s, D_MODEL), bf16),
        compiler_params=_params("parallel", "arbitrary"),
        name="merge",
    )(y_ssd, y_att, w_a, w_b, qkvg, qkvg)


def _take_topk(s, k, rank, payloads):
    big = jnp.int32(2 ** 30)
    vals, ranks, picked = [], [], [[] for _ in payloads]
    for _ in range(k):
        m = jnp.max(s, axis=0, keepdims=True)
        pos = jnp.min(jnp.where(s == m, rank, big), axis=0, keepdims=True)
        hit = rank == pos
        vals.append(m)
        ranks.append(pos)
        for out, pay in zip(picked, payloads):
            out.append(jnp.max(jnp.where(hit, pay, -1.0), axis=0, keepdims=True))
        s = jnp.where(hit, -jnp.inf, s)
    cat = lambda xs: jnp.concatenate(xs, axis=0)
    return cat(vals), cat(ranks), [cat(p) for p in picked]


def _pair_candidates(v1, i1, v2, i2):
    k = PEER_TOPK
    tb = v1.shape[1]
    row8 = lax.broadcasted_iota(jnp.int32, (8, tb), 0)
    neg = jnp.float32(-jnp.inf)
    sums, flat, c1, c2 = [], [], [], []
    for a in range(k // 2):
        nb = k // (a + 1)
        for b0 in range(0, nb, 8):
            piece = v1[a:a + 1] + v2[b0:b0 + 8]
            if nb - b0 < 8:
                piece = jnp.where(row8 < nb - b0, piece, neg)
            sums.append(piece)
            flat.append(row8 + (a * k + b0))
            c1.append(jnp.broadcast_to(i1[a:a + 1], (8, tb)))
            c2.append(i2[b0:b0 + 8])
    sums.append(v1[k // 2:] + v2[0:1])
    flat.append((row8 + k // 2) * k)
    c1.append(i1[k // 2:])
    c2.append(jnp.broadcast_to(i2[0:1], (8, tb)))
    cat = lambda xs: jnp.concatenate(xs, axis=0)
    return cat(sums), cat(flat), cat(c1), cat(c2)


def _topk_kernel(q_ref, keys_ref, i1_ref, i2_ref, gate_ref):
    tb = q_ref.shape[0]
    kk = PEER_TOPK
    key_iota = lax.broadcasted_iota(jnp.int32, (PEER_NKEYS, tb), 0)
    i1s, i2s, gates = [], [], []
    for h in range(PEER_HEADS):
        tops = []
        for half in range(2):
            c0 = (h * 2 + half) * PEER_NKEYS
            qh = q_ref[:, c0:c0 + PEER_NKEYS].astype(bf16)
            keys = keys_ref[h, half].astype(bf16)
            s = lax.dot_general(keys, qh, NT, preferred_element_type=f32)
            vals, idx, _ = _take_topk(s, kk, key_iota, [])
            tops.append((vals, idx.astype(f32)))
        (v1, i1), (v2, i2) = tops
        cand, flat, c1, c2 = _pair_candidates(v1, i1, v2, i2)
        sc, _, (e1, e2) = _take_topk(cand, kk, flat, [c1, c2])
        e = jnp.exp(sc - jnp.max(sc, axis=0, keepdims=True))
        gates.append(e / jnp.sum(e, axis=0, keepdims=True))
        i1s.append(e1)
        i2s.append(e2)
    i1_ref[...] = jnp.concatenate(i1s, axis=0).T
    i2_ref[...] = jnp.concatenate(i2s, axis=0).T
    gate_ref[...] = jnp.concatenate(gates, axis=0).T


def _peer_topk(qp, keys, tb=256):
    s = qp.shape[0]
    tb = min(tb, s)
    nsel = PEER_HEADS * PEER_TOPK
    row = pl.BlockSpec((tb, nsel), lambda i: (i, 0))
    return pl.pallas_call(
        _topk_kernel,
        grid=(s // tb,),
        in_specs=[pl.BlockSpec((tb, qp.shape[1]), lambda i: (i, 0)),
                  pl.BlockSpec(keys.shape, lambda i: (0, 0, 0, 0))],
        out_specs=[row, row, row],
        out_shape=[jax.ShapeDtypeStruct((s, nsel), f32)] * 3,
        compiler_params=_params("parallel"),
        name="peer_topk",
    )(qp, keys)


ROUTE_PITCH = PEER_NKEYS + 8


def _route_kernel(i1_ref, i2_ref, gate_ref, w_ref, tile_ref):
    n = PEER_NKEYS
    tb = i1_ref.shape[0]
    sub = lax.broadcasted_iota(jnp.int32, (n, i1_ref.shape[1]), 0).astype(f32)

    def body(t, carry):
        i1 = i1_ref[pl.ds(t, 1), :]
        i2 = i2_ref[pl.ds(t, 1), :]
        g = gate_ref[pl.ds(t, 1), :]
        a = jnp.where(sub == i1, g, 0.0).astype(bf16)
        b = jnp.where(sub == i2, 1.0, 0.0).astype(bf16)
        tile_ref[pl.ds(pl.multiple_of(t * ROUTE_PITCH, 8), n), :] = lax.dot_general(
            a, b, NT, preferred_element_type=f32)
        return carry

    lax.fori_loop(0, tb, body, 0, unroll=32)
    for a in range(n):
        w_ref[:, a * n:(a + 1) * n] = tile_ref[pl.ds(a, tb, stride=ROUTE_PITCH), :].astype(w_ref.dtype)


def _peer_route(i1, i2, gate, tb=256):
    s, nsel = i1.shape
    tb = min(tb, s)
    row = pl.BlockSpec((tb, nsel), lambda i: (i, 0))
    return pl.pallas_call(
        _route_kernel,
        grid=(s // tb,),
        in_specs=[row, row, row],
        out_specs=pl.BlockSpec((tb, PEER_EXPERTS), lambda i: (i, 0)),
        out_shape=jax.ShapeDtypeStruct((s, PEER_EXPERTS), bf16),
        scratch_shapes=[pltpu.VMEM((tb * ROUTE_PITCH, PEER_NKEYS), f32)],
        compiler_params=_params("parallel"),
        name="peer_route",
    )(i1, i2, gate)


def _peer_dense_kernel(h_ref, u0_ref, ub_ref, un_ref, v_ref, w_ref, o_ref, a_ref, g_ref, *, strip):
    e = pl.program_id(1)
    te = ub_ref.shape[0]
    h = h_ref[...]

    def mix(slot, lo):
        for r0 in range(0, h_ref.shape[0], strip):
            rows = slice(r0, r0 + strip)
            a = a_ref[slot, rows, :]
            act = 0.5 * a * (1.0 + lax.erf(a * (2.0 ** -0.5)))
            g_ref[slot, rows, :] = (act * w_ref[rows, lo:lo + te].astype(f32)).astype(bf16)
        return jnp.dot(g_ref[slot], v_ref[lo:lo + te, :], preferred_element_type=f32)

    @pl.when(e == 0)
    def _():
        a_ref[0] = lax.dot_general(h, u0_ref[...], NT, preferred_element_type=f32)
        o_ref[...] = jnp.zeros_like(o_ref)

    a_ref[1] = lax.dot_general(h, ub_ref[...], NT, preferred_element_type=f32)
    o_ref[...] += mix(0, 0)
    a_ref[0] = lax.dot_general(h, un_ref[...], NT, preferred_element_type=f32)
    o_ref[...] += mix(1, te)


def _peer_dense(h, u, v, w, tm=1024, te=256):
    s, d = h.shape
    tm = min(tm, s)
    nt = u.shape[0] // te
    utile = lambda f: pl.BlockSpec((te, d), lambda i, e: (f(e), 0))
    return pl.pallas_call(
        functools.partial(_peer_dense_kernel, strip=min(128, tm)),
        grid=(s // tm, nt // 2),
        in_specs=[pl.BlockSpec((tm, d), lambda i, e: (i, 0)),
                  utile(lambda e: 0),
                  utile(lambda e: 2 * e + 1),
                  utile(lambda e: jnp.minimum(2 * e + 2, nt - 1)),
                  pl.BlockSpec((2 * te, d), lambda i, e: (e, 0)),
                  pl.BlockSpec((tm, 2 * te), lambda i, e: (i, e))],
        out_specs=pl.BlockSpec((tm, d), lambda i, e: (i, 0)),
        out_shape=jax.ShapeDtypeStruct((s, d), f32),
        scratch_shapes=[pltpu.VMEM((2, tm, te), f32), pltpu.VMEM((2, tm, te), bf16)],
        compiler_params=_params("parallel", "arbitrary"),
        name="peer_dense",
    )(h, u, u, u, v, w)


def _ple_kernel(h_ref, wg_ref, p_ref, wp_ref, x_ref, o_ref):
    gate = _sigmoid(jnp.dot(h_ref[...], wg_ref[...].astype(bf16), preferred_element_type=f32))
    up = jnp.dot(p_ref[...].astype(bf16), wp_ref[...].astype(bf16), preferred_element_type=f32)
    o_ref[...] = x_ref[...] + gate * up


def _ple(h, w_gate, p, w_up, x, tm=2048, tn=512):
    s = h.shape[0]
    tm = min(tm, s)
    return pl.pallas_call(
        _ple_kernel,
        grid=(s // tm, D_MODEL // tn),
        in_specs=[pl.BlockSpec((tm, D_MODEL), lambda i, j: (i, 0)),
                  pl.BlockSpec((D_MODEL, tn), lambda i, j: (0, j)),
                  pl.BlockSpec((tm, PLE_DIM), lambda i, j: (i, 0)),
                  pl.BlockSpec((PLE_DIM, tn), lambda i, j: (0, j)),
                  pl.BlockSpec((tm, tn), lambda i, j: (i, j))],
        out_specs=pl.BlockSpec((tm, tn), lambda i, j: (i, j)),
        out_shape=jax.ShapeDtypeStruct((s, D_MODEL), f32),
        compiler_params=_params("parallel", "arbitrary"),
        name="ple",
    )(h, w_gate, p, w_up, x)


def _layer(i, x, p, pos, norm1_g, w_in, conv_w, conv_b, a_log_f, a_log_b, dt_bias_f, dt_bias_b,
           d_skip, ssd_norm_g, q_norm_g, k_norm_g, lam_q1, lam_k1, lam_q2, lam_k2, subln_g,
           w_ssd_br, w_att_br, w_out, norm2_g, peer_wq, peer_keys, peer_u, peer_v, norm3_g,
           ple_gate_w, ple_up_w):
    h = _rms_norm(x, norm1_g)
    zx = _matmul(h, w_in, 0, ZX_COLS, bf16)
    dt, dtt = _dt_proj(h, w_in, DT_COL0, 2 * SSD_HEADS)
    qkvg = _matmul_unaligned(h, w_in, REST_COL0, w_in.shape[1] - REST_COL0, bf16)

    xbc = _conv_silu(zx, conv_w, conv_b)
    y_f, y_b = _ssd_scan(xbc, dt, dtt, jnp.concatenate([dt_bias_f, dt_bias_b]),
                         jnp.concatenate([a_log_f, a_log_b]))
    y_ssd = _ssd_post(y_f, y_b, xbc, zx, d_skip, ssd_norm_g)

    cos_t, sin_t = _rope_tables(pos)
    qk = _qk_prep(qkvg, q_norm_g, k_norm_g, cos_t, sin_t)
    lam_init = 0.8 - 0.6 * math.exp(-0.3 * i)
    y_att = _diff_attention(qk, qkvg, lam_q1, lam_k1, lam_q2, lam_k2, subln_g, lam_init)

    mixed = _merge(y_ssd, y_att, w_ssd_br, w_att_br, qkvg)
    x = _matmul(mixed, w_out, 0, D_MODEL, f32, residual=x)

    h2 = _rms_norm(x, norm2_g)
    qp = _matmul(h2, peer_wq, 0, peer_wq.shape[1], f32)
    i1, i2, gate = _peer_topk(qp, peer_keys)
    w = _peer_route(i1, i2, gate)
    mix = _peer_dense(h2, peer_u.astype(bf16), peer_v.astype(bf16), w)
    x, h3 = _add_rms_norm(x, mix, norm3_g)

    return _ple(h3, ple_gate_w, p, ple_up_w, x)


def kernel(x, p, positions, norm1_g, w_in, conv_w, conv_b, a_log_f, a_log_b, dt_bias_f, dt_bias_b, d_skip, ssd_norm_g, q_norm_g, k_norm_g, lam_q1, lam_k1, lam_q2, lam_k2, subln_g, w_ssd_br, w_att_br, w_out, norm2_g, peer_wq, peer_keys, peer_u, peer_v, norm3_g, ple_gate_w, ple_up_w):
    batch, seq, d = x.shape
    depth = w_in.shape[0]
    outs = []
    for b in range(batch):
        xb = x[b]
        for i in range(depth):
            xb = _layer(i, xb, p[i, b], positions[b], norm1_g[i], w_in[i], conv_w[i], conv_b[i],
                        a_log_f[i], a_log_b[i], dt_bias_f[i], dt_bias_b[i], d_skip[i],
                        ssd_norm_g[i], q_norm_g[i], k_norm_g[i], lam_q1[i], lam_k1[i], lam_q2[i],
                        lam_k2[i], subln_g[i], w_ssd_br[i], w_att_br[i], w_out[i], norm2_g[i],
                        peer_wq[i], peer_keys[i], peer_u[i], peer_v[i], norm3_g[i],
                        ple_gate_w[i], ple_up_w[i])
        outs.append(xb)
    return jnp.stack(outs)
```

```python
import functools
import math

import jax
import jax.numpy as jnp
from jax import lax
from jax.experimental import pallas as pl
from jax.experimental.pallas import tpu as pltpu

f32 = jnp.float32
bf16 = jnp.bfloat16

D_MODEL = 2048
D_SSD = 2048
SSD_HEAD_DIM = 64
SSD_HEADS = 32
SSD_GROUPS = 4
SSD_STATE = 128
GROUP_COLS = D_SSD // SSD_GROUPS
CONV_K = 5
CONV_CH = 3072
CHUNK = 128
ATT_HEADS = 16
ATT_HEAD_DIM = 64
ATT_V_DIM = 128
ROPE_THETA = 10000.0
PEER_HEADS = 8
PEER_NKEYS = 128
PEER_EXPERTS = PEER_NKEYS * PEER_NKEYS
PEER_TOPK = 16
PLE_DIM = 256
EPS = 1e-6
LOG2E = math.log2(math.e)

ZX_COLS = D_SSD + CONV_CH
DT_COL0 = ZX_COLS
REST_COL0 = ZX_COLS + 2 * SSD_HEADS
LANES = 128
BF16_ROWS = 16

VMEM_LIMIT = 56 * 1024 * 1024

NT = (((1,), (1,)), ((), ()))
TN = (((0,), (0,)), ((), ()))


def _params(*sem):
    return pltpu.CompilerParams(dimension_semantics=sem, vmem_limit_bytes=VMEM_LIMIT)


def _sigmoid(x):
    return 1.0 / (1.0 + jnp.exp(-x))


def _softplus(x):
    return jnp.maximum(x, 0.0) + jnp.log1p(jnp.exp(-jnp.abs(x)))


def _norm_kernel(x_ref, g_ref, h_ref):
    x = x_ref[...]
    ms = jnp.mean(x * x, axis=-1, keepdims=True)
    h_ref[...] = (x * lax.rsqrt(ms + EPS) * g_ref[...]).astype(h_ref.dtype)


def _rms_norm(x, g, tb=512):
    s, d = x.shape
    tb = min(tb, s)
    return pl.pallas_call(
        _norm_kernel,
        grid=(s // tb,),
        in_specs=[pl.BlockSpec((tb, d), lambda i: (i, 0)),
                  pl.BlockSpec((1, d), lambda i: (0, 0))],
        out_specs=pl.BlockSpec((tb, d), lambda i: (i, 0)),
        out_shape=jax.ShapeDtypeStruct((s, d), bf16),
        compiler_params=_params("parallel"),
        name="rms_norm",
    )(x, g.reshape(1, d))


def _add_norm_kernel(x_ref, d_ref, g_ref, xo_ref, h_ref):
    x = x_ref[...] + d_ref[...]
    xo_ref[...] = x
    ms = jnp.mean(x * x, axis=-1, keepdims=True)
    h_ref[...] = (x * lax.rsqrt(ms + EPS) * g_ref[...]).astype(h_ref.dtype)


def _add_rms_norm(x, delta, g, tb=512):
    s, d = x.shape
    tb = min(tb, s)
    row = pl.BlockSpec((tb, d), lambda i: (i, 0))
    return pl.pallas_call(
        _add_norm_kernel,
        grid=(s // tb,),
        in_specs=[row, row, pl.BlockSpec((1, d), lambda i: (0, 0))],
        out_specs=[row, row],
        out_shape=[jax.ShapeDtypeStruct((s, d), f32), jax.ShapeDtypeStruct((s, d), bf16)],
        compiler_params=_params("parallel"),
        name="add_rms_norm",
    )(x, delta, g.reshape(1, d))


def _mm_kernel(h_ref, w_ref, o_ref):
    o_ref[...] = jnp.dot(h_ref[...], w_ref[...].astype(bf16),
                         preferred_element_type=f32).astype(o_ref.dtype)


def _mm_res_kernel(h_ref, w_ref, r_ref, o_ref):
    o_ref[...] = r_ref[...] + jnp.dot(h_ref[...], w_ref[...].astype(bf16),
                                      preferred_element_type=f32)


def _matmul(h, w, col0, n, out_dtype, residual=None, tm=2048, tn=512):
    s, k = h.shape
    tm = min(tm, s)
    assert col0 % tn == 0 and n % tn == 0 and s % tm == 0
    cb = col0 // tn
    in_specs = [pl.BlockSpec((tm, k), lambda i, j: (i, 0)),
                pl.BlockSpec((k, tn), lambda i, j: (0, j + cb))]
    args = [h, w]
    kern = _mm_kernel
    if residual is not None:
        in_specs.append(pl.BlockSpec((tm, tn), lambda i, j: (i, j)))
        args.append(residual)
        kern = _mm_res_kernel
    return pl.pallas_call(
        kern,
        grid=(s // tm, n // tn),
        in_specs=in_specs,
        out_specs=pl.BlockSpec((tm, tn), lambda i, j: (i, j)),
        out_shape=jax.ShapeDtypeStruct((s, n), out_dtype),
        compiler_params=_params("parallel", "arbitrary"),
        name="matmul",
    )(*args)


def _mm_nt_kernel(h_ref, w_ref, o_ref):
    o_ref[...] = lax.dot_general(h_ref[...], w_ref[...].astype(bf16), NT,
                                 preferred_element_type=f32).astype(o_ref.dtype)


def _matmul_nt(h, w_t, row0, n, out_dtype, tm=2048, tn=512):
    s, k = h.shape
    tm = min(tm, s)
    assert row0 % BF16_ROWS == 0 and n % tn == 0 and s % tm == 0 and row0 + n <= w_t.shape[0]
    return pl.pallas_call(
        _mm_nt_kernel,
        grid=(s // tm, n // tn),
        in_specs=[pl.BlockSpec((tm, k), lambda i, j: (i, 0)),
                  pl.BlockSpec((pl.Element(tn), pl.Element(k)),
                               lambda i, j: ((row0 // 8 + j * (tn // 8)) * 8, 0))],
        out_specs=pl.BlockSpec((tm, tn), lambda i, j: (i, j)),
        out_shape=jax.ShapeDtypeStruct((s, n), out_dtype),
        compiler_params=_params("parallel", "arbitrary"),
        name="matmul_nt",
    )(h, w_t)


def _dt_kernel(h_ref, w_ref, dt_ref, dtt_ref):
    h = h_ref[...]
    w = w_ref[...].astype(bf16)
    dt_ref[...] = lax.dot_general(h, w, NT, preferred_element_type=f32)
    dtt_ref[...] = lax.dot_general(w, h, NT, preferred_element_type=f32)


def _dt_proj(h, w_t, row0, n, tm=1024):
    s, k = h.shape
    tm = min(tm, s)
    assert row0 % n == 0
    return pl.pallas_call(
        _dt_kernel,
        grid=(s // tm,),
        in_specs=[pl.BlockSpec((tm, k), lambda i: (i, 0)),
                  pl.BlockSpec((n, k), lambda i: (row0 // n, 0))],
        out_specs=[pl.BlockSpec((tm, n), lambda i: (i, 0)),
                   pl.BlockSpec((n, tm), lambda i: (0, i))],
        out_shape=[jax.ShapeDtypeStruct((s, n), f32), jax.ShapeDtypeStruct((n, s), f32)],
        compiler_params=_params("parallel"),
        name="dt_proj",
    )(h, w_t)


def _conv_kernel(prev_ref, main_ref, next_ref, w_ref, b_ref, o_ref):
    i = pl.program_id(0)
    last = pl.num_programs(0) - 1
    tb = main_ref.shape[0]
    halo = prev_ref.shape[0]
    pv = jnp.where(i > 0, prev_ref[...].astype(f32), 0.0)
    nx = jnp.where(i < last, next_ref[...].astype(f32), 0.0)
    ext = jnp.concatenate([pv, main_ref[...].astype(f32), nx], axis=0)
    acc = b_ref[...] + jnp.zeros((tb, main_ref.shape[1]), f32)
    for k in range(CONV_K):
        off = halo + k - CONV_K // 2
        acc = acc + w_ref[k:k + 1, :] * ext[off:off + tb]
    o_ref[...] = (acc * _sigmoid(acc)).astype(o_ref.dtype)


def _conv_silu(zx, conv_w, conv_b, tb=512, tc=1024):
    s = zx.shape[0]
    tb = min(tb, s)
    halo = BF16_ROWS
    rb = tb // halo
    nhalo = s // halo
    cb = D_SSD // tc
    return pl.pallas_call(
        _conv_kernel,
        grid=(s // tb, CONV_CH // tc),
        in_specs=[
            pl.BlockSpec((halo, tc), lambda i, j: (jnp.maximum(i * rb - 1, 0), j + cb)),
            pl.BlockSpec((tb, tc), lambda i, j: (i, j + cb)),
            pl.BlockSpec((halo, tc), lambda i, j: (jnp.minimum((i + 1) * rb, nhalo - 1), j + cb)),
            pl.BlockSpec((CONV_K, tc), lambda i, j: (0, j)),
            pl.BlockSpec((1, tc), lambda i, j: (0, j)),
        ],
        out_specs=pl.BlockSpec((tb, tc), lambda i, j: (i, j)),
        out_shape=jax.ShapeDtypeStruct((s, CONV_CH), bf16),
        compiler_params=_params("parallel", "parallel"),
        name="conv_silu",
    )(zx, zx, zx, conv_w, conv_b.reshape(1, CONV_CH))


def _ssd_direction(xs, bm, cm, dt_raw, dtt_raw, bias, bias_t, a_log, a_log_t, expand,
                   state_ref, reverse):
    L = CHUNK
    dt = _softplus(dt_raw + bias)
    dtt = _softplus(dtt_raw + bias_t)
    a = dt * (-jnp.exp(a_log))
    at = dtt * (-jnp.exp(a_log_t))
    ri = lax.broadcasted_iota(jnp.int32, (L, L), 0)
    ci = lax.broadcasted_iota(jnp.int32, (L, L), 1)
    causal = (ri <= ci) if reverse else (ri >= ci)
    cum_l = jnp.where(causal, 1.0, 0.0).astype(f32)
    cum_r = jnp.where((ri >= ci) if reverse else (ri <= ci), 1.0, 0.0).astype(f32)
    acs = jnp.dot(cum_l, a, preferred_element_type=f32, precision=lax.Precision.HIGHEST)
    acst = jnp.dot(at, cum_r, preferred_element_type=f32, precision=lax.Precision.HIGHEST)
    edge = 0 if reverse else L - 1
    acs_end = acs[edge:edge + 1, :]
    small = jnp.concatenate([dt, jnp.exp(acs_end - acs), jnp.exp(acs)], axis=0)
    small_hi = small.astype(bf16)
    small_lo = (small - small_hi.astype(f32)).astype(bf16)
    wide = (jnp.dot(small_hi, expand, preferred_element_type=f32)
            + jnp.dot(small_lo, expand, preferred_element_type=f32))
    dtx, dinx, eacsx = wide[:L], wide[L:2 * L], wide[2 * L:]
    xdt = xs * dtx
    xdt_b = xdt.astype(bf16)
    xdec_b = (xdt * dinx).astype(bf16)
    cdec = eacsx[edge:edge + 1, :]
    lane = lax.broadcasted_iota(jnp.int32, (L, LANES), 1)
    ys = []
    for g in range(SSD_GROUPS):
        bg = bm[:, g * SSD_STATE:(g + 1) * SSD_STATE]
        cg = cm[:, g * SSD_STATE:(g + 1) * SSD_STATE]
        gs = slice(g * GROUP_COLS, (g + 1) * GROUP_COLS)
        cb = lax.dot_general(cg, bg, NT, preferred_element_type=f32)
        h_in = state_ref[:, gs]
        y_off = jnp.dot(cg, h_in.astype(bf16), preferred_element_type=f32) * eacsx[:, gs]
        st = lax.dot_general(bg, xdec_b[:, gs], TN, preferred_element_type=f32)
        state_ref[:, gs] = h_in * cdec[:, gs] + st
        tiles = []
        for pair in range(GROUP_COLS // LANES):
            ms = []
            for sub in range(2):
                h = g * (SSD_HEADS // SSD_GROUPS) + pair * 2 + sub
                seg = acs[:, h:h + 1] - acst[h:h + 1, :]
                ms.append((jnp.where(causal, jnp.exp(seg), 0.0) * cb).astype(bf16))
            col = g * GROUP_COLS + pair * LANES
            xp = xdt_b[:, col:col + LANES]
            rhs = jnp.concatenate([jnp.where(lane < SSD_HEAD_DIM, xp, jnp.zeros_like(xp)),
                                   jnp.where(lane >= SSD_HEAD_DIM, xp, jnp.zeros_like(xp))], axis=0)
            tiles.append(jnp.dot(jnp.concatenate(ms, axis=1), rhs, preferred_element_type=f32))
        ys.append(jnp.concatenate(tiles, axis=1) + y_off)
    return jnp.concatenate(ys, axis=1)


def _ssd_kernel(xf_ref, bf_ref, cf_ref, dtf_ref, dttf_ref,
                xb_ref, bb_ref, cb_ref, dtb_ref, dttb_ref,
                bias_ref, biast_ref, alog_ref, alogt_ref, expand_ref,
                yf_ref, yb_ref, sf_ref, sb_ref):
    @pl.when(pl.program_id(0) == 0)
    def _():
        sf_ref[...] = jnp.zeros_like(sf_ref)
        sb_ref[...] = jnp.zeros_like(sb_ref)

    H = SSD_HEADS
    expand = expand_ref[...]
    yf_ref[...] = _ssd_direction(
        xf_ref[...].astype(f32), bf_ref[...], cf_ref[...],
        dtf_ref[:, :H], dttf_ref[:H, :], bias_ref[:, :H], biast_ref[:H, :],
        alog_ref[:, :H], alogt_ref[:H, :], expand, sf_ref, reverse=False)
    yb_ref[...] = _ssd_direction(
        xb_ref[...].astype(f32), bb_ref[...], cb_ref[...],
        dtb_ref[:, H:], dttb_ref[H:, :], bias_ref[:, H:], biast_ref[H:, :],
        alog_ref[:, H:], alogt_ref[H:, :], expand, sb_ref, reverse=True)


def _ssd_scan(xbc, dt, dtt, dt_bias, a_log):
    s = xbc.shape[0]
    nc = s // CHUNK
    gn = SSD_GROUPS * SSD_STATE
    bcol = D_SSD // gn
    fwd = lambda c: c
    bwd = lambda c: nc - 1 - c

    def chunk_specs(sel):
        return [pl.BlockSpec((CHUNK, D_SSD), lambda c: (sel(c), 0)),
                pl.BlockSpec((CHUNK, gn), lambda c: (sel(c), bcol)),
                pl.BlockSpec((CHUNK, gn), lambda c: (sel(c), bcol + 1)),
                pl.BlockSpec((CHUNK, 2 * SSD_HEADS), lambda c: (sel(c), 0)),
                pl.BlockSpec((2 * SSD_HEADS, CHUNK), lambda c: (0, sel(c)))]

    const = lambda shape: pl.BlockSpec(shape, lambda c: (0, 0))
    expand = (jnp.arange(D_SSD)[None, :] // SSD_HEAD_DIM == jnp.arange(SSD_HEADS)[:, None]).astype(bf16)
    return pl.pallas_call(
        _ssd_kernel,
        grid=(nc,),
        in_specs=chunk_specs(fwd) + chunk_specs(bwd) + [
            const((1, 2 * SSD_HEADS)), const((2 * SSD_HEADS, 1)),
            const((1, 2 * SSD_HEADS)), const((2 * SSD_HEADS, 1)),
            const((SSD_HEADS, D_SSD))],
        out_specs=[pl.BlockSpec((CHUNK, D_SSD), lambda c: (fwd(c), 0)),
                   pl.BlockSpec((CHUNK, D_SSD), lambda c: (bwd(c), 0))],
        out_shape=[jax.ShapeDtypeStruct((s, D_SSD), f32)] * 2,
        scratch_shapes=[pltpu.VMEM((SSD_STATE, D_SSD), f32)] * 2,
        compiler_params=_params("arbitrary"),
        name="ssd_scan",
    )(xbc, xbc, xbc, dt, dtt, xbc, xbc, xbc, dt, dtt,
      dt_bias.reshape(1, -1), dt_bias.reshape(-1, 1), a_log.reshape(1, -1), a_log.reshape(-1, 1),
      expand)


def _ssd_post_kernel(yf_ref, yb_ref, xs_ref, z_ref, d_ref, g_ref, o_ref):
    z = z_ref[...].astype(f32)
    y = (yf_ref[...] + yb_ref[...] + d_ref[...] * xs_ref[...].astype(f32)) * (z * _sigmoid(z))
    ms = jnp.mean(y * y, axis=-1, keepdims=True)
    o_ref[...] = (y * lax.rsqrt(ms + EPS) * g_ref[...]).astype(o_ref.dtype)


def _ssd_post(y_f, y_b, xbc, zx, d_skip, g, tb=512):
    s = y_f.shape[0]
    tb = min(tb, s)
    row = pl.BlockSpec((tb, D_SSD), lambda i: (i, 0))
    vec = pl.BlockSpec((1, D_SSD), lambda i: (0, 0))
    return pl.pallas_call(
        _ssd_post_kernel,
        grid=(s // tb,),
        in_specs=[row, row, row, row, vec, vec],
        out_specs=row,
        out_shape=jax.ShapeDtypeStruct((s, D_SSD), bf16),
        compiler_params=_params("parallel"),
        name="ssd_post",
    )(y_f, y_b, xbc, zx, jnp.repeat(d_skip, SSD_HEAD_DIM).reshape(1, D_SSD), g.reshape(1, D_SSD))


def _rope_table_kernel(pos_ref, inv_ref, sign_ref, cos_ref, sin_ref):
    ang = pos_ref[...].astype(f32) * inv_ref[...]
    cos_ref[...] = jnp.cos(ang)
    sin_ref[...] = jnp.sin(ang) * sign_ref[...]


def _rope_tables(positions, tb=512):
    s = positions.shape[0]
    tb = min(tb, s)
    half = ATT_HEAD_DIM // 2
    inv = ROPE_THETA ** (-jnp.arange(0, ATT_HEAD_DIM, 2, dtype=f32) / ATT_HEAD_DIM)
    inv_t = jnp.tile(inv, LANES // half).reshape(1, LANES)
    sign = jnp.tile(jnp.concatenate([-jnp.ones((half,), f32), jnp.ones((half,), f32)]),
                    LANES // ATT_HEAD_DIM).reshape(1, LANES)
    vec = pl.BlockSpec((1, LANES), lambda i: (0, 0))
    row = pl.BlockSpec((tb, LANES), lambda i: (i, 0))
    return pl.pallas_call(
        _rope_table_kernel,
        grid=(s // tb,),
        in_specs=[pl.BlockSpec((tb, 1), lambda i: (i, 0)), vec, vec],
        out_specs=[row, row],
        out_shape=[jax.ShapeDtypeStruct((s, LANES), f32)] * 2,
        compiler_params=_params("parallel"),
        name="rope_tables",
    )(positions.reshape(s, 1), inv_t, sign)


def _qk_kernel(x_ref, g_ref, cos_ref, sin_ref, o_ref):
    which = pl.program_id(1)
    scale = jnp.where(which == 0, ATT_HEAD_DIM ** -0.5 * LOG2E, 1.0).astype(f32)
    g = g_ref[0] * scale
    cos = cos_ref[...]
    sin = sin_ref[...]
    r = lax.broadcasted_iota(jnp.int32, (LANES, LANES), 0) // ATT_HEAD_DIM
    c = lax.broadcasted_iota(jnp.int32, (LANES, LANES), 1) // ATT_HEAD_DIM
    seg = jnp.where(r == c, 1.0, 0.0).astype(bf16)
    lane = lax.broadcasted_iota(jnp.int32, cos.shape, 1)
    first_half = (lane % ATT_HEAD_DIM) < ATT_HEAD_DIM // 2
    for t in range(x_ref.shape[1] // LANES):
        x = x_ref[:, t * LANES:(t + 1) * LANES].astype(f32)
        sq = x * x
        hi = sq.astype(bf16)
        lo = (sq - hi.astype(f32)).astype(bf16)
        ss = (jnp.dot(hi, seg, preferred_element_type=f32)
              + jnp.dot(lo, seg, preferred_element_type=f32))
        xn = x * lax.rsqrt(ss * (1.0 / ATT_HEAD_DIM) + EPS) * g
        swapped = jnp.where(first_half,
                            pltpu.roll(xn, LANES - ATT_HEAD_DIM // 2, axis=1),
                            pltpu.roll(xn, ATT_HEAD_DIM // 2, axis=1))
        o_ref[:, t * LANES:(t + 1) * LANES] = (xn * cos + swapped * sin).astype(o_ref.dtype)


def _qk_prep(qkvg, q_g, k_g, cos_t, sin_t, tb=512):
    s = qkvg.shape[0]
    tb = min(tb, s)
    cols = 2 * ATT_HEADS * ATT_HEAD_DIM
    g2 = jnp.stack([jnp.tile(q_g, 2), jnp.tile(k_g, 2)]).reshape(2, 1, LANES)
    tab = pl.BlockSpec((tb, LANES), lambda i, j: (i, 0))
    return pl.pallas_call(
        _qk_kernel,
        grid=(s // tb, 2),
        in_specs=[pl.BlockSpec((tb, cols), lambda i, j: (i, j)),
                  pl.BlockSpec((1, 1, LANES), lambda i, j: (j, 0, 0)),
                  tab, tab],
        out_specs=pl.BlockSpec((tb, cols), lambda i, j: (i, j)),
        out_shape=jax.ShapeDtypeStruct((s, 2 * cols), bf16),
        compiler_params=_params("parallel", "parallel"),
        name="qk_prep",
    )(qkvg, g2, cos_t, sin_t)


def _attn_kernel(q_ref, k_ref, v_ref, lq1_ref, lk1_ref, lq2_ref, lk2_ref, g_ref, o_ref,
                 qm_ref, s_ref, p_ref, m_ref, l_ref, acc_ref, *, tk, strip, lam_init):
    tq = q_ref.shape[0]
    nk = k_ref.shape[0] // tk
    q = q_ref[...]
    lane = lax.broadcasted_iota(jnp.int32, q.shape, 1)
    zero = jnp.zeros_like(q)
    qm_ref[0] = jnp.where(lane < ATT_HEAD_DIM, q, zero)
    qm_ref[1] = jnp.where(lane >= ATT_HEAD_DIM, q, zero)
    m_ref[...] = jnp.full(m_ref.shape, -1e30, f32)
    l_ref[...] = jnp.zeros_like(l_ref)
    acc_ref[...] = jnp.zeros_like(acc_ref)

    def scores(c, slot):
        kc = k_ref[c * tk:(c + 1) * tk, :]
        for half in range(2):
            s_ref[slot, half] = lax.dot_general(qm_ref[half], kc, NT, preferred_element_type=f32)

    def absorb(c, slot):
        vc = v_ref[c * tk:(c + 1) * tk, :]
        for half in range(2):
            for r0 in range(0, tq, strip):
                rows = slice(r0, r0 + strip)
                s = s_ref[slot, half, rows, :]
                mx = s[:, :LANES]
                for t in range(1, tk // LANES):
                    mx = jnp.maximum(mx, s[:, t * LANES:(t + 1) * LANES])
                m_old = m_ref[half, rows, :]
                m_new = jnp.maximum(m_old, jnp.broadcast_to(jnp.max(mx, axis=-1, keepdims=True),
                                                            m_old.shape))
                alpha = jnp.exp2(m_old - m_new)
                shifted = jnp.concatenate([s[:, t * LANES:(t + 1) * LANES] - m_new
                                           for t in range(tk // LANES)], axis=1)
                p = jnp.exp2(shifted.astype(bf16))
                ps = p[:, :LANES]
                for t in range(1, tk // LANES):
                    ps = ps + p[:, t * LANES:(t + 1) * LANES]
                l_ref[half, rows, :] = alpha * l_ref[half, rows, :] + ps.astype(f32)
                acc_ref[half, rows, :] = alpha * acc_ref[half, rows, :]
                m_ref[half, rows, :] = m_new
                p_ref[half, rows, :] = p
            acc_ref[half] += jnp.dot(p_ref[half], vc, preferred_element_type=f32)

    scores(0, 0)
    for c in range(nk):
        if c + 1 < nk:
            scores(c + 1, (c + 1) % 2)
        absorb(c, c % 2)
    lam = (jnp.exp(jnp.sum(lq1_ref[...] * lk1_ref[...], axis=-1, keepdims=True))
           - jnp.exp(jnp.sum(lq2_ref[...] * lk2_ref[...], axis=-1, keepdims=True)) + lam_init)
    o1 = acc_ref[0] / jnp.sum(l_ref[0], axis=-1, keepdims=True)
    o2 = acc_ref[1] / jnp.sum(l_ref[1], axis=-1, keepdims=True)
    o = o1 - lam * o2
    ms = jnp.mean(o * o, axis=-1, keepdims=True)
    o_ref[...] = (o * lax.rsqrt(ms + EPS) * (g_ref[...] * (1.0 - lam_init))).astype(o_ref.dtype)


def _diff_attention(qk, qkvg, lam_q1, lam_k1, lam_q2, lam_k2, subln_g, lam_init,
                    tq=512, tk=512, strip=64):
    s = qk.shape[0]
    tq = min(tq, s)
    tk = min(tk, s)
    assert s % tk == 0 and tq % strip == 0
    kcol = 2 * ATT_HEADS * ATT_HEAD_DIM // LANES
    vcol = 2 * kcol
    vec = lambda n: pl.BlockSpec((1, n), lambda h, i: (0, 0))
    return pl.pallas_call(
        functools.partial(_attn_kernel, tk=tk, strip=strip, lam_init=lam_init),
        grid=(ATT_HEADS, s // tq),
        in_specs=[pl.BlockSpec((tq, LANES), lambda h, i: (i, h)),
                  pl.BlockSpec((s, LANES), lambda h, i: (0, kcol + h)),
                  pl.BlockSpec((s, LANES), lambda h, i: (0, vcol + h)),
                  vec(ATT_HEAD_DIM), vec(ATT_HEAD_DIM), vec(ATT_HEAD_DIM), vec(ATT_HEAD_DIM),
                  vec(ATT_V_DIM)],
        out_specs=pl.BlockSpec((tq, LANES), lambda h, i: (i, h)),
        out_shape=jax.ShapeDtypeStruct((s, ATT_HEADS * ATT_V_DIM), bf16),
        scratch_shapes=[pltpu.VMEM((2, tq, LANES), bf16),
                        pltpu.VMEM((2, 2, tq, tk), f32),
                        pltpu.VMEM((2, tq, tk), bf16),
                        pltpu.VMEM((2, tq, LANES), f32),
                        pltpu.VMEM((2, tq, LANES), f32),
                        pltpu.VMEM((2, tq, ATT_V_DIM), f32)],
        compiler_params=_params("parallel", "arbitrary"),
        name="diff_attention",
    )(qk, qk, qkvg, lam_q1.reshape(1, -1), lam_k1.reshape(1, -1), lam_q2.reshape(1, -1),
      lam_k2.reshape(1, -1), subln_g.reshape(1, -1))


def _merge_kernel(ys_ref, ya_ref, wa_ref, wb_ref, gs_ref, ga_ref, o_ref):
    a = jnp.dot(ys_ref[...], wa_ref[...].astype(bf16), preferred_element_type=f32)
    b = jnp.dot(ya_ref[...], wb_ref[...].astype(bf16), preferred_element_type=f32)
    mixed = _sigmoid(gs_ref[...].astype(f32)) * a + _sigmoid(ga_ref[...].astype(f32)) * b
    o_ref[...] = mixed.astype(o_ref.dtype)


def _merge(y_ssd, y_att, w_a, w_b, qkvg, tm=2048, tn=256):
    s = y_ssd.shape[0]
    tm = min(tm, s)
    gcol = 3 * 2 * ATT_HEADS * ATT_HEAD_DIM // tn
    nj = D_MODEL // tn
    row = pl.BlockSpec((tm, D_MODEL), lambda i, j: (i, 0))
    wcol = pl.BlockSpec((D_MODEL, tn), lambda i, j: (0, j))
    return pl.pallas_call(
        _merge_kernel,
        grid=(s // tm, nj),
        in_specs=[row, row, wcol, wcol,
                  pl.BlockSpec((tm, tn), lambda i, j: (i, gcol + j)),
                  pl.BlockSpec((tm, tn), lambda i, j: (i, gcol + nj + j))],
        out_specs=pl.BlockSpec((tm, tn), lambda i, j: (i, j)),
        out_shape=jax.ShapeDtypeStruct((s, D_MODEL), bf16),
        compiler_params=_params("parallel", "arbitrary"),
        name="merge",
    )(y_ssd, y_att, w_a, w_b, qkvg, qkvg)


def _take_topk(s, k, rank, payloads):
    big = jnp.int32(2 ** 30)
    vals, ranks, picked = [], [], [[] for _ in payloads]
    for _ in range(k):
        m = jnp.max(s, axis=0, keepdims=True)
        pos = jnp.min(jnp.where(s == m, rank, big), axis=0, keepdims=True)
        hit = rank == pos
        vals.append(m)
        ranks.append(pos)
        for out, pay in zip(picked, payloads):
            out.append(jnp.max(jnp.where(hit, pay, -1.0), axis=0, keepdims=True))
        s = jnp.where(hit, -jnp.inf, s)
    cat = lambda xs: jnp.concatenate(xs, axis=0)
    return cat(vals), cat(ranks), [cat(p) for p in picked]


def _pair_candidates(v1, i1, v2, i2):
    k = PEER_TOPK
    tb = v1.shape[1]
    row8 = lax.broadcasted_iota(jnp.int32, (8, tb), 0)
    neg = jnp.float32(-jnp.inf)
    sums, flat, c1, c2 = [], [], [], []
    for a in range(k // 2):
        nb = k // (a + 1)
        for b0 in range(0, nb, 8):
            piece = v1[a:a + 1] + v2[b0:b0 + 8]
            if nb - b0 < 8:
                piece = jnp.where(row8 < nb - b0, piece, neg)
            sums.append(piece)
            flat.append(row8 + (a * k + b0))
            c1.append(jnp.broadcast_to(i1[a:a + 1], (8, tb)))
            c2.append(i2[b0:b0 + 8])
    sums.append(v1[k // 2:] + v2[0:1])
    flat.append((row8 + k // 2) * k)
    c1.append(i1[k // 2:])
    c2.append(jnp.broadcast_to(i2[0:1], (8, tb)))
    cat = lambda xs: jnp.concatenate(xs, axis=0)
    return cat(sums), cat(flat), cat(c1), cat(c2)


def _topk_kernel(q_ref, keys_ref, i1_ref, i2_ref, gate_ref):
    tb = q_ref.shape[0]
    kk = PEER_TOPK
    key_iota = lax.broadcasted_iota(jnp.int32, (PEER_NKEYS, tb), 0)
    i1s, i2s, gates = [], [], []
    for h in range(PEER_HEADS):
        tops = []
        for half in range(2):
            c0 = (h * 2 + half) * PEER_NKEYS
            qh = q_ref[:, c0:c0 + PEER_NKEYS].astype(bf16)
            keys = keys_ref[h, half].astype(bf16)
            s = lax.dot_general(keys, qh, NT, preferred_element_type=f32)
            vals, idx, _ = _take_topk(s, kk, key_iota, [])
            tops.append((vals, idx.astype(f32)))
        (v1, i1), (v2, i2) = tops
        cand, flat, c1, c2 = _pair_candidates(v1, i1, v2, i2)
        sc, _, (e1, e2) = _take_topk(cand, kk, flat, [c1, c2])
        e = jnp.exp(sc - jnp.max(sc, axis=0, keepdims=True))
        gates.append(e / jnp.sum(e, axis=0, keepdims=True))
        i1s.append(e1)
        i2s.append(e2)
    i1_ref[...] = jnp.concatenate(i1s, axis=0).T
    i2_ref[...] = jnp.concatenate(i2s, axis=0).T
    gate_ref[...] = jnp.concatenate(gates, axis=0).T


def _peer_topk(qp, keys, tb=256):
    s = qp.shape[0]
    tb = min(tb, s)
    nsel = PEER_HEADS * PEER_TOPK
    row = pl.BlockSpec((tb, nsel), lambda i: (i, 0))
    return pl.pallas_call(
        _topk_kernel,
        grid=(s // tb,),
        in_specs=[pl.BlockSpec((tb, qp.shape[1]), lambda i: (i, 0)),
                  pl.BlockSpec(keys.shape, lambda i: (0, 0, 0, 0))],
        out_specs=[row, row, row],
        out_shape=[jax.ShapeDtypeStruct((s, nsel), f32)] * 3,
        compiler_params=_params("parallel"),
        name="peer_topk",
    )(qp, keys)


ROUTE_PITCH = PEER_NKEYS + 8


def _route_kernel(i1_ref, i2_ref, gate_ref, w_ref, tile_ref):
    n = PEER_NKEYS
    tb = i1_ref.shape[0]
    sub = lax.broadcasted_iota(jnp.int32, (n, i1_ref.shape[1]), 0).astype(f32)

    def body(t, carry):
        i1 = i1_ref[pl.ds(t, 1), :]
        i2 = i2_ref[pl.ds(t, 1), :]
        g = gate_ref[pl.ds(t, 1), :]
        a = jnp.where(sub == i1, g, 0.0).astype(bf16)
        b = jnp.where(sub == i2, 1.0, 0.0).astype(bf16)
        tile_ref[pl.ds(pl.multiple_of(t * ROUTE_PITCH, 8), n), :] = lax.dot_general(
            a, b, NT, preferred_element_type=f32)
        return carry

    lax.fori_loop(0, tb, body, 0, unroll=32)
    for a in range(n):
        w_ref[:, a * n:(a + 1) * n] = tile_ref[pl.ds(a, tb, stride=ROUTE_PITCH), :].astype(w_ref.dtype)


def _peer_route(i1, i2, gate, tb=256):
    s, nsel = i1.shape
    tb = min(tb, s)
    row = pl.BlockSpec((tb, nsel), lambda i: (i, 0))
    return pl.pallas_call(
        _route_kernel,
        grid=(s // tb,),
        in_specs=[row, row, row],
        out_specs=pl.BlockSpec((tb, PEER_EXPERTS), lambda i: (i, 0)),
        out_shape=jax.ShapeDtypeStruct((s, PEER_EXPERTS), bf16),
        scratch_shapes=[pltpu.VMEM((tb * ROUTE_PITCH, PEER_NKEYS), f32)],
        compiler_params=_params("parallel"),
        name="peer_route",
    )(i1, i2, gate)


def _peer_dense_kernel(h_ref, u0_ref, ub_ref, un_ref, v_ref, w_ref, o_ref, a_ref, g_ref, *, strip):
    e = pl.program_id(1)
    te = ub_ref.shape[0]
    h = h_ref[...]

    def mix(slot, lo):
        for r0 in range(0, h_ref.shape[0], strip):
            rows = slice(r0, r0 + strip)
            a = a_ref[slot, rows, :]
            act = 0.5 * a * (1.0 + lax.erf(a * (2.0 ** -0.5)))
            g_ref[slot, rows, :] = (act * w_ref[rows, lo:lo + te].astype(f32)).astype(bf16)
        return jnp.dot(g_ref[slot], v_ref[lo:lo + te, :].astype(bf16), preferred_element_type=f32)

    @pl.when(e == 0)
    def _():
        a_ref[0] = lax.dot_general(h, u0_ref[...].astype(bf16), NT, preferred_element_type=f32)
        o_ref[...] = jnp.zeros_like(o_ref)

    a_ref[1] = lax.dot_general(h, ub_ref[...].astype(bf16), NT, preferred_element_type=f32)
    o_ref[...] += mix(0, 0)
    a_ref[0] = lax.dot_general(h, un_ref[...].astype(bf16), NT,
                               preferred_element_type=f32)
    o_ref[...] += mix(1, te)


def _peer_dense(h, u, v, w, tm=1024, te=256):
    s, d = h.shape
    tm = min(tm, s)
    nt = u.shape[0] // te
    utile = lambda f: pl.BlockSpec((te, d), lambda i, e: (f(e), 0))
    return pl.pallas_call(
        functools.partial(_peer_dense_kernel, strip=min(128, tm)),
        grid=(s // tm, nt // 2),
        in_specs=[pl.BlockSpec((tm, d), lambda i, e: (i, 0)),
                  utile(lambda e: 0),
                  utile(lambda e: 2 * e + 1),
                  utile(lambda e: jnp.minimum(2 * e + 2, nt - 1)),
                  pl.BlockSpec((2 * te, d), lambda i, e: (e, 0)),
                  pl.BlockSpec((tm, 2 * te), lambda i, e: (i, e))],
        out_specs=pl.BlockSpec((tm, d), lambda i, e: (i, 0)),
        out_shape=jax.ShapeDtypeStruct((s, d), f32),
        scratch_shapes=[pltpu.VMEM((2, tm, te), f32), pltpu.VMEM((2, tm, te), bf16)],
        compiler_params=_params("parallel", "arbitrary"),
        name="peer_dense",
    )(h, u, u, u, v, w)


def _ple_kernel(h_ref, wg_ref, p_ref, wp_ref, x_ref, o_ref):
    gate = _sigmoid(jnp.dot(h_ref[...], wg_ref[...].astype(bf16), preferred_element_type=f32))
    up = jnp.dot(p_ref[...].astype(bf16), wp_ref[...].astype(bf16), preferred_element_type=f32)
    o_ref[...] = x_ref[...] + gate * up


def _ple(h, w_gate, p, w_up, x, tm=2048, tn=512):
    s = h.shape[0]
    tm = min(tm, s)
    return pl.pallas_call(
        _ple_kernel,
        grid=(s // tm, D_MODEL // tn),
        in_specs=[pl.BlockSpec((tm, D_MODEL), lambda i, j: (i, 0)),
                  pl.BlockSpec((D_MODEL, tn), lambda i, j: (0, j)),
                  pl.BlockSpec((tm, PLE_DIM), lambda i, j: (i, 0)),
                  pl.BlockSpec((PLE_DIM, tn), lambda i, j: (0, j)),
                  pl.BlockSpec((tm, tn), lambda i, j: (i, j))],
        out_specs=pl.BlockSpec((tm, tn), lambda i, j: (i, j)),
        out_shape=jax.ShapeDtypeStruct((s, D_MODEL), f32),
        compiler_params=_params("parallel", "arbitrary"),
        name="ple",
    )(h, w_gate, p, w_up, x)


def _layer(i, x, p, pos, norm1_g, w_in, conv_w, conv_b, a_log_f, a_log_b, dt_bias_f, dt_bias_b,
           d_skip, ssd_norm_g, q_norm_g, k_norm_g, lam_q1, lam_k1, lam_q2, lam_k2, subln_g,
           w_ssd_br, w_att_br, w_out, norm2_g, peer_wq, peer_keys, peer_u, peer_v, norm3_g,
           ple_gate_w, ple_up_w):
    h = _rms_norm(x, norm1_g)
    w_t = w_in.T
    zx = _matmul_nt(h, w_t, 0, ZX_COLS, bf16)
    dt, dtt = _dt_proj(h, w_t, DT_COL0, 2 * SSD_HEADS)
    qkvg = _matmul_nt(h, w_t, REST_COL0, w_t.shape[0] - REST_COL0, bf16)

    xbc = _conv_silu(zx, conv_w, conv_b)
    y_f, y_b = _ssd_scan(xbc, dt, dtt, jnp.concatenate([dt_bias_f, dt_bias_b]),
                         jnp.concatenate([a_log_f, a_log_b]))
    y_ssd = _ssd_post(y_f, y_b, xbc, zx, d_skip, ssd_norm_g)

    cos_t, sin_t = _rope_tables(pos)
    qk = _qk_prep(qkvg, q_norm_g, k_norm_g, cos_t, sin_t)
    lam_init = 0.8 - 0.6 * math.exp(-0.3 * i)
    y_att = _diff_attention(qk, qkvg, lam_q1, lam_k1, lam_q2, lam_k2, subln_g, lam_init)

    mixed = _merge(y_ssd, y_att, w_ssd_br, w_att_br, qkvg)
    x = _matmul(mixed, w_out, 0, D_MODEL, f32, residual=x)

    h2 = _rms_norm(x, norm2_g)
    qp = _matmul(h2, peer_wq, 0, peer_wq.shape[1], f32)
    i1, i2, gate = _peer_topk(qp, peer_keys)
    w = _peer_route(i1, i2, gate)
    mix = _peer_dense(h2, peer_u, peer_v, w)
    x, h3 = _add_rms_norm(x, mix, norm3_g)

    return _ple(h3, ple_gate_w, p, ple_up_w, x)


def kernel(x, p, positions, norm1_g, w_in, conv_w, conv_b, a_log_f, a_log_b, dt_bias_f, dt_bias_b, d_skip, ssd_norm_g, q_norm_g, k_norm_g, lam_q1, lam_k1, lam_q2, lam_k2, subln_g, w_ssd_br, w_att_br, w_out, norm2_g, peer_wq, peer_keys, peer_u, peer_v, norm3_g, ple_gate_w, ple_up_w):
    batch, seq, d = x.shape
    depth = w_in.shape[0]
    outs = []
    for b in range(batch):
        xb = x[b]
        for i in range(depth):
            xb = _layer(i, xb, p[i, b], positions[b], norm1_g[i], w_in[i], conv_w[i], conv_b[i],
                        a_log_f[i], a_log_b[i], dt_bias_f[i], dt_bias_b[i], d_skip[i],
                        ssd_norm_g[i], q_norm_g[i], k_norm_g[i], lam_q1[i], lam_k1[i], lam_q2[i],
                        lam_k2[i], subln_g[i], w_ssd_br[i], w_att_br[i], w_out[i], norm2_g[i],
                        peer_wq[i], peer_keys[i], peer_u[i], peer_v[i], norm3_g[i],
                        ple_gate_w[i], ple_up_w[i])
        outs.append(xb)
    return jnp.stack(outs)
```

```python
import functools
import math

import jax
import jax.numpy as jnp
from jax import lax
from jax.experimental import pallas as pl
from jax.experimental.pallas import tpu as pltpu

f32 = jnp.float32
bf16 = jnp.bfloat16

D_MODEL = 2048
D_SSD = 2048
SSD_HEAD_DIM = 64
SSD_HEADS = 32
SSD_GROUPS = 4
SSD_STATE = 128
GROUP_COLS = D_SSD // SSD_GROUPS
CONV_K = 5
CONV_CH = 3072
CHUNK = 128
ATT_HEADS = 16
ATT_HEAD_DIM = 64
ATT_V_DIM = 128
ROPE_THETA = 10000.0
PEER_HEADS = 8
PEER_NKEYS = 128
PEER_EXPERTS = PEER_NKEYS * PEER_NKEYS
PEER_TOPK = 16
PLE_DIM = 256
EPS = 1e-6
LOG2E = math.log2(math.e)

ZX_COLS = D_SSD + CONV_CH
DT_COL0 = ZX_COLS
REST_COL0 = ZX_COLS + 2 * SSD_HEADS
LANES = 128
BF16_ROWS = 16

VMEM_LIMIT = 56 * 1024 * 1024

NT = (((1,), (1,)), ((), ()))
TN = (((0,), (0,)), ((), ()))


def _params(*sem):
    return pltpu.CompilerParams(dimension_semantics=sem, vmem_limit_bytes=VMEM_LIMIT)


def _sigmoid(x):
    return 1.0 / (1.0 + jnp.exp(-x))


def _softplus(x):
    return jnp.maximum(x, 0.0) + jnp.log1p(jnp.exp(-jnp.abs(x)))


def _norm_kernel(x_ref, g_ref, h_ref):
    x = x_ref[...]
    ms = jnp.mean(x * x, axis=-1, keepdims=True)
    h_ref[...] = (x * lax.rsqrt(ms + EPS) * g_ref[...]).astype(h_ref.dtype)


def _rms_norm(x, g, tb=512):
    s, d = x.shape
    tb = min(tb, s)
    return pl.pallas_call(
        _norm_kernel,
        grid=(s // tb,),
        in_specs=[pl.BlockSpec((tb, d), lambda i: (i, 0)),
                  pl.BlockSpec((1, d), lambda i: (0, 0))],
        out_specs=pl.BlockSpec((tb, d), lambda i: (i, 0)),
        out_shape=jax.ShapeDtypeStruct((s, d), bf16),
        compiler_params=_params("parallel"),
        name="rms_norm",
    )(x, g.reshape(1, d))


def _add_norm_kernel(x_ref, d_ref, g_ref, xo_ref, h_ref):
    x = x_ref[...] + d_ref[...]
    xo_ref[...] = x
    ms = jnp.mean(x * x, axis=-1, keepdims=True)
    h_ref[...] = (x * lax.rsqrt(ms + EPS) * g_ref[...]).astype(h_ref.dtype)


def _add_rms_norm(x, delta, g, tb=512):
    s, d = x.shape
    tb = min(tb, s)
    row = pl.BlockSpec((tb, d), lambda i: (i, 0))
    return pl.pallas_call(
        _add_norm_kernel,
        grid=(s // tb,),
        in_specs=[row, row, pl.BlockSpec((1, d), lambda i: (0, 0))],
        out_specs=[row, row],
        out_shape=[jax.ShapeDtypeStruct((s, d), f32), jax.ShapeDtypeStruct((s, d), bf16)],
        compiler_params=_params("parallel"),
        name="add_rms_norm",
    )(x, delta, g.reshape(1, d))


def _mm_kernel(h_ref, w_ref, o_ref):
    o_ref[...] = jnp.dot(h_ref[...], w_ref[...].astype(bf16),
                         preferred_element_type=f32).astype(o_ref.dtype)


def _mm_res_kernel(h_ref, w_ref, r_ref, o_ref):
    o_ref[...] = r_ref[...] + jnp.dot(h_ref[...], w_ref[...].astype(bf16),
                                      preferred_element_type=f32)


def _matmul(h, w, col0, n, out_dtype, residual=None, tm=2048, tn=512):
    s, k = h.shape
    tm = min(tm, s)
    assert col0 % tn == 0 and n % tn == 0 and s % tm == 0
    cb = col0 // tn
    in_specs = [pl.BlockSpec((tm, k), lambda i, j: (i, 0)),
                pl.BlockSpec((k, tn), lambda i, j: (0, j + cb))]
    args = [h, w]
    kern = _mm_kernel
    if residual is not None:
        in_specs.append(pl.BlockSpec((tm, tn), lambda i, j: (i, j)))
        args.append(residual)
        kern = _mm_res_kernel
    return pl.pallas_call(
        kern,
        grid=(s // tm, n // tn),
        in_specs=in_specs,
        out_specs=pl.BlockSpec((tm, tn), lambda i, j: (i, j)),
        out_shape=jax.ShapeDtypeStruct((s, n), out_dtype),
        compiler_params=_params("parallel", "arbitrary"),
        name="matmul",
    )(*args)


def _mm_nt_kernel(h_ref, w_ref, o_ref):
    o_ref[...] = lax.dot_general(h_ref[...], w_ref[...].astype(bf16), NT,
                                 preferred_element_type=f32).astype(o_ref.dtype)


def _matmul_nt(h, w_t, row0, n, out_dtype, tm=2048, tn=512):
    s, k = h.shape
    tm = min(tm, s)
    assert row0 % BF16_ROWS == 0 and n % tn == 0 and s % tm == 0 and row0 + n <= w_t.shape[0]
    return pl.pallas_call(
        _mm_nt_kernel,
        grid=(s // tm, n // tn),
        in_specs=[pl.BlockSpec((tm, k), lambda i, j: (i, 0)),
                  pl.BlockSpec((pl.Element(tn), pl.Element(k)),
                               lambda i, j: ((row0 // 8 + j * (tn // 8)) * 8, 0))],
        out_specs=pl.BlockSpec((tm, tn), lambda i, j: (i, j)),
        out_shape=jax.ShapeDtypeStruct((s, n), out_dtype),
        compiler_params=_params("parallel", "arbitrary"),
        name="matmul_nt",
    )(h, w_t)


def _dt_kernel(h_ref, w_ref, dt_ref, dtt_ref):
    h = h_ref[...]
    w = w_ref[...].astype(bf16)
    dt_ref[...] = lax.dot_general(h, w, NT, preferred_element_type=f32)
    dtt_ref[...] = lax.dot_general(w, h, NT, preferred_element_type=f32)


def _dt_proj(h, w_t, row0, n, tm=1024):
    s, k = h.shape
    tm = min(tm, s)
    assert row0 % n == 0
    return pl.pallas_call(
        _dt_kernel,
        grid=(s // tm,),
        in_specs=[pl.BlockSpec((tm, k), lambda i: (i, 0)),
                  pl.BlockSpec((n, k), lambda i: (row0 // n, 0))],
        out_specs=[pl.BlockSpec((tm, n), lambda i: (i, 0)),
                   pl.BlockSpec((n, tm), lambda i: (0, i))],
        out_shape=[jax.ShapeDtypeStruct((s, n), f32), jax.ShapeDtypeStruct((n, s), f32)],
        compiler_params=_params("parallel"),
        name="dt_proj",
    )(h, w_t)


def _conv_kernel(prev_ref, main_ref, next_ref, w_ref, b_ref, o_ref):
    i = pl.program_id(0)
    last = pl.num_programs(0) - 1
    tb = main_ref.shape[0]
    halo = prev_ref.shape[0]
    pv = jnp.where(i > 0, prev_ref[...].astype(f32), 0.0)
    nx = jnp.where(i < last, next_ref[...].astype(f32), 0.0)
    ext = jnp.concatenate([pv, main_ref[...].astype(f32), nx], axis=0)
    acc = b_ref[...] + jnp.zeros((tb, main_ref.shape[1]), f32)
    for k in range(CONV_K):
        off = halo + k - CONV_K // 2
        acc = acc + w_ref[k:k + 1, :] * ext[off:off + tb]
    o_ref[...] = (acc * _sigmoid(acc)).astype(o_ref.dtype)


def _conv_silu(zx, conv_w, conv_b, tb=512, tc=1024):
    s = zx.shape[0]
    tb = min(tb, s)
    halo = BF16_ROWS
    rb = tb // halo
    nhalo = s // halo
    cb = D_SSD // tc
    return pl.pallas_call(
        _conv_kernel,
        grid=(s // tb, CONV_CH // tc),
        in_specs=[
            pl.BlockSpec((halo, tc), lambda i, j: (jnp.maximum(i * rb - 1, 0), j + cb)),
            pl.BlockSpec((tb, tc), lambda i, j: (i, j + cb)),
            pl.BlockSpec((halo, tc), lambda i, j: (jnp.minimum((i + 1) * rb, nhalo - 1), j + cb)),
            pl.BlockSpec((CONV_K, tc), lambda i, j: (0, j)),
            pl.BlockSpec((1, tc), lambda i, j: (0, j)),
        ],
        out_specs=pl.BlockSpec((tb, tc), lambda i, j: (i, j)),
        out_shape=jax.ShapeDtypeStruct((s, CONV_CH), bf16),
        compiler_params=_params("parallel", "parallel"),
        name="conv_silu",
    )(zx, zx, zx, conv_w, conv_b.reshape(1, CONV_CH))


def _ssd_direction(xs, bm, cm, dt_raw, dtt_raw, bias, bias_t, a_log, a_log_t, expand,
                   state_ref, reverse):
    L = CHUNK
    dt = _softplus(dt_raw + bias)
    dtt = _softplus(dtt_raw + bias_t)
    a = dt * (-jnp.exp(a_log))
    at = dtt * (-jnp.exp(a_log_t))
    ri = lax.broadcasted_iota(jnp.int32, (L, L), 0)
    ci = lax.broadcasted_iota(jnp.int32, (L, L), 1)
    causal = (ri <= ci) if reverse else (ri >= ci)
    cum_l = jnp.where(causal, 1.0, 0.0).astype(f32)
    cum_r = jnp.where((ri >= ci) if reverse else (ri <= ci), 1.0, 0.0).astype(f32)
    acs = jnp.dot(cum_l, a, preferred_element_type=f32, precision=lax.Precision.HIGHEST)
    acst = jnp.dot(at, cum_r, preferred_element_type=f32, precision=lax.Precision.HIGHEST)
    edge = 0 if reverse else L - 1
    acs_end = acs[edge:edge + 1, :]
    end_decay = jnp.exp(acs_end)
    end_hi = end_decay.astype(bf16)
    end_lo = (end_decay - end_hi.astype(f32)).astype(bf16)
    small = jnp.concatenate([(dt * jnp.exp(acs_end - acs)).astype(bf16), jnp.exp(acs).astype(bf16),
                             jnp.broadcast_to(end_hi, (8, SSD_HEADS)),
                             jnp.broadcast_to(end_lo, (8, SSD_HEADS))], axis=0)
    wide = jnp.dot(small, expand, preferred_element_type=f32)
    dinx, eacsx = wide[:L], wide[L:2 * L]
    cdec = wide[2 * L:2 * L + 1] + wide[2 * L + 8:2 * L + 9]
    xs_b = xs.astype(bf16)
    xdec_b = (xs * dinx).astype(bf16)
    lane = lax.broadcasted_iota(jnp.int32, (L, LANES), 1)
    ys = []
    for g in range(SSD_GROUPS):
        bg = bm[:, g * SSD_STATE:(g + 1) * SSD_STATE]
        cg = cm[:, g * SSD_STATE:(g + 1) * SSD_STATE]
        gs = slice(g * GROUP_COLS, (g + 1) * GROUP_COLS)
        cb = lax.dot_general(cg, bg, NT, preferred_element_type=f32)
        h_in = state_ref[:, gs]
        y_off = jnp.dot(cg, h_in.astype(bf16), preferred_element_type=f32) * eacsx[:, gs]
        st = lax.dot_general(bg, xdec_b[:, gs], TN, preferred_element_type=f32)
        state_ref[:, gs] = h_in * cdec[:, gs] + st
        tiles = []
        for pair in range(GROUP_COLS // LANES):
            ms = []
            for sub in range(2):
                h = g * (SSD_HEADS // SSD_GROUPS) + pair * 2 + sub
                seg = acs[:, h:h + 1] - acst[h:h + 1, :]
                ms.append((jnp.where(causal, jnp.exp(seg), 0.0) * (cb * dtt[h:h + 1, :])).astype(bf16))
            col = g * GROUP_COLS + pair * LANES
            xp = xs_b[:, col:col + LANES]
            rhs = jnp.concatenate([jnp.where(lane < SSD_HEAD_DIM, xp, jnp.zeros_like(xp)),
                                   jnp.where(lane >= SSD_HEAD_DIM, xp, jnp.zeros_like(xp))], axis=0)
            tiles.append(jnp.dot(jnp.concatenate(ms, axis=1), rhs, preferred_element_type=f32))
        ys.append(jnp.concatenate(tiles, axis=1) + y_off)
    return jnp.concatenate(ys, axis=1)


def _ssd_kernel(xf_ref, bf_ref, cf_ref, dtf_ref, dttf_ref,
                xb_ref, bb_ref, cb_ref, dtb_ref, dttb_ref,
                bias_ref, biast_ref, alog_ref, alogt_ref, expand_ref,
                yf_ref, yb_ref, sf_ref, sb_ref):
    @pl.when(pl.program_id(0) == 0)
    def _():
        sf_ref[...] = jnp.zeros_like(sf_ref)
        sb_ref[...] = jnp.zeros_like(sb_ref)

    H = SSD_HEADS
    expand = expand_ref[...]
    yf_ref[...] = _ssd_direction(
        xf_ref[...].astype(f32), bf_ref[...], cf_ref[...],
        dtf_ref[:, :H], dttf_ref[:H, :], bias_ref[:, :H], biast_ref[:H, :],
        alog_ref[:, :H], alogt_ref[:H, :], expand, sf_ref, reverse=False).astype(yf_ref.dtype)
    yb_ref[...] = _ssd_direction(
        xb_ref[...].astype(f32), bb_ref[...], cb_ref[...],
        dtb_ref[:, H:], dttb_ref[H:, :], bias_ref[:, H:], biast_ref[H:, :],
        alog_ref[:, H:], alogt_ref[H:, :], expand, sb_ref, reverse=True).astype(yb_ref.dtype)


def _ssd_scan(xbc, dt, dtt, dt_bias, a_log):
    s = xbc.shape[0]
    nc = s // CHUNK
    gn = SSD_GROUPS * SSD_STATE
    bcol = D_SSD // gn
    fwd = lambda c: c
    bwd = lambda c: nc - 1 - c

    def chunk_specs(sel):
        return [pl.BlockSpec((CHUNK, D_SSD), lambda c: (sel(c), 0)),
                pl.BlockSpec((CHUNK, gn), lambda c: (sel(c), bcol)),
                pl.BlockSpec((CHUNK, gn), lambda c: (sel(c), bcol + 1)),
                pl.BlockSpec((CHUNK, 2 * SSD_HEADS), lambda c: (sel(c), 0)),
                pl.BlockSpec((2 * SSD_HEADS, CHUNK), lambda c: (0, sel(c)))]

    const = lambda shape: pl.BlockSpec(shape, lambda c: (0, 0))
    expand = (jnp.arange(D_SSD)[None, :] // SSD_HEAD_DIM == jnp.arange(SSD_HEADS)[:, None]).astype(bf16)
    return pl.pallas_call(
        _ssd_kernel,
        grid=(nc,),
        in_specs=chunk_specs(fwd) + chunk_specs(bwd) + [
            const((1, 2 * SSD_HEADS)), const((2 * SSD_HEADS, 1)),
            const((1, 2 * SSD_HEADS)), const((2 * SSD_HEADS, 1)),
            const((SSD_HEADS, D_SSD))],
        out_specs=[pl.BlockSpec((CHUNK, D_SSD), lambda c: (fwd(c), 0)),
                   pl.BlockSpec((CHUNK, D_SSD), lambda c: (bwd(c), 0))],
        out_shape=[jax.ShapeDtypeStruct((s, D_SSD), bf16)] * 2,
        scratch_shapes=[pltpu.VMEM((SSD_STATE, D_SSD), f32)] * 2,
        compiler_params=_params("arbitrary"),
        name="ssd_scan",
    )(xbc, xbc, xbc, dt, dtt, xbc, xbc, xbc, dt, dtt,
      dt_bias.reshape(1, -1), dt_bias.reshape(-1, 1), a_log.reshape(1, -1), a_log.reshape(-1, 1),
      expand)


def _ssd_post_kernel(yf_ref, yb_ref, xs_ref, z_ref, d_ref, g_ref, o_ref):
    z = z_ref[...].astype(f32)
    y = (yf_ref[...].astype(f32) + yb_ref[...].astype(f32)
         + d_ref[...] * xs_ref[...].astype(f32)) * (z * _sigmoid(z))
    ms = jnp.mean(y * y, axis=-1, keepdims=True)
    o_ref[...] = (y * lax.rsqrt(ms + EPS) * g_ref[...]).astype(o_ref.dtype)


def _ssd_post(y_f, y_b, xbc, zx, d_skip, g, tb=512):
    s = y_f.shape[0]
    tb = min(tb, s)
    row = pl.BlockSpec((tb, D_SSD), lambda i: (i, 0))
    vec = pl.BlockSpec((1, D_SSD), lambda i: (0, 0))
    return pl.pallas_call(
        _ssd_post_kernel,
        grid=(s // tb,),
        in_specs=[row, row, row, row, vec, vec],
        out_specs=row,
        out_shape=jax.ShapeDtypeStruct((s, D_SSD), bf16),
        compiler_params=_params("parallel"),
        name="ssd_post",
    )(y_f, y_b, xbc, zx, jnp.repeat(d_skip, SSD_HEAD_DIM).reshape(1, D_SSD), g.reshape(1, D_SSD))


def _rope_table_kernel(pos_ref, inv_ref, sign_ref, cos_ref, sin_ref):
    ang = pos_ref[...].astype(f32) * inv_ref[...]
    cos_ref[...] = jnp.cos(ang)
    sin_ref[...] = jnp.sin(ang) * sign_ref[...]


def _rope_tables(positions, tb=512):
    s = positions.shape[0]
    tb = min(tb, s)
    half = ATT_HEAD_DIM // 2
    inv = ROPE_THETA ** (-jnp.arange(0, ATT_HEAD_DIM, 2, dtype=f32) / ATT_HEAD_DIM)
    inv_t = jnp.tile(inv, LANES // half).reshape(1, LANES)
    sign = jnp.tile(jnp.concatenate([-jnp.ones((half,), f32), jnp.ones((half,), f32)]),
                    LANES // ATT_HEAD_DIM).reshape(1, LANES)
    vec = pl.BlockSpec((1, LANES), lambda i: (0, 0))
    row = pl.BlockSpec((tb, LANES), lambda i: (i, 0))
    return pl.pallas_call(
        _rope_table_kernel,
        grid=(s // tb,),
        in_specs=[pl.BlockSpec((tb, 1), lambda i: (i, 0)), vec, vec],
        out_specs=[row, row],
        out_shape=[jax.ShapeDtypeStruct((s, LANES), f32)] * 2,
        compiler_params=_params("parallel"),
        name="rope_tables",
    )(positions.reshape(s, 1), inv_t, sign)


def _qk_kernel(x_ref, g_ref, cos_ref, sin_ref, o_ref):
    which = pl.program_id(1)
    scale = jnp.where(which == 0, ATT_HEAD_DIM ** -0.5 * LOG2E, 1.0).astype(f32)
    g = g_ref[0] * scale
    cos = cos_ref[...]
    sin = sin_ref[...]
    r = lax.broadcasted_iota(jnp.int32, (LANES, LANES), 0) // ATT_HEAD_DIM
    c = lax.broadcasted_iota(jnp.int32, (LANES, LANES), 1) // ATT_HEAD_DIM
    seg = jnp.where(r == c, 1.0, 0.0).astype(bf16)
    lane = lax.broadcasted_iota(jnp.int32, cos.shape, 1)
    first_half = (lane % ATT_HEAD_DIM) < ATT_HEAD_DIM // 2
    for t in range(x_ref.shape[1] // LANES):
        x = x_ref[:, t * LANES:(t + 1) * LANES].astype(f32)
        sq = x * x
        hi = sq.astype(bf16)
        lo = (sq - hi.astype(f32)).astype(bf16)
        ss = (jnp.dot(hi, seg, preferred_element_type=f32)
              + jnp.dot(lo, seg, preferred_element_type=f32))
        xn = x * lax.rsqrt(ss * (1.0 / ATT_HEAD_DIM) + EPS) * g
        swapped = jnp.where(first_half,
                            pltpu.roll(xn, LANES - ATT_HEAD_DIM // 2, axis=1),
                            pltpu.roll(xn, ATT_HEAD_DIM // 2, axis=1))
        o_ref[:, t * LANES:(t + 1) * LANES] = (xn * cos + swapped * sin).astype(o_ref.dtype)


def _qk_prep(qkvg, q_g, k_g, cos_t, sin_t, tb=512):
    s = qkvg.shape[0]
    tb = min(tb, s)
    cols = 2 * ATT_HEADS * ATT_HEAD_DIM
    g2 = jnp.stack([jnp.tile(q_g, 2), jnp.tile(k_g, 2)]).reshape(2, 1, LANES)
    tab = pl.BlockSpec((tb, LANES), lambda i, j: (i, 0))
    return pl.pallas_call(
        _qk_kernel,
        grid=(s // tb, 2),
        in_specs=[pl.BlockSpec((tb, cols), lambda i, j: (i, j)),
                  pl.BlockSpec((1, 1, LANES), lambda i, j: (j, 0, 0)),
                  tab, tab],
        out_specs=pl.BlockSpec((tb, cols), lambda i, j: (i, j)),
        out_shape=jax.ShapeDtypeStruct((s, 2 * cols), bf16),
        compiler_params=_params("parallel", "parallel"),
        name="qk_prep",
    )(qkvg, g2, cos_t, sin_t)


def _attn_kernel(q_ref, k_ref, v_ref, lq1_ref, lk1_ref, lq2_ref, lk2_ref, g_ref, o_ref,
                 qm_ref, s_ref, p_ref, m_ref, l_ref, acc_ref, *, tk, strip, lam_init):
    tq = q_ref.shape[0]
    nk = k_ref.shape[0] // tk
    q = q_ref[...]
    lane = lax.broadcasted_iota(jnp.int32, q.shape, 1)
    zero = jnp.zeros_like(q)
    qm_ref[0] = jnp.where(lane < ATT_HEAD_DIM, q, zero)
    qm_ref[1] = jnp.where(lane >= ATT_HEAD_DIM, q, zero)
    m_ref[...] = jnp.full(m_ref.shape, -1e30, f32)
    l_ref[...] = jnp.zeros_like(l_ref)
    acc_ref[...] = jnp.zeros_like(acc_ref)

    def scores(c, slot):
        kc = k_ref[c * tk:(c + 1) * tk, :]
        for half in range(2):
            s_ref[slot, half] = lax.dot_general(qm_ref[half], kc, NT, preferred_element_type=f32)

    def absorb(c, slot):
        vc = v_ref[c * tk:(c + 1) * tk, :]
        for half in range(2):
            for r0 in range(0, tq, strip):
                rows = slice(r0, r0 + strip)
                s = s_ref[slot, half, rows, :]
                mx = s[:, :LANES]
                for t in range(1, tk // LANES):
                    mx = jnp.maximum(mx, s[:, t * LANES:(t + 1) * LANES])
                m_old = m_ref[half, rows, :]
                m_new = jnp.maximum(m_old, jnp.broadcast_to(jnp.max(mx, axis=-1, keepdims=True),
                                                            m_old.shape))
                alpha = jnp.exp2(m_old - m_new)
                shifted = jnp.concatenate([s[:, t * LANES:(t + 1) * LANES] - m_new
                                           for t in range(tk // LANES)], axis=1)
                p = jnp.exp2(shifted.astype(bf16))
                ps = p[:, :LANES]
                for t in range(1, tk // LANES):
                    ps = ps + p[:, t * LANES:(t + 1) * LANES]
                l_ref[half, rows, :] = alpha * l_ref[half, rows, :] + ps.astype(f32)
                acc_ref[half, rows, :] = alpha * acc_ref[half, rows, :]
                m_ref[half, rows, :] = m_new
                p_ref[half, rows, :] = p
            acc_ref[half] += jnp.dot(p_ref[half], vc, preferred_element_type=f32)

    scores(0, 0)
    for c in range(nk):
        if c + 1 < nk:
            scores(c + 1, (c + 1) % 2)
        absorb(c, c % 2)
    lam = (jnp.exp(jnp.sum(lq1_ref[...] * lk1_ref[...], axis=-1, keepdims=True))
           - jnp.exp(jnp.sum(lq2_ref[...] * lk2_ref[...], axis=-1, keepdims=True)) + lam_init)
    o1 = acc_ref[0] / jnp.sum(l_ref[0], axis=-1, keepdims=True)
    o2 = acc_ref[1] / jnp.sum(l_ref[1], axis=-1, keepdims=True)
    o = o1 - lam * o2
    ms = jnp.mean(o * o, axis=-1, keepdims=True)
    o_ref[...] = (o * lax.rsqrt(ms + EPS) * (g_ref[...] * (1.0 - lam_init))).astype(o_ref.dtype)


def _diff_attention(qk, qkvg, lam_q1, lam_k1, lam_q2, lam_k2, subln_g, lam_init,
                    tq=512, tk=512, strip=64):
    s = qk.shape[0]
    tq = min(tq, s)
    tk = min(tk, s)
    assert s % tk == 0 and tq % strip == 0
    kcol = 2 * ATT_HEADS * ATT_HEAD_DIM // LANES
    vcol = 2 * kcol
    vec = lambda n: pl.BlockSpec((1, n), lambda h, i: (0, 0))
    return pl.pallas_call(
        functools.partial(_attn_kernel, tk=tk, strip=strip, lam_init=lam_init),
        grid=(ATT_HEADS, s // tq),
        in_specs=[pl.BlockSpec((tq, LANES), lambda h, i: (i, h)),
                  pl.BlockSpec((s, LANES), lambda h, i: (0, kcol + h)),
                  pl.BlockSpec((s, LANES), lambda h, i: (0, vcol + h)),
                  vec(ATT_HEAD_DIM), vec(ATT_HEAD_DIM), vec(ATT_HEAD_DIM), vec(ATT_HEAD_DIM),
                  vec(ATT_V_DIM)],
        out_specs=pl.BlockSpec((tq, LANES), lambda h, i: (i, h)),
        out_shape=jax.ShapeDtypeStruct((s, ATT_HEADS * ATT_V_DIM), bf16),
        scratch_shapes=[pltpu.VMEM((2, tq, LANES), bf16),
                        pltpu.VMEM((2, 2, tq, tk), f32),
                        pltpu.VMEM((2, tq, tk), bf16),
                        pltpu.VMEM((2, tq, LANES), f32),
                        pltpu.VMEM((2, tq, LANES), f32),
                        pltpu.VMEM((2, tq, ATT_V_DIM), f32)],
        compiler_params=_params("parallel", "arbitrary"),
        name="diff_attention",
    )(qk, qk, qkvg, lam_q1.reshape(1, -1), lam_k1.reshape(1, -1), lam_q2.reshape(1, -1),
      lam_k2.reshape(1, -1), subln_g.reshape(1, -1))


def _merge_kernel(ys_ref, ya_ref, wa_ref, wb_ref, gs_ref, ga_ref, o_ref):
    a = jnp.dot(ys_ref[...], wa_ref[...].astype(bf16), preferred_element_type=f32)
    b = jnp.dot(ya_ref[...], wb_ref[...].astype(bf16), preferred_element_type=f32)
    mixed = _sigmoid(gs_ref[...].astype(f32)) * a + _sigmoid(ga_ref[...].astype(f32)) * b
    o_ref[...] = mixed.astype(o_ref.dtype)


def _merge(y_ssd, y_att, w_a, w_b, qkvg, tm=2048, tn=256):
    s = y_ssd.shape[0]
    tm = min(tm, s)
    gcol = 3 * 2 * ATT_HEADS * ATT_HEAD_DIM // tn
    nj = D_MODEL // tn
    row = pl.BlockSpec((tm, D_MODEL), lambda i, j: (i, 0))
    wcol = pl.BlockSpec((D_MODEL, tn), lambda i, j: (0, j))
    return pl.pallas_call(
        _merge_kernel,
        grid=(s // tm, nj),
        in_specs=[row, row, wcol, wcol,
                  pl.BlockSpec((tm, tn), lambda i, j: (i, gcol + j)),
                  pl.BlockSpec((tm, tn), lambda i, j: (i, gcol + nj + j))],
        out_specs=pl.BlockSpec((tm, tn), lambda i, j: (i, j)),
        out_shape=jax.ShapeDtypeStruct((s, D_MODEL), bf16),
        compiler_params=_params("parallel", "arbitrary"),
        name="merge",
    )(y_ssd, y_att, w_a, w_b, qkvg, qkvg)


def _take_topk(s, k, rank, payloads):
    big = jnp.int32(2 ** 30)
    vals, ranks, picked = [], [], [[] for _ in payloads]
    for _ in range(k):
        m = jnp.max(s, axis=0, keepdims=True)
        pos = jnp.min(jnp.where(s == m, rank, big), axis=0, keepdims=True)
        hit = rank == pos
        vals.append(m)
        ranks.append(pos)
        for out, pay in zip(picked, payloads):
            out.append(jnp.max(jnp.where(hit, pay, -1.0), axis=0, keepdims=True))
        s = jnp.where(hit, -jnp.inf, s)
    cat = lambda xs: jnp.concatenate(xs, axis=0)
    return cat(vals), cat(ranks), [cat(p) for p in picked]


def _pair_candidates(v1, i1, v2, i2):
    k = PEER_TOPK
    tb = v1.shape[1]
    row8 = lax.broadcasted_iota(jnp.int32, (8, tb), 0)
    neg = jnp.float32(-jnp.inf)
    sums, flat, c1, c2 = [], [], [], []
    for a in range(k // 2):
        nb = k // (a + 1)
        for b0 in range(0, nb, 8):
            piece = v1[a:a + 1] + v2[b0:b0 + 8]
            if nb - b0 < 8:
                piece = jnp.where(row8 < nb - b0, piece, neg)
            sums.append(piece)
            flat.append(row8 + (a * k + b0))
            c1.append(jnp.broadcast_to(i1[a:a + 1], (8, tb)))
            c2.append(i2[b0:b0 + 8])
    sums.append(v1[k // 2:] + v2[0:1])
    flat.append((row8 + k // 2) * k)
    c1.append(i1[k // 2:])
    c2.append(jnp.broadcast_to(i2[0:1], (8, tb)))
    cat = lambda xs: jnp.concatenate(xs, axis=0)
    return cat(sums), cat(flat), cat(c1), cat(c2)


def _topk_kernel(q_ref, keys_ref, i1_ref, i2_ref, gate_ref):
    tb = q_ref.shape[0]
    kk = PEER_TOPK
    key_iota = lax.broadcasted_iota(jnp.int32, (PEER_NKEYS, tb), 0)
    i1s, i2s, gates = [], [], []
    for h in range(PEER_HEADS):
        tops = []
        for half in range(2):
            c0 = (h * 2 + half) * PEER_NKEYS
            qh = q_ref[:, c0:c0 + PEER_NKEYS].astype(bf16)
            keys = keys_ref[h, half].astype(bf16)
            s = lax.dot_general(keys, qh, NT, preferred_element_type=f32)
            vals, idx, _ = _take_topk(s, kk, key_iota, [])
            tops.append((vals, idx.astype(f32)))
        (v1, i1), (v2, i2) = tops
        cand, flat, c1, c2 = _pair_candidates(v1, i1, v2, i2)
        sc, _, (e1, e2) = _take_topk(cand, kk, flat, [c1, c2])
        e = jnp.exp(sc - jnp.max(sc, axis=0, keepdims=True))
        gates.append(e / jnp.sum(e, axis=0, keepdims=True))
        i1s.append(e1)
        i2s.append(e2)
    i1_ref[...] = jnp.concatenate(i1s, axis=0).T
    i2_ref[...] = jnp.concatenate(i2s, axis=0).T
    gate_ref[...] = jnp.concatenate(gates, axis=0).T


def _peer_topk(qp, keys, tb=256):
    s = qp.shape[0]
    tb = min(tb, s)
    nsel = PEER_HEADS * PEER_TOPK
    row = pl.BlockSpec((tb, nsel), lambda i: (i, 0))
    return pl.pallas_call(
        _topk_kernel,
        grid=(s // tb,),
        in_specs=[pl.BlockSpec((tb, qp.shape[1]), lambda i: (i, 0)),
                  pl.BlockSpec(keys.shape, lambda i: (0, 0, 0, 0))],
        out_specs=[row, row, row],
        out_shape=[jax.ShapeDtypeStruct((s, nsel), f32)] * 3,
        compiler_params=_params("parallel"),
        name="peer_topk",
    )(qp, keys)


ROUTE_PITCH = PEER_NKEYS + 8


def _route_kernel(i1_ref, i2_ref, gate_ref, w_ref, tile_ref):
    n = PEER_NKEYS
    tb = i1_ref.shape[0]
    sub = lax.broadcasted_iota(jnp.int32, (n, i1_ref.shape[1]), 0).astype(f32)

    def body(t, carry):
        i1 = i1_ref[pl.ds(t, 1), :]
        i2 = i2_ref[pl.ds(t, 1), :]
        g = gate_ref[pl.ds(t, 1), :]
        a = jnp.where(sub == i1, g, 0.0).astype(bf16)
        b = jnp.where(sub == i2, 1.0, 0.0).astype(bf16)
        tile_ref[pl.ds(pl.multiple_of(t * ROUTE_PITCH, 8), n), :] = lax.dot_general(
            a, b, NT, preferred_element_type=f32)
        return carry

    lax.fori_loop(0, tb, body, 0, unroll=32)
    for a in range(n):
        w_ref[:, a * n:(a + 1) * n] = tile_ref[pl.ds(a, tb, stride=ROUTE_PITCH), :].astype(w_ref.dtype)


def _peer_route(i1, i2, gate, tb=256):
    s, nsel = i1.shape
    tb = min(tb, s)
    row = pl.BlockSpec((tb, nsel), lambda i: (i, 0))
    return pl.pallas_call(
        _route_kernel,
        grid=(s // tb,),
        in_specs=[row, row, row],
        out_specs=pl.BlockSpec((tb, PEER_EXPERTS), lambda i: (i, 0)),
        out_shape=jax.ShapeDtypeStruct((s, PEER_EXPERTS), bf16),
        scratch_shapes=[pltpu.VMEM((tb * ROUTE_PITCH, PEER_NKEYS), f32)],
        compiler_params=_params("parallel"),
        name="peer_route",
    )(i1, i2, gate)


def _peer_dense_kernel(h_ref, u0_ref, ub_ref, un_ref, v_ref, w_ref, o_ref, a_ref, g_ref, *, strip):
    e = pl.program_id(1)
    te = ub_ref.shape[0]
    h = h_ref[...]

    def mix(slot, lo):
        for r0 in range(0, h_ref.shape[0], strip):
            rows = slice(r0, r0 + strip)
            a = a_ref[slot, rows, :]
            act = 0.5 * a * (1.0 + lax.erf(a * (2.0 ** -0.5)))
            g_ref[slot, rows, :] = (act * w_ref[rows, lo:lo + te].astype(f32)).astype(bf16)
        return jnp.dot(g_ref[slot], v_ref[lo:lo + te, :].astype(bf16), preferred_element_type=f32)

    @pl.when(e == 0)
    def _():
        a_ref[0] = lax.dot_general(h, u0_ref[...].astype(bf16), NT, preferred_element_type=f32)
        o_ref[...] = jnp.zeros_like(o_ref)

    a_ref[1] = lax.dot_general(h, ub_ref[...].astype(bf16), NT, preferred_element_type=f32)
    o_ref[...] += mix(0, 0)
    a_ref[0] = lax.dot_general(h, un_ref[...].astype(bf16), NT,
                               preferred_element_type=f32)
    o_ref[...] += mix(1, te)


def _peer_dense(h, u, v, w, tm=1024, te=256):
    s, d = h.shape
    tm = min(tm, s)
    nt = u.shape[0] // te
    utile = lambda f: pl.BlockSpec((te, d), lambda i, e: (f(e), 0))
    return pl.pallas_call(
        functools.partial(_peer_dense_kernel, strip=min(128, tm)),
        grid=(s // tm, nt // 2),
        in_specs=[pl.BlockSpec((tm, d), lambda i, e: (i, 0)),
                  utile(lambda e: 0),
                  utile(lambda e: 2 * e + 1),
                  utile(lambda e: jnp.minimum(2 * e + 2, nt - 1)),
                  pl.BlockSpec((2 * te, d), lambda i, e: (e, 0)),
                  pl.BlockSpec((tm, 2 * te), lambda i, e: (i, e))],
        out_specs=pl.BlockSpec((tm, d), lambda i, e: (i, 0)),
        out_shape=jax.ShapeDtypeStruct((s, d), f32),
        scratch_shapes=[pltpu.VMEM((2, tm, te), f32), pltpu.VMEM((2, tm, te), bf16)],
        compiler_params=_params("parallel", "arbitrary"),
        name="peer_dense",
    )(h, u, u, u, v, w)


def _ple_kernel(h_ref, wg_ref, p_ref, wp_ref, x_ref, o_ref):
    gate = _sigmoid(jnp.dot(h_ref[...], wg_ref[...].astype(bf16), preferred_element_type=f32))
    up = jnp.dot(p_ref[...].astype(bf16), wp_ref[...].astype(bf16), preferred_element_type=f32)
    o_ref[...] = x_ref[...] + gate * up


def _ple(h, w_gate, p, w_up, x, tm=2048, tn=512):
    s = h.shape[0]
    tm = min(tm, s)
    return pl.pallas_call(
        _ple_kernel,
        grid=(s // tm, D_MODEL // tn),
        in_specs=[pl.BlockSpec((tm, D_MODEL), lambda i, j: (i, 0)),
                  pl.BlockSpec((D_MODEL, tn), lambda i, j: (0, j)),
                  pl.BlockSpec((tm, PLE_DIM), lambda i, j: (i, 0)),
                  pl.BlockSpec((PLE_DIM, tn), lambda i, j: (0, j)),
                  pl.BlockSpec((tm, tn), lambda i, j: (i, j))],
        out_specs=pl.BlockSpec((tm, tn), lambda i, j: (i, j)),
        out_shape=jax.ShapeDtypeStruct((s, D_MODEL), f32),
        compiler_params=_params("parallel", "arbitrary"),
        name="ple",
    )(h, w_gate, p, w_up, x)


def _layer(i, x, p, pos, norm1_g, w_in, conv_w, conv_b, a_log_f, a_log_b, dt_bias_f, dt_bias_b,
           d_skip, ssd_norm_g, q_norm_g, k_norm_g, lam_q1, lam_k1, lam_q2, lam_k2, subln_g,
           w_ssd_br, w_att_br, w_out, norm2_g, peer_wq, peer_keys, peer_u, peer_v, norm3_g,
           ple_gate_w, ple_up_w):
    h = _rms_norm(x, norm1_g)
    w_t = w_in.T
    zx = _matmul_nt(h, w_t, 0, ZX_COLS, bf16)
    dt, dtt = _dt_proj(h, w_t, DT_COL0, 2 * SSD_HEADS)
    qkvg = _matmul_nt(h, w_t, REST_COL0, w_t.shape[0] - REST_COL0, bf16)

    xbc = _conv_silu(zx, conv_w, conv_b)
    y_f, y_b = _ssd_scan(xbc, dt, dtt, jnp.concatenate([dt_bias_f, dt_bias_b]),
                         jnp.concatenate([a_log_f, a_log_b]))
    y_ssd = _ssd_post(y_f, y_b, xbc, zx, d_skip, ssd_norm_g)

    cos_t, sin_t = _rope_tables(pos)
    qk = _qk_prep(qkvg, q_norm_g, k_norm_g, cos_t, sin_t)
    lam_init = 0.8 - 0.6 * math.exp(-0.3 * i)
    y_att = _diff_attention(qk, qkvg, lam_q1, lam_k1, lam_q2, lam_k2, subln_g, lam_init)

    mixed = _merge(y_ssd, y_att, w_ssd_br, w_att_br, qkvg)
    x = _matmul(mixed, w_out, 0, D_MODEL, f32, residual=x)

    h2 = _rms_norm(x, norm2_g)
    qp = _matmul(h2, peer_wq, 0, peer_wq.shape[1], f32)
    i1, i2, gate = _peer_topk(qp, peer_keys)
    w = _peer_route(i1, i2, gate)
    mix = _peer_dense(h2, peer_u, peer_v, w)
    x, h3 = _add_rms_norm(x, mix, norm3_g)

    return _ple(h3, ple_gate_w, p, ple_up_w, x)


def kernel(x, p, positions, norm1_g, w_in, conv_w, conv_b, a_log_f, a_log_b, dt_bias_f, dt_bias_b, d_skip, ssd_norm_g, q_norm_g, k_norm_g, lam_q1, lam_k1, lam_q2, lam_k2, subln_g, w_ssd_br, w_att_br, w_out, norm2_g, peer_wq, peer_keys, peer_u, peer_v, norm3_g, ple_gate_w, ple_up_w):
    batch, seq, d = x.shape
    depth = w_in.shape[0]
    outs = []
    for b in range(batch):
        xb = x[b]
        for i in range(depth):
            xb = _layer(i, xb, p[i, b], positions[b], norm1_g[i], w_in[i], conv_w[i], conv_b[i],
                        a_log_f[i], a_log_b[i], dt_bias_f[i], dt_bias_b[i], d_skip[i],
                        ssd_norm_g[i], q_norm_g[i], k_norm_g[i], lam_q1[i], lam_k1[i], lam_q2[i],
                        lam_k2[i], subln_g[i], w_ssd_br[i], w_att_br[i], w_out[i], norm2_g[i],
                        peer_wq[i], peer_keys[i], peer_u[i], peer_v[i], norm3_g[i],
                        ple_gate_w[i], ple_up_w[i])
        outs.append(xb)
    return jnp.stack(outs)
```

```python
import functools
import math

import jax
import jax.numpy as jnp
from jax import lax
from jax.experimental import pallas as pl
from jax.experimental.pallas import tpu as pltpu

f32 = jnp.float32
bf16 = jnp.bfloat16

D_MODEL = 2048
D_SSD = 2048
SSD_HEAD_DIM = 64
SSD_HEADS = 32
SSD_GROUPS = 4
SSD_STATE = 128
GROUP_COLS = D_SSD // SSD_GROUPS
CONV_K = 5
CONV_CH = 3072
CHUNK = 128
ATT_HEADS = 16
ATT_HEAD_DIM = 64
ATT_V_DIM = 128
ROPE_THETA = 10000.0
PEER_HEADS = 8
PEER_NKEYS = 128
PEER_EXPERTS = PEER_NKEYS * PEER_NKEYS
PEER_TOPK = 16
PLE_DIM = 256
EPS = 1e-6
LOG2E = math.log2(math.e)

ZX_COLS = D_SSD + CONV_CH
DT_COL0 = ZX_COLS
REST_COL0 = ZX_COLS + 2 * SSD_HEADS
LANES = 128
BF16_ROWS = 16

VMEM_LIMIT = 56 * 1024 * 1024

NT = (((1,), (1,)), ((), ()))
TN = (((0,), (0,)), ((), ()))


def _params(*sem):
    return pltpu.CompilerParams(dimension_semantics=sem, vmem_limit_bytes=VMEM_LIMIT)


def _sigmoid(x):
    return 1.0 / (1.0 + jnp.exp(-x))


def _softplus(x):
    return jnp.maximum(x, 0.0) + jnp.log1p(jnp.exp(-jnp.abs(x)))


def _norm_kernel(x_ref, g_ref, h_ref):
    x = x_ref[...]
    ms = jnp.mean(x * x, axis=-1, keepdims=True)
    h_ref[...] = (x * lax.rsqrt(ms + EPS) * g_ref[...]).astype(h_ref.dtype)


def _rms_norm(x, g, tb=512):
    s, d = x.shape
    tb = min(tb, s)
    return pl.pallas_call(
        _norm_kernel,
        grid=(s // tb,),
        in_specs=[pl.BlockSpec((tb, d), lambda i: (i, 0)),
                  pl.BlockSpec((1, d), lambda i: (0, 0))],
        out_specs=pl.BlockSpec((tb, d), lambda i: (i, 0)),
        out_shape=jax.ShapeDtypeStruct((s, d), bf16),
        compiler_params=_params("parallel"),
        name="rms_norm",
    )(x, g.reshape(1, d))


def _add_norm_kernel(x_ref, d_ref, g_ref, xo_ref, h_ref):
    x = x_ref[...] + d_ref[...]
    xo_ref[...] = x
    ms = jnp.mean(x * x, axis=-1, keepdims=True)
    h_ref[...] = (x * lax.rsqrt(ms + EPS) * g_ref[...]).astype(h_ref.dtype)


def _add_rms_norm(x, delta, g, tb=512):
    s, d = x.shape
    tb = min(tb, s)
    row = pl.BlockSpec((tb, d), lambda i: (i, 0))
    return pl.pallas_call(
        _add_norm_kernel,
        grid=(s // tb,),
        in_specs=[row, row, pl.BlockSpec((1, d), lambda i: (0, 0))],
        out_specs=[row, row],
        out_shape=[jax.ShapeDtypeStruct((s, d), f32), jax.ShapeDtypeStruct((s, d), bf16)],
        compiler_params=_params("parallel"),
        name="add_rms_norm",
    )(x, delta, g.reshape(1, d))


def _mm_kernel(h_ref, w_ref, o_ref):
    o_ref[...] = jnp.dot(h_ref[...], w_ref[...].astype(bf16),
                         preferred_element_type=f32).astype(o_ref.dtype)


def _mm_res_kernel(h_ref, w_ref, r_ref, o_ref):
    o_ref[...] = r_ref[...] + jnp.dot(h_ref[...], w_ref[...].astype(bf16),
                                      preferred_element_type=f32)


def _matmul(h, w, col0, n, out_dtype, residual=None, tm=2048, tn=512):
    s, k = h.shape
    tm = min(tm, s)
    assert col0 % tn == 0 and n % tn == 0 and s % tm == 0
    cb = col0 // tn
    in_specs = [pl.BlockSpec((tm, k), lambda i, j: (i, 0)),
                pl.BlockSpec((k, tn), lambda i, j: (0, j + cb))]
    args = [h, w]
    kern = _mm_kernel
    if residual is not None:
        in_specs.append(pl.BlockSpec((tm, tn), lambda i, j: (i, j)))
        args.append(residual)
        kern = _mm_res_kernel
    return pl.pallas_call(
        kern,
        grid=(s // tm, n // tn),
        in_specs=in_specs,
        out_specs=pl.BlockSpec((tm, tn), lambda i, j: (i, j)),
        out_shape=jax.ShapeDtypeStruct((s, n), out_dtype),
        compiler_params=_params("parallel", "arbitrary"),
        name="matmul",
    )(*args)


def _mm_nt_kernel(h_ref, w_ref, o_ref):
    o_ref[...] = lax.dot_general(h_ref[...], w_ref[...].astype(bf16), NT,
                                 preferred_element_type=f32).astype(o_ref.dtype)


def _matmul_nt(h, w_t, row0, n, out_dtype, tm=2048, tn=512):
    s, k = h.shape
    tm = min(tm, s)
    assert row0 % BF16_ROWS == 0 and n % tn == 0 and s % tm == 0 and row0 + n <= w_t.shape[0]
    return pl.pallas_call(
        _mm_nt_kernel,
        grid=(s // tm, n // tn),
        in_specs=[pl.BlockSpec((tm, k), lambda i, j: (i, 0)),
                  pl.BlockSpec((pl.Element(tn), pl.Element(k)),
                               lambda i, j: ((row0 // 8 + j * (tn // 8)) * 8, 0))],
        out_specs=pl.BlockSpec((tm, tn), lambda i, j: (i, j)),
        out_shape=jax.ShapeDtypeStruct((s, n), out_dtype),
        compiler_params=_params("parallel", "arbitrary"),
        name="matmul_nt",
    )(h, w_t)


def _dt_kernel(h_ref, w_ref, dt_ref, dtt_ref):
    h = h_ref[...]
    w = w_ref[...].astype(bf16)
    dt_ref[...] = lax.dot_general(h, w, NT, preferred_element_type=f32)
    dtt_ref[...] = lax.dot_general(w, h, NT, preferred_element_type=f32)


def _dt_proj(h, w_t, row0, n, tm=1024):
    s, k = h.shape
    tm = min(tm, s)
    assert row0 % n == 0
    return pl.pallas_call(
        _dt_kernel,
        grid=(s // tm,),
        in_specs=[pl.BlockSpec((tm, k), lambda i: (i, 0)),
                  pl.BlockSpec((n, k), lambda i: (row0 // n, 0))],
        out_specs=[pl.BlockSpec((tm, n), lambda i: (i, 0)),
                   pl.BlockSpec((n, tm), lambda i: (0, i))],
        out_shape=[jax.ShapeDtypeStruct((s, n), f32), jax.ShapeDtypeStruct((n, s), f32)],
        compiler_params=_params("parallel"),
        name="dt_proj",
    )(h, w_t)


def _conv_kernel(prev_ref, main_ref, next_ref, w_ref, b_ref, o_ref):
    i = pl.program_id(0)
    last = pl.num_programs(0) - 1
    tb = main_ref.shape[0]
    halo = prev_ref.shape[0]
    pv = jnp.where(i > 0, prev_ref[...].astype(f32), 0.0)
    nx = jnp.where(i < last, next_ref[...].astype(f32), 0.0)
    ext = jnp.concatenate([pv, main_ref[...].astype(f32), nx], axis=0)
    acc = b_ref[...] + jnp.zeros((tb, main_ref.shape[1]), f32)
    for k in range(CONV_K):
        off = halo + k - CONV_K // 2
        acc = acc + w_ref[k:k + 1, :] * ext[off:off + tb]
    o_ref[...] = (acc * _sigmoid(acc)).astype(o_ref.dtype)


def _conv_silu(zx, conv_w, conv_b, tb=512, tc=1024):
    s = zx.shape[0]
    tb = min(tb, s)
    halo = BF16_ROWS
    rb = tb // halo
    nhalo = s // halo
    cb = D_SSD // tc
    return pl.pallas_call(
        _conv_kernel,
        grid=(s // tb, CONV_CH // tc),
        in_specs=[
            pl.BlockSpec((halo, tc), lambda i, j: (jnp.maximum(i * rb - 1, 0), j + cb)),
            pl.BlockSpec((tb, tc), lambda i, j: (i, j + cb)),
            pl.BlockSpec((halo, tc), lambda i, j: (jnp.minimum((i + 1) * rb, nhalo - 1), j + cb)),
            pl.BlockSpec((CONV_K, tc), lambda i, j: (0, j)),
            pl.BlockSpec((1, tc), lambda i, j: (0, j)),
        ],
        out_specs=pl.BlockSpec((tb, tc), lambda i, j: (i, j)),
        out_shape=jax.ShapeDtypeStruct((s, CONV_CH), bf16),
        compiler_params=_params("parallel", "parallel"),
        name="conv_silu",
    )(zx, zx, zx, conv_w, conv_b.reshape(1, CONV_CH))


def _ssd_direction(xs, bm, cm, dt_raw, dtt_raw, bias, bias_t, a_log, a_log_t, expand,
                   state_ref, reverse):
    L = CHUNK
    dt = _softplus(dt_raw + bias)
    dtt = _softplus(dtt_raw + bias_t)
    a = dt * (-jnp.exp(a_log))
    at = dtt * (-jnp.exp(a_log_t))
    ri = lax.broadcasted_iota(jnp.int32, (L, L), 0)
    ci = lax.broadcasted_iota(jnp.int32, (L, L), 1)
    causal = (ri <= ci) if reverse else (ri >= ci)
    cum_l = jnp.where(causal, 1.0, 0.0).astype(f32)
    cum_r = jnp.where((ri >= ci) if reverse else (ri <= ci), 1.0, 0.0).astype(f32)
    acs = jnp.dot(cum_l, a, preferred_element_type=f32, precision=lax.Precision.HIGHEST)
    acst = jnp.dot(at, cum_r, preferred_element_type=f32, precision=lax.Precision.HIGHEST)
    edge = 0 if reverse else L - 1
    acs_end = acs[edge:edge + 1, :]
    end_decay = jnp.exp(acs_end)
    end_hi = end_decay.astype(bf16)
    end_lo = (end_decay - end_hi.astype(f32)).astype(bf16)
    small = jnp.concatenate([(dt * jnp.exp(acs_end - acs)).astype(bf16), jnp.exp(acs).astype(bf16),
                             jnp.broadcast_to(end_hi, (8, SSD_HEADS)),
                             jnp.broadcast_to(end_lo, (8, SSD_HEADS))], axis=0)
    wide = jnp.dot(small, expand, preferred_element_type=f32)
    dinx, eacsx = wide[:L], wide[L:2 * L]
    cdec = wide[2 * L:2 * L + 1] + wide[2 * L + 8:2 * L + 9]
    xs_b = xs.astype(bf16)
    xdec_b = (xs * dinx).astype(bf16)
    lane = lax.broadcasted_iota(jnp.int32, (L, LANES), 1)
    ys = []
    for g in range(SSD_GROUPS):
        bg = bm[:, g * SSD_STATE:(g + 1) * SSD_STATE]
        cg = cm[:, g * SSD_STATE:(g + 1) * SSD_STATE]
        gs = slice(g * GROUP_COLS, (g + 1) * GROUP_COLS)
        cb = lax.dot_general(cg, bg, NT, preferred_element_type=f32)
        h_in = state_ref[:, gs]
        y_off = jnp.dot(cg, h_in.astype(bf16), preferred_element_type=f32) * eacsx[:, gs]
        st = lax.dot_general(bg, xdec_b[:, gs], TN, preferred_element_type=f32)
        state_ref[:, gs] = h_in * cdec[:, gs] + st
        tiles = []
        for pair in range(GROUP_COLS // LANES):
            ms = []
            for sub in range(2):
                h = g * (SSD_HEADS // SSD_GROUPS) + pair * 2 + sub
                seg = acs[:, h:h + 1] - acst[h:h + 1, :]
                ms.append((jnp.where(causal, jnp.exp(seg), 0.0) * (cb * dtt[h:h + 1, :])).astype(bf16))
            col = g * GROUP_COLS + pair * LANES
            xp = xs_b[:, col:col + LANES]
            rhs = jnp.concatenate([jnp.where(lane < SSD_HEAD_DIM, xp, jnp.zeros_like(xp)),
                                   jnp.where(lane >= SSD_HEAD_DIM, xp, jnp.zeros_like(xp))], axis=0)
            tiles.append(jnp.dot(jnp.concatenate(ms, axis=1), rhs, preferred_element_type=f32))
        ys.append(jnp.concatenate(tiles, axis=1) + y_off)
    return jnp.concatenate(ys, axis=1)


def _ssd_kernel(xf_ref, bf_ref, cf_ref, dtf_ref, dttf_ref,
                xb_ref, bb_ref, cb_ref, dtb_ref, dttb_ref,
                bias_ref, biast_ref, alog_ref, alogt_ref, expand_ref,
                yf_ref, yb_ref, sf_ref, sb_ref):
    @pl.when(pl.program_id(0) == 0)
    def _():
        sf_ref[...] = jnp.zeros_like(sf_ref)
        sb_ref[...] = jnp.zeros_like(sb_ref)

    H = SSD_HEADS
    expand = expand_ref[...]
    yf_ref[...] = _ssd_direction(
        xf_ref[...].astype(f32), bf_ref[...], cf_ref[...],
        dtf_ref[:, :H], dttf_ref[:H, :], bias_ref[:, :H], biast_ref[:H, :],
        alog_ref[:, :H], alogt_ref[:H, :], expand, sf_ref, reverse=False).astype(yf_ref.dtype)
    yb_ref[...] = _ssd_direction(
        xb_ref[...].astype(f32), bb_ref[...], cb_ref[...],
        dtb_ref[:, H:], dttb_ref[H:, :], bias_ref[:, H:], biast_ref[H:, :],
        alog_ref[:, H:], alogt_ref[H:, :], expand, sb_ref, reverse=True).astype(yb_ref.dtype)


def _ssd_scan(xbc, dt, dtt, dt_bias, a_log):
    s = xbc.shape[0]
    nc = s // CHUNK
    gn = SSD_GROUPS * SSD_STATE
    bcol = D_SSD // gn
    fwd = lambda c: c
    bwd = lambda c: nc - 1 - c

    def chunk_specs(sel):
        return [pl.BlockSpec((CHUNK, D_SSD), lambda c: (sel(c), 0)),
                pl.BlockSpec((CHUNK, gn), lambda c: (sel(c), bcol)),
                pl.BlockSpec((CHUNK, gn), lambda c: (sel(c), bcol + 1)),
                pl.BlockSpec((CHUNK, 2 * SSD_HEADS), lambda c: (sel(c), 0)),
                pl.BlockSpec((2 * SSD_HEADS, CHUNK), lambda c: (0, sel(c)))]

    const = lambda shape: pl.BlockSpec(shape, lambda c: (0, 0))
    expand = (jnp.arange(D_SSD)[None, :] // SSD_HEAD_DIM == jnp.arange(SSD_HEADS)[:, None]).astype(bf16)
    return pl.pallas_call(
        _ssd_kernel,
        grid=(nc,),
        in_specs=chunk_specs(fwd) + chunk_specs(bwd) + [
            const((1, 2 * SSD_HEADS)), const((2 * SSD_HEADS, 1)),
            const((1, 2 * SSD_HEADS)), const((2 * SSD_HEADS, 1)),
            const((SSD_HEADS, D_SSD))],
        out_specs=[pl.BlockSpec((CHUNK, D_SSD), lambda c: (fwd(c), 0)),
                   pl.BlockSpec((CHUNK, D_SSD), lambda c: (bwd(c), 0))],
        out_shape=[jax.ShapeDtypeStruct((s, D_SSD), bf16)] * 2,
        scratch_shapes=[pltpu.VMEM((SSD_STATE, D_SSD), f32)] * 2,
        compiler_params=_params("arbitrary"),
        name="ssd_scan",
    )(xbc, xbc, xbc, dt, dtt, xbc, xbc, xbc, dt, dtt,
      dt_bias.reshape(1, -1), dt_bias.reshape(-1, 1), a_log.reshape(1, -1), a_log.reshape(-1, 1),
      expand)


def _ssd_post_kernel(yf_ref, yb_ref, xs_ref, z_ref, d_ref, g_ref, o_ref):
    z = z_ref[...].astype(f32)
    y = (yf_ref[...].astype(f32) + yb_ref[...].astype(f32)
         + d_ref[...] * xs_ref[...].astype(f32)) * (z * _sigmoid(z))
    ms = jnp.mean(y * y, axis=-1, keepdims=True)
    o_ref[...] = (y * lax.rsqrt(ms + EPS) * g_ref[...]).astype(o_ref.dtype)


def _ssd_post(y_f, y_b, xbc, zx, d_skip, g, tb=512):
    s = y_f.shape[0]
    tb = min(tb, s)
    row = pl.BlockSpec((tb, D_SSD), lambda i: (i, 0))
    vec = pl.BlockSpec((1, D_SSD), lambda i: (0, 0))
    return pl.pallas_call(
        _ssd_post_kernel,
        grid=(s // tb,),
        in_specs=[row, row, row, row, vec, vec],
        out_specs=row,
        out_shape=jax.ShapeDtypeStruct((s, D_SSD), bf16),
        compiler_params=_params("parallel"),
        name="ssd_post",
    )(y_f, y_b, xbc, zx, jnp.repeat(d_skip, SSD_HEAD_DIM).reshape(1, D_SSD), g.reshape(1, D_SSD))


def _rope_table_kernel(pos_ref, inv_ref, sign_ref, cos_ref, sin_ref):
    ang = pos_ref[...].astype(f32) * inv_ref[...]
    cos_ref[...] = jnp.cos(ang)
    sin_ref[...] = jnp.sin(ang) * sign_ref[...]


def _rope_tables(positions, tb=512):
    s = positions.shape[0]
    tb = min(tb, s)
    half = ATT_HEAD_DIM // 2
    inv = ROPE_THETA ** (-jnp.arange(0, ATT_HEAD_DIM, 2, dtype=f32) / ATT_HEAD_DIM)
    inv_t = jnp.tile(inv, LANES // half).reshape(1, LANES)
    sign = jnp.tile(jnp.concatenate([-jnp.ones((half,), f32), jnp.ones((half,), f32)]),
                    LANES // ATT_HEAD_DIM).reshape(1, LANES)
    vec = pl.BlockSpec((1, LANES), lambda i: (0, 0))
    row = pl.BlockSpec((tb, LANES), lambda i: (i, 0))
    return pl.pallas_call(
        _rope_table_kernel,
        grid=(s // tb,),
        in_specs=[pl.BlockSpec((tb, 1), lambda i: (i, 0)), vec, vec],
        out_specs=[row, row],
        out_shape=[jax.ShapeDtypeStruct((s, LANES), f32)] * 2,
        compiler_params=_params("parallel"),
        name="rope_tables",
    )(positions.reshape(s, 1), inv_t, sign)


def _qk_kernel(x_ref, g_ref, cos_ref, sin_ref, o_ref):
    which = pl.program_id(1)
    scale = jnp.where(which == 0, ATT_HEAD_DIM ** -0.5 * LOG2E, 1.0).astype(f32)
    g = g_ref[0] * scale
    cos = cos_ref[...]
    sin = sin_ref[...]
    r = lax.broadcasted_iota(jnp.int32, (LANES, LANES), 0) // ATT_HEAD_DIM
    c = lax.broadcasted_iota(jnp.int32, (LANES, LANES), 1) // ATT_HEAD_DIM
    seg = jnp.where(r == c, 1.0, 0.0).astype(bf16)
    lane = lax.broadcasted_iota(jnp.int32, cos.shape, 1)
    first_half = (lane % ATT_HEAD_DIM) < ATT_HEAD_DIM // 2
    for t in range(x_ref.shape[1] // LANES):
        x = x_ref[:, t * LANES:(t + 1) * LANES].astype(f32)
        sq = x * x
        hi = sq.astype(bf16)
        lo = (sq - hi.astype(f32)).astype(bf16)
        ss = (jnp.dot(hi, seg, preferred_element_type=f32)
              + jnp.dot(lo, seg, preferred_element_type=f32))
        xn = x * lax.rsqrt(ss * (1.0 / ATT_HEAD_DIM) + EPS) * g
        swapped = jnp.where(first_half,
                            pltpu.roll(xn, LANES - ATT_HEAD_DIM // 2, axis=1),
                            pltpu.roll(xn, ATT_HEAD_DIM // 2, axis=1))
        o_ref[:, t * LANES:(t + 1) * LANES] = (xn * cos + swapped * sin).astype(o_ref.dtype)


def _qk_prep(qkvg, q_g, k_g, cos_t, sin_t, tb=512):
    s = qkvg.shape[0]
    tb = min(tb, s)
    cols = 2 * ATT_HEADS * ATT_HEAD_DIM
    g2 = jnp.stack([jnp.tile(q_g, 2), jnp.tile(k_g, 2)]).reshape(2, 1, LANES)
    tab = pl.BlockSpec((tb, LANES), lambda i, j: (i, 0))
    return pl.pallas_call(
        _qk_kernel,
        grid=(s // tb, 2),
        in_specs=[pl.BlockSpec((tb, cols), lambda i, j: (i, j)),
                  pl.BlockSpec((1, 1, LANES), lambda i, j: (j, 0, 0)),
                  tab, tab],
        out_specs=pl.BlockSpec((tb, cols), lambda i, j: (i, j)),
        out_shape=jax.ShapeDtypeStruct((s, 2 * cols), bf16),
        compiler_params=_params("parallel", "parallel"),
        name="qk_prep",
    )(qkvg, g2, cos_t, sin_t)


def _attn_kernel(q_ref, k_ref, v_ref, lq1_ref, lk1_ref, lq2_ref, lk2_ref, g_ref, o_ref,
                 qm_ref, s_ref, p_ref, m_ref, l_ref, acc_ref, *, tk, strip, lam_init):
    tq = q_ref.shape[0]
    nk = k_ref.shape[0] // tk
    q = q_ref[...]
    lane = lax.broadcasted_iota(jnp.int32, q.shape, 1)
    zero = jnp.zeros_like(q)
    qm_ref[0] = jnp.where(lane < ATT_HEAD_DIM, q, zero)
    qm_ref[1] = jnp.where(lane >= ATT_HEAD_DIM, q, zero)
    m_ref[...] = jnp.full(m_ref.shape, -1e30, f32)
    l_ref[...] = jnp.zeros_like(l_ref)
    acc_ref[...] = jnp.zeros_like(acc_ref)

    def scores(c, slot):
        kc = k_ref[c * tk:(c + 1) * tk, :]
        for half in range(2):
            s_ref[slot, half] = lax.dot_general(qm_ref[half], kc, NT, preferred_element_type=f32)

    def absorb(c, slot):
        vc = v_ref[c * tk:(c + 1) * tk, :]
        for half in range(2):
            for r0 in range(0, tq, strip):
                rows = slice(r0, r0 + strip)
                s = s_ref[slot, half, rows, :]
                mx = s[:, :LANES]
                for t in range(1, tk // LANES):
                    mx = jnp.maximum(mx, s[:, t * LANES:(t + 1) * LANES])
                m_old = m_ref[half, rows, :]
                m_new = jnp.maximum(m_old, jnp.broadcast_to(jnp.max(mx, axis=-1, keepdims=True),
                                                            m_old.shape))
                alpha = jnp.exp2(m_old - m_new)
                shifted = jnp.concatenate([s[:, t * LANES:(t + 1) * LANES] - m_new
                                           for t in range(tk // LANES)], axis=1)
                p = jnp.exp2(shifted.astype(bf16))
                ps = p[:, :LANES]
                for t in range(1, tk // LANES):
                    ps = ps + p[:, t * LANES:(t + 1) * LANES]
                l_ref[half, rows, :] = alpha * l_ref[half, rows, :] + ps.astype(f32)
                acc_ref[half, rows, :] = alpha * acc_ref[half, rows, :]
                m_ref[half, rows, :] = m_new
                p_ref[half, rows, :] = p
            acc_ref[half] += jnp.dot(p_ref[half], vc, preferred_element_type=f32)

    scores(0, 0)
    for c in range(nk):
        if c + 1 < nk:
            scores(c + 1, (c + 1) % 2)
        absorb(c, c % 2)
    lam = (jnp.exp(jnp.sum(lq1_ref[...] * lk1_ref[...], axis=-1, keepdims=True))
           - jnp.exp(jnp.sum(lq2_ref[...] * lk2_ref[...], axis=-1, keepdims=True)) + lam_init)
    o1 = acc_ref[0] / jnp.sum(l_ref[0], axis=-1, keepdims=True)
    o2 = acc_ref[1] / jnp.sum(l_ref[1], axis=-1, keepdims=True)
    o = o1 - lam * o2
    ms = jnp.mean(o * o, axis=-1, keepdims=True)
    o_ref[...] = (o * lax.rsqrt(ms + EPS) * (g_ref[...] * (1.0 - lam_init))).astype(o_ref.dtype)


def _diff_attention(qk, qkvg, lam_q1, lam_k1, lam_q2, lam_k2, subln_g, lam_init,
                    tq=512, tk=512, strip=64):
    s = qk.shape[0]
    tq = min(tq, s)
    tk = min(tk, s)
    assert s % tk == 0 and tq % strip == 0
    kcol = 2 * ATT_HEADS * ATT_HEAD_DIM // LANES
    vcol = 2 * kcol
    vec = lambda n: pl.BlockSpec((1, n), lambda h, i: (0, 0))
    return pl.pallas_call(
        functools.partial(_attn_kernel, tk=tk, strip=strip, lam_init=lam_init),
        grid=(ATT_HEADS, s // tq),
        in_specs=[pl.BlockSpec((tq, LANES), lambda h, i: (i, h)),
                  pl.BlockSpec((s, LANES), lambda h, i: (0, kcol + h)),
                  pl.BlockSpec((s, LANES), lambda h, i: (0, vcol + h)),
                  vec(ATT_HEAD_DIM), vec(ATT_HEAD_DIM), vec(ATT_HEAD_DIM), vec(ATT_HEAD_DIM),
                  vec(ATT_V_DIM)],
        out_specs=pl.BlockSpec((tq, LANES), lambda h, i: (i, h)),
        out_shape=jax.ShapeDtypeStruct((s, ATT_HEADS * ATT_V_DIM), bf16),
        scratch_shapes=[pltpu.VMEM((2, tq, LANES), bf16),
                        pltpu.VMEM((2, 2, tq, tk), f32),
                        pltpu.VMEM((2, tq, tk), bf16),
                        pltpu.VMEM((2, tq, LANES), f32),
                        pltpu.VMEM((2, tq, LANES), f32),
                        pltpu.VMEM((2, tq, ATT_V_DIM), f32)],
        compiler_params=_params("parallel", "arbitrary"),
        name="diff_attention",
    )(qk, qk, qkvg, lam_q1.reshape(1, -1), lam_k1.reshape(1, -1), lam_q2.reshape(1, -1),
      lam_k2.reshape(1, -1), subln_g.reshape(1, -1))


def _merge_kernel(ys_ref, ya_ref, wa_ref, wb_ref, gs_ref, ga_ref, o_ref):
    a = jnp.dot(ys_ref[...], wa_ref[...].astype(bf16), preferred_element_type=f32)
    b = jnp.dot(ya_ref[...], wb_ref[...].astype(bf16), preferred_element_type=f32)
    mixed = _sigmoid(gs_ref[...].astype(f32)) * a + _sigmoid(ga_ref[...].astype(f32)) * b
    o_ref[...] = mixed.astype(o_ref.dtype)


def _merge(y_ssd, y_att, w_a, w_b, qkvg, tm=2048, tn=256):
    s = y_ssd.shape[0]
    tm = min(tm, s)
    gcol = 3 * 2 * ATT_HEADS * ATT_HEAD_DIM // tn
    nj = D_MODEL // tn
    row = pl.BlockSpec((tm, D_MODEL), lambda i, j: (i, 0))
    wcol = pl.BlockSpec((D_MODEL, tn), lambda i, j: (0, j))
    return pl.pallas_call(
        _merge_kernel,
        grid=(s // tm, nj),
        in_specs=[row, row, wcol, wcol,
                  pl.BlockSpec((tm, tn), lambda i, j: (i, gcol + j)),
                  pl.BlockSpec((tm, tn), lambda i, j: (i, gcol + nj + j))],
        out_specs=pl.BlockSpec((tm, tn), lambda i, j: (i, j)),
        out_shape=jax.ShapeDtypeStruct((s, D_MODEL), bf16),
        compiler_params=_params("parallel", "arbitrary"),
        name="merge",
    )(y_ssd, y_att, w_a, w_b, qkvg, qkvg)


def _take_topk(s, k, rank, payloads):
    big = jnp.float32(2 ** 30)
    vals, ranks, picked = [], [], [[] for _ in payloads]
    for _ in range(k):
        m = jnp.max(s, axis=0, keepdims=True)
        pos = jnp.min(jnp.where(s == m, rank, big), axis=0, keepdims=True)
        hit = rank == pos
        vals.append(m)
        ranks.append(pos)
        for out, pay in zip(picked, payloads):
            out.append(jnp.max(jnp.where(hit, pay, -1.0), axis=0, keepdims=True))
        s = jnp.where(hit, -jnp.inf, s)
    cat = lambda xs: jnp.concatenate(xs, axis=0)
    return cat(vals), cat(ranks), [cat(p) for p in picked]


def _pair_candidates(v1, i1, v2, i2):
    k = PEER_TOPK
    tb = v1.shape[1]
    row8 = lax.broadcasted_iota(jnp.int32, (8, tb), 0)
    neg = jnp.float32(-jnp.inf)
    sums, flat, c1, c2 = [], [], [], []
    for a in range(k // 2):
        nb = k // (a + 1)
        for b0 in range(0, nb, 8):
            piece = v1[a:a + 1] + v2[b0:b0 + 8]
            if nb - b0 < 8:
                piece = jnp.where(row8 < nb - b0, piece, neg)
            sums.append(piece)
            flat.append((row8 + (a * k + b0)).astype(f32))
            c1.append(jnp.broadcast_to(i1[a:a + 1], (8, tb)))
            c2.append(i2[b0:b0 + 8])
    sums.append(v1[k // 2:] + v2[0:1])
    flat.append(((row8 + k // 2) * k).astype(f32))
    c1.append(i1[k // 2:])
    c2.append(jnp.broadcast_to(i2[0:1], (8, tb)))
    cat = lambda xs: jnp.concatenate(xs, axis=0)
    return cat(sums), cat(flat), cat(c1), cat(c2)


def _topk_kernel(q_ref, keys_ref, i1_ref, i2_ref, gate_ref):
    tb = q_ref.shape[0]
    kk = PEER_TOPK
    key_iota = lax.broadcasted_iota(jnp.int32, (PEER_NKEYS, tb), 0).astype(f32)
    i1s, i2s, gates = [], [], []
    for h in range(PEER_HEADS):
        tops = []
        for half in range(2):
            c0 = (h * 2 + half) * PEER_NKEYS
            qh = q_ref[:, c0:c0 + PEER_NKEYS].astype(bf16)
            keys = keys_ref[h, half].astype(bf16)
            s = lax.dot_general(keys, qh, NT, preferred_element_type=f32)
            vals, idx, _ = _take_topk(s, kk, key_iota, [])
            tops.append((vals, idx))
        (v1, i1), (v2, i2) = tops
        cand, flat, c1, c2 = _pair_candidates(v1, i1, v2, i2)
        sc, _, (e1, e2) = _take_topk(cand, kk, flat, [c1, c2])
        e = jnp.exp(sc - jnp.max(sc, axis=0, keepdims=True))
        gates.append(e / jnp.sum(e, axis=0, keepdims=True))
        i1s.append(e1)
        i2s.append(e2)
    i1_ref[...] = jnp.concatenate(i1s, axis=0).T
    i2_ref[...] = jnp.concatenate(i2s, axis=0).T
    gate_ref[...] = jnp.concatenate(gates, axis=0).T


def _peer_topk(qp, keys, tb=256):
    s = qp.shape[0]
    tb = min(tb, s)
    nsel = PEER_HEADS * PEER_TOPK
    row = pl.BlockSpec((tb, nsel), lambda i: (i, 0))
    return pl.pallas_call(
        _topk_kernel,
        grid=(s // tb,),
        in_specs=[pl.BlockSpec((tb, qp.shape[1]), lambda i: (i, 0)),
                  pl.BlockSpec(keys.shape, lambda i: (0, 0, 0, 0))],
        out_specs=[row, row, row],
        out_shape=[jax.ShapeDtypeStruct((s, nsel), f32)] * 3,
        compiler_params=_params("parallel"),
        name="peer_topk",
    )(qp, keys)


ROUTE_PITCH = PEER_NKEYS + 8


def _route_kernel(i1_ref, i2_ref, gate_ref, w_ref, tile_ref):
    n = PEER_NKEYS
    tb = i1_ref.shape[0]
    sub = lax.broadcasted_iota(jnp.int32, (n, i1_ref.shape[1]), 0).astype(f32)

    def body(t, carry):
        i1 = i1_ref[pl.ds(t, 1), :]
        i2 = i2_ref[pl.ds(t, 1), :]
        g = gate_ref[pl.ds(t, 1), :]
        a = jnp.where(sub == i1, g, 0.0).astype(bf16)
        b = jnp.where(sub == i2, 1.0, 0.0).astype(bf16)
        tile_ref[pl.ds(pl.multiple_of(t * ROUTE_PITCH, 8), n), :] = lax.dot_general(
            a, b, NT, preferred_element_type=f32)
        return carry

    lax.fori_loop(0, tb, body, 0, unroll=64)
    for a in range(n):
        w_ref[:, a * n:(a + 1) * n] = tile_ref[pl.ds(a, tb, stride=ROUTE_PITCH), :].astype(w_ref.dtype)


def _peer_route(i1, i2, gate, tb=256):
    s, nsel = i1.shape
    tb = min(tb, s)
    row = pl.BlockSpec((tb, nsel), lambda i: (i, 0))
    return pl.pallas_call(
        _route_kernel,
        grid=(s // tb,),
        in_specs=[row, row, row],
        out_specs=pl.BlockSpec((tb, PEER_EXPERTS), lambda i: (i, 0)),
        out_shape=jax.ShapeDtypeStruct((s, PEER_EXPERTS), bf16),
        scratch_shapes=[pltpu.VMEM((tb * ROUTE_PITCH, PEER_NKEYS), f32)],
        compiler_params=_params("parallel"),
        name="peer_route",
    )(i1, i2, gate)


def _peer_dense_kernel(h_ref, u0_ref, ub_ref, un_ref, v_ref, w_ref, o_ref, a_ref, g_ref, *, strip):
    e = pl.program_id(1)
    te = ub_ref.shape[0]
    h = h_ref[...]

    def mix(slot, lo):
        for r0 in range(0, h_ref.shape[0], strip):
            rows = slice(r0, r0 + strip)
            a = a_ref[slot, rows, :]
            act = 0.5 * a * (1.0 + lax.erf(a * (2.0 ** -0.5)))
            g_ref[slot, rows, :] = (act * w_ref[rows, lo:lo + te].astype(f32)).astype(bf16)
        return jnp.dot(g_ref[slot], v_ref[lo:lo + te, :].astype(bf16), preferred_element_type=f32)

    @pl.when(e == 0)
    def _():
        a_ref[0] = lax.dot_general(h, u0_ref[...].astype(bf16), NT, preferred_element_type=f32)
        o_ref[...] = jnp.zeros_like(o_ref)

    a_ref[1] = lax.dot_general(h, ub_ref[...].astype(bf16), NT, preferred_element_type=f32)
    o_ref[...] += mix(0, 0)
    a_ref[0] = lax.dot_general(h, un_ref[...].astype(bf16), NT,
                               preferred_element_type=f32)
    o_ref[...] += mix(1, te)


def _peer_dense(h, u, v, w, tm=1024, te=256):
    s, d = h.shape
    tm = min(tm, s)
    nt = u.shape[0] // te
    utile = lambda f: pl.BlockSpec((te, d), lambda i, e: (f(e), 0))
    return pl.pallas_call(
        functools.partial(_peer_dense_kernel, strip=min(128, tm)),
        grid=(s // tm, nt // 2),
        in_specs=[pl.BlockSpec((tm, d), lambda i, e: (i, 0)),
                  utile(lambda e: 0),
                  utile(lambda e: 2 * e + 1),
                  utile(lambda e: jnp.minimum(2 * e + 2, nt - 1)),
                  pl.BlockSpec((2 * te, d), lambda i, e: (e, 0)),
                  pl.BlockSpec((tm, 2 * te), lambda i, e: (i, e))],
        out_specs=pl.BlockSpec((tm, d), lambda i, e: (i, 0)),
        out_shape=jax.ShapeDtypeStruct((s, d), f32),
        scratch_shapes=[pltpu.VMEM((2, tm, te), f32), pltpu.VMEM((2, tm, te), bf16)],
        compiler_params=_params("parallel", "arbitrary"),
        name="peer_dense",
    )(h, u, u, u, v, w)


def _ple_kernel(h_ref, wg_ref, p_ref, wp_ref, x_ref, o_ref):
    gate = _sigmoid(jnp.dot(h_ref[...], wg_ref[...].astype(bf16), preferred_element_type=f32))
    up = jnp.dot(p_ref[...].astype(bf16), wp_ref[...].astype(bf16), preferred_element_type=f32)
    o_ref[...] = x_ref[...] + gate * up


def _ple(h, w_gate, p, w_up, x, tm=2048, tn=512):
    s = h.shape[0]
    tm = min(tm, s)
    return pl.pallas_call(
        _ple_kernel,
        grid=(s // tm, D_MODEL // tn),
        in_specs=[pl.BlockSpec((tm, D_MODEL), lambda i, j: (i, 0)),
                  pl.BlockSpec((D_MODEL, tn), lambda i, j: (0, j)),
                  pl.BlockSpec((tm, PLE_DIM), lambda i, j: (i, 0)),
                  pl.BlockSpec((PLE_DIM, tn), lambda i, j: (0, j)),
                  pl.BlockSpec((tm, tn), lambda i, j: (i, j))],
        out_specs=pl.BlockSpec((tm, tn), lambda i, j: (i, j)),
        out_shape=jax.ShapeDtypeStruct((s, D_MODEL), f32),
        compiler_params=_params("parallel", "arbitrary"),
        name="ple",
    )(h, w_gate, p, w_up, x)


def _layer(i, x, p, pos, norm1_g, w_in, conv_w, conv_b, a_log_f, a_log_b, dt_bias_f, dt_bias_b,
           d_skip, ssd_norm_g, q_norm_g, k_norm_g, lam_q1, lam_k1, lam_q2, lam_k2, subln_g,
           w_ssd_br, w_att_br, w_out, norm2_g, peer_wq, peer_keys, peer_u, peer_v, norm3_g,
           ple_gate_w, ple_up_w):
    h = _rms_norm(x, norm1_g)
    w_t = w_in.T
    zx = _matmul_nt(h, w_t, 0, ZX_COLS, bf16)
    dt, dtt = _dt_proj(h, w_t, DT_COL0, 2 * SSD_HEADS)
    qkvg = _matmul_nt(h, w_t, REST_COL0, w_t.shape[0] - REST_COL0, bf16)

    xbc = _conv_silu(zx, conv_w, conv_b)
    y_f, y_b = _ssd_scan(xbc, dt, dtt, jnp.concatenate([dt_bias_f, dt_bias_b]),
                         jnp.concatenate([a_log_f, a_log_b]))
    y_ssd = _ssd_post(y_f, y_b, xbc, zx, d_skip, ssd_norm_g)

    cos_t, sin_t = _rope_tables(pos)
    qk = _qk_prep(qkvg, q_norm_g, k_norm_g, cos_t, sin_t)
    lam_init = 0.8 - 0.6 * math.exp(-0.3 * i)
    y_att = _diff_attention(qk, qkvg, lam_q1, lam_k1, lam_q2, lam_k2, subln_g, lam_init)

    mixed = _merge(y_ssd, y_att, w_ssd_br, w_att_br, qkvg)
    x = _matmul(mixed, w_out, 0, D_MODEL, f32, residual=x)

    h2 = _rms_norm(x, norm2_g)
    qp = _matmul(h2, peer_wq, 0, peer_wq.shape[1], f32)
    i1, i2, gate = _peer_topk(qp, peer_keys)
    w = _peer_route(i1, i2, gate)
    mix = _peer_dense(h2, peer_u, peer_v, w)
    x, h3 = _add_rms_norm(x, mix, norm3_g)

    return _ple(h3, ple_gate_w, p, ple_up_w, x)


def kernel(x, p, positions, norm1_g, w_in, conv_w, conv_b, a_log_f, a_log_b, dt_bias_f, dt_bias_b, d_skip, ssd_norm_g, q_norm_g, k_norm_g, lam_q1, lam_k1, lam_q2, lam_k2, subln_g, w_ssd_br, w_att_br, w_out, norm2_g, peer_wq, peer_keys, peer_u, peer_v, norm3_g, ple_gate_w, ple_up_w):
    batch, seq, d = x.shape
    depth = w_in.shape[0]
    outs = []
    for b in range(batch):
        xb = x[b]
        for i in range(depth):
            xb = _layer(i, xb, p[i, b], positions[b], norm1_g[i], w_in[i], conv_w[i], conv_b[i],
                        a_log_f[i], a_log_b[i], dt_bias_f[i], dt_bias_b[i], d_skip[i],
                        ssd_norm_g[i], q_norm_g[i], k_norm_g[i], lam_q1[i], lam_k1[i], lam_q2[i],
                        lam_k2[i], subln_g[i], w_ssd_br[i], w_att_br[i], w_out[i], norm2_g[i],
                        peer_wq[i], peer_keys[i], peer_u[i], peer_v[i], norm3_g[i],
                        ple_gate_w[i], ple_up_w[i])
        outs.append(xb)
    return jnp.stack(outs)
```

```python
import functools
import math

import jax
import jax.numpy as jnp
from jax import lax
from jax.experimental import pallas as pl
from jax.experimental.pallas import tpu as pltpu

f32 = jnp.float32
bf16 = jnp.bfloat16

D_MODEL = 2048
D_SSD = 2048
SSD_HEAD_DIM = 64
SSD_HEADS = 32
SSD_GROUPS = 4
SSD_STATE = 128
GROUP_COLS = D_SSD // SSD_GROUPS
CONV_K = 5
CONV_CH = 3072
CHUNK = 128
ATT_HEADS = 16
ATT_HEAD_DIM = 64
ATT_V_DIM = 128
ROPE_THETA = 10000.0
PEER_HEADS = 8
PEER_NKEYS = 128
PEER_EXPERTS = PEER_NKEYS * PEER_NKEYS
PEER_TOPK = 16
PLE_DIM = 256
EPS = 1e-6
LOG2E = math.log2(math.e)

ZX_COLS = D_SSD + CONV_CH
DT_COL0 = ZX_COLS
REST_COL0 = ZX_COLS + 2 * SSD_HEADS
LANES = 128
BF16_ROWS = 16

VMEM_LIMIT = 56 * 1024 * 1024

NT = (((1,), (1,)), ((), ()))
TN = (((0,), (0,)), ((), ()))


def _params(*sem):
    return pltpu.CompilerParams(dimension_semantics=sem, vmem_limit_bytes=VMEM_LIMIT)


def _sigmoid(x):
    return 1.0 / (1.0 + jnp.exp(-x))


def _softplus(x):
    return jnp.maximum(x, 0.0) + jnp.log1p(jnp.exp(-jnp.abs(x)))


def _norm_kernel(x_ref, g_ref, h_ref):
    x = x_ref[...]
    ms = jnp.mean(x * x, axis=-1, keepdims=True)
    h_ref[...] = (x * lax.rsqrt(ms + EPS) * g_ref[...]).astype(h_ref.dtype)


def _rms_norm(x, g, tb=512):
    s, d = x.shape
    tb = min(tb, s)
    return pl.pallas_call(
        _norm_kernel,
        grid=(s // tb,),
        in_specs=[pl.BlockSpec((tb, d), lambda i: (i, 0)),
                  pl.BlockSpec((1, d), lambda i: (0, 0))],
        out_specs=pl.BlockSpec((tb, d), lambda i: (i, 0)),
        out_shape=jax.ShapeDtypeStruct((s, d), bf16),
        compiler_params=_params("parallel"),
        name="rms_norm",
    )(x, g.reshape(1, d))


def _add_norm_kernel(x_ref, d_ref, g_ref, xo_ref, h_ref):
    x = x_ref[...] + d_ref[...]
    xo_ref[...] = x
    ms = jnp.mean(x * x, axis=-1, keepdims=True)
    h_ref[...] = (x * lax.rsqrt(ms + EPS) * g_ref[...]).astype(h_ref.dtype)


def _add_rms_norm(x, delta, g, tb=512):
    s, d = x.shape
    tb = min(tb, s)
    row = pl.BlockSpec((tb, d), lambda i: (i, 0))
    return pl.pallas_call(
        _add_norm_kernel,
        grid=(s // tb,),
        in_specs=[row, row, pl.BlockSpec((1, d), lambda i: (0, 0))],
        out_specs=[row, row],
        out_shape=[jax.ShapeDtypeStruct((s, d), f32), jax.ShapeDtypeStruct((s, d), bf16)],
        compiler_params=_params("parallel"),
        name="add_rms_norm",
    )(x, delta, g.reshape(1, d))


def _mm_kernel(h_ref, w_ref, o_ref):
    o_ref[...] = jnp.dot(h_ref[...], w_ref[...].astype(bf16),
                         preferred_element_type=f32).astype(o_ref.dtype)


def _mm_res_kernel(h_ref, w_ref, r_ref, o_ref):
    o_ref[...] = r_ref[...] + jnp.dot(h_ref[...], w_ref[...].astype(bf16),
                                      preferred_element_type=f32)


def _matmul(h, w, col0, n, out_dtype, residual=None, tm=2048, tn=512):
    s, k = h.shape
    tm = min(tm, s)
    assert col0 % tn == 0 and n % tn == 0 and s % tm == 0
    cb = col0 // tn
    in_specs = [pl.BlockSpec((tm, k), lambda i, j: (i, 0)),
                pl.BlockSpec((k, tn), lambda i, j: (0, j + cb))]
    args = [h, w]
    kern = _mm_kernel
    if residual is not None:
        in_specs.append(pl.BlockSpec((tm, tn), lambda i, j: (i, j)))
        args.append(residual)
        kern = _mm_res_kernel
    return pl.pallas_call(
        kern,
        grid=(s // tm, n // tn),
        in_specs=in_specs,
        out_specs=pl.BlockSpec((tm, tn), lambda i, j: (i, j)),
        out_shape=jax.ShapeDtypeStruct((s, n), out_dtype),
        compiler_params=_params("parallel", "arbitrary"),
        name="matmul",
    )(*args)


def _mm_nt_kernel(h_ref, w_ref, o_ref):
    o_ref[...] = lax.dot_general(h_ref[...], w_ref[...].astype(bf16), NT,
                                 preferred_element_type=f32).astype(o_ref.dtype)


def _matmul_nt(h, w_t, row0, n, out_dtype, tm=2048, tn=512):
    s, k = h.shape
    tm = min(tm, s)
    assert row0 % BF16_ROWS == 0 and n % tn == 0 and s % tm == 0 and row0 + n <= w_t.shape[0]
    return pl.pallas_call(
        _mm_nt_kernel,
        grid=(s // tm, n // tn),
        in_specs=[pl.BlockSpec((tm, k), lambda i, j: (i, 0)),
                  pl.BlockSpec((pl.Element(tn), pl.Element(k)),
                               lambda i, j: ((row0 // 8 + j * (tn // 8)) * 8, 0))],
        out_specs=pl.BlockSpec((tm, tn), lambda i, j: (i, j)),
        out_shape=jax.ShapeDtypeStruct((s, n), out_dtype),
        compiler_params=_params("parallel", "arbitrary"),
        name="matmul_nt",
    )(h, w_t)


def _dt_kernel(h_ref, w_ref, dt_ref, dtt_ref):
    h = h_ref[...]
    w = w_ref[...].astype(bf16)
    dt_ref[...] = lax.dot_general(h, w, NT, preferred_element_type=f32)
    dtt_ref[...] = lax.dot_general(w, h, NT, preferred_element_type=f32)


def _dt_proj(h, w_t, row0, n, tm=1024):
    s, k = h.shape
    tm = min(tm, s)
    assert row0 % n == 0
    return pl.pallas_call(
        _dt_kernel,
        grid=(s // tm,),
        in_specs=[pl.BlockSpec((tm, k), lambda i: (i, 0)),
                  pl.BlockSpec((n, k), lambda i: (row0 // n, 0))],
        out_specs=[pl.BlockSpec((tm, n), lambda i: (i, 0)),
                   pl.BlockSpec((n, tm), lambda i: (0, i))],
        out_shape=[jax.ShapeDtypeStruct((s, n), f32), jax.ShapeDtypeStruct((n, s), f32)],
        compiler_params=_params("parallel"),
        name="dt_proj",
    )(h, w_t)


def _conv_kernel(prev_ref, main_ref, next_ref, w_ref, b_ref, o_ref):
    i = pl.program_id(0)
    last = pl.num_programs(0) - 1
    tb = main_ref.shape[0]
    halo = prev_ref.shape[0]
    pv = jnp.where(i > 0, prev_ref[...].astype(f32), 0.0)
    nx = jnp.where(i < last, next_ref[...].astype(f32), 0.0)
    ext = jnp.concatenate([pv, main_ref[...].astype(f32), nx], axis=0)
    acc = b_ref[...] + jnp.zeros((tb, main_ref.shape[1]), f32)
    for k in range(CONV_K):
        off = halo + k - CONV_K // 2
        acc = acc + w_ref[k:k + 1, :] * ext[off:off + tb]
    o_ref[...] = (acc * _sigmoid(acc)).astype(o_ref.dtype)


def _conv_silu(zx, conv_w, conv_b, tb=512, tc=1024):
    s = zx.shape[0]
    tb = min(tb, s)
    halo = BF16_ROWS
    rb = tb // halo
    nhalo = s // halo
    cb = D_SSD // tc
    return pl.pallas_call(
        _conv_kernel,
        grid=(s // tb, CONV_CH // tc),
        in_specs=[
            pl.BlockSpec((halo, tc), lambda i, j: (jnp.maximum(i * rb - 1, 0), j + cb)),
            pl.BlockSpec((tb, tc), lambda i, j: (i, j + cb)),
            pl.BlockSpec((halo, tc), lambda i, j: (jnp.minimum((i + 1) * rb, nhalo - 1), j + cb)),
            pl.BlockSpec((CONV_K, tc), lambda i, j: (0, j)),
            pl.BlockSpec((1, tc), lambda i, j: (0, j)),
        ],
        out_specs=pl.BlockSpec((tb, tc), lambda i, j: (i, j)),
        out_shape=jax.ShapeDtypeStruct((s, CONV_CH), bf16),
        compiler_params=_params("parallel", "parallel"),
        name="conv_silu",
    )(zx, zx, zx, conv_w, conv_b.reshape(1, CONV_CH))


def _ssd_direction(xs, bm, cm, dt_raw, dtt_raw, bias, bias_t, a_log, a_log_t, expand,
                   state_ref, reverse):
    L = CHUNK
    dt = _softplus(dt_raw + bias)
    dtt = _softplus(dtt_raw + bias_t)
    a = dt * (-jnp.exp(a_log))
    at = dtt * (-jnp.exp(a_log_t))
    ri = lax.broadcasted_iota(jnp.int32, (L, L), 0)
    ci = lax.broadcasted_iota(jnp.int32, (L, L), 1)
    causal = (ri <= ci) if reverse else (ri >= ci)
    cum_l = jnp.where(causal, 1.0, 0.0).astype(f32)
    cum_r = jnp.where((ri >= ci) if reverse else (ri <= ci), 1.0, 0.0).astype(f32)
    acs = jnp.dot(cum_l, a, preferred_element_type=f32, precision=lax.Precision.HIGHEST)
    acst = jnp.dot(at, cum_r, preferred_element_type=f32, precision=lax.Precision.HIGHEST)
    edge = 0 if reverse else L - 1
    acs_end = acs[edge:edge + 1, :]
    end_decay = jnp.exp(acs_end)
    end_hi = end_decay.astype(bf16)
    end_lo = (end_decay - end_hi.astype(f32)).astype(bf16)
    small = jnp.concatenate([(dt * jnp.exp(acs_end - acs)).astype(bf16), jnp.exp(acs).astype(bf16),
                             jnp.broadcast_to(end_hi, (8, SSD_HEADS)),
                             jnp.broadcast_to(end_lo, (8, SSD_HEADS))], axis=0)
    wide = jnp.dot(small, expand, preferred_element_type=f32)
    dinx, eacsx = wide[:L], wide[L:2 * L]
    cdec = wide[2 * L:2 * L + 1] + wide[2 * L + 8:2 * L + 9]
    xs_b = xs.astype(bf16)
    xdec_b = (xs * dinx).astype(bf16)
    lane = lax.broadcasted_iota(jnp.int32, (L, LANES), 1)
    ys = []
    for g in range(SSD_GROUPS):
        bg = bm[:, g * SSD_STATE:(g + 1) * SSD_STATE]
        cg = cm[:, g * SSD_STATE:(g + 1) * SSD_STATE]
        gs = slice(g * GROUP_COLS, (g + 1) * GROUP_COLS)
        cb = lax.dot_general(cg, bg, NT, preferred_element_type=f32)
        h_in = state_ref[:, gs]
        y_off = jnp.dot(cg, h_in.astype(bf16), preferred_element_type=f32) * eacsx[:, gs]
        st = lax.dot_general(bg, xdec_b[:, gs], TN, preferred_element_type=f32)
        state_ref[:, gs] = h_in * cdec[:, gs] + st
        tiles = []
        for pair in range(GROUP_COLS // LANES):
            ms = []
            for sub in range(2):
                h = g * (SSD_HEADS // SSD_GROUPS) + pair * 2 + sub
                seg = acs[:, h:h + 1] - acst[h:h + 1, :]
                ms.append((jnp.where(causal, jnp.exp(seg), 0.0) * (cb * dtt[h:h + 1, :])).astype(bf16))
            col = g * GROUP_COLS + pair * LANES
            xp = xs_b[:, col:col + LANES]
            rhs = jnp.concatenate([jnp.where(lane < SSD_HEAD_DIM, xp, jnp.zeros_like(xp)),
                                   jnp.where(lane >= SSD_HEAD_DIM, xp, jnp.zeros_like(xp))], axis=0)
            tiles.append(jnp.dot(jnp.concatenate(ms, axis=1), rhs, preferred_element_type=f32))
        ys.append(jnp.concatenate(tiles, axis=1) + y_off)
    return jnp.concatenate(ys, axis=1)


def _ssd_kernel(xf_ref, bf_ref, cf_ref, dtf_ref, dttf_ref,
                xb_ref, bb_ref, cb_ref, dtb_ref, dttb_ref,
                bias_ref, biast_ref, alog_ref, alogt_ref, expand_ref,
                yf_ref, yb_ref, sf_ref, sb_ref):
    @pl.when(pl.program_id(0) == 0)
    def _():
        sf_ref[...] = jnp.zeros_like(sf_ref)
        sb_ref[...] = jnp.zeros_like(sb_ref)

    H = SSD_HEADS
    expand = expand_ref[...]
    yf_ref[...] = _ssd_direction(
        xf_ref[...].astype(f32), bf_ref[...], cf_ref[...],
        dtf_ref[:, :H], dttf_ref[:H, :], bias_ref[:, :H], biast_ref[:H, :],
        alog_ref[:, :H], alogt_ref[:H, :], expand, sf_ref, reverse=False).astype(yf_ref.dtype)
    yb_ref[...] = _ssd_direction(
        xb_ref[...].astype(f32), bb_ref[...], cb_ref[...],
        dtb_ref[:, H:], dttb_ref[H:, :], bias_ref[:, H:], biast_ref[H:, :],
        alog_ref[:, H:], alogt_ref[H:, :], expand, sb_ref, reverse=True).astype(yb_ref.dtype)


def _ssd_scan(xbc, dt, dtt, dt_bias, a_log):
    s = xbc.shape[0]
    nc = s // CHUNK
    gn = SSD_GROUPS * SSD_STATE
    bcol = D_SSD // gn
    fwd = lambda c: c
    bwd = lambda c: nc - 1 - c

    def chunk_specs(sel):
        return [pl.BlockSpec((CHUNK, D_SSD), lambda c: (sel(c), 0)),
                pl.BlockSpec((CHUNK, gn), lambda c: (sel(c), bcol)),
                pl.BlockSpec((CHUNK, gn), lambda c: (sel(c), bcol + 1)),
                pl.BlockSpec((CHUNK, 2 * SSD_HEADS), lambda c: (sel(c), 0)),
                pl.BlockSpec((2 * SSD_HEADS, CHUNK), lambda c: (0, sel(c)))]

    const = lambda shape: pl.BlockSpec(shape, lambda c: (0, 0))
    expand = (jnp.arange(D_SSD)[None, :] // SSD_HEAD_DIM == jnp.arange(SSD_HEADS)[:, None]).astype(bf16)
    return pl.pallas_call(
        _ssd_kernel,
        grid=(nc,),
        in_specs=chunk_specs(fwd) + chunk_specs(bwd) + [
            const((1, 2 * SSD_HEADS)), const((2 * SSD_HEADS, 1)),
            const((1, 2 * SSD_HEADS)), const((2 * SSD_HEADS, 1)),
            const((SSD_HEADS, D_SSD))],
        out_specs=[pl.BlockSpec((CHUNK, D_SSD), lambda c: (fwd(c), 0)),
                   pl.BlockSpec((CHUNK, D_SSD), lambda c: (bwd(c), 0))],
        out_shape=[jax.ShapeDtypeStruct((s, D_SSD), bf16)] * 2,
        scratch_shapes=[pltpu.VMEM((SSD_STATE, D_SSD), f32)] * 2,
        compiler_params=_params("arbitrary"),
        name="ssd_scan",
    )(xbc, xbc, xbc, dt, dtt, xbc, xbc, xbc, dt, dtt,
      dt_bias.reshape(1, -1), dt_bias.reshape(-1, 1), a_log.reshape(1, -1), a_log.reshape(-1, 1),
      expand)


def _ssd_post_kernel(yf_ref, yb_ref, xs_ref, z_ref, d_ref, g_ref, o_ref):
    z = z_ref[...].astype(f32)
    y = (yf_ref[...].astype(f32) + yb_ref[...].astype(f32)
         + d_ref[...] * xs_ref[...].astype(f32)) * (z * _sigmoid(z))
    ms = jnp.mean(y * y, axis=-1, keepdims=True)
    o_ref[...] = (y * lax.rsqrt(ms + EPS) * g_ref[...]).astype(o_ref.dtype)


def _ssd_post(y_f, y_b, xbc, zx, d_skip, g, tb=512):
    s = y_f.shape[0]
    tb = min(tb, s)
    row = pl.BlockSpec((tb, D_SSD), lambda i: (i, 0))
    vec = pl.BlockSpec((1, D_SSD), lambda i: (0, 0))
    return pl.pallas_call(
        _ssd_post_kernel,
        grid=(s // tb,),
        in_specs=[row, row, row, row, vec, vec],
        out_specs=row,
        out_shape=jax.ShapeDtypeStruct((s, D_SSD), bf16),
        compiler_params=_params("parallel"),
        name="ssd_post",
    )(y_f, y_b, xbc, zx, jnp.repeat(d_skip, SSD_HEAD_DIM).reshape(1, D_SSD), g.reshape(1, D_SSD))


def _rope_table_kernel(pos_ref, inv_ref, sign_ref, cos_ref, sin_ref):
    ang = pos_ref[...].astype(f32) * inv_ref[...]
    cos_ref[...] = jnp.cos(ang)
    sin_ref[...] = jnp.sin(ang) * sign_ref[...]


def _rope_tables(positions, tb=512):
    s = positions.shape[0]
    tb = min(tb, s)
    half = ATT_HEAD_DIM // 2
    inv = ROPE_THETA ** (-jnp.arange(0, ATT_HEAD_DIM, 2, dtype=f32) / ATT_HEAD_DIM)
    inv_t = jnp.tile(inv, LANES // half).reshape(1, LANES)
    sign = jnp.tile(jnp.concatenate([-jnp.ones((half,), f32), jnp.ones((half,), f32)]),
                    LANES // ATT_HEAD_DIM).reshape(1, LANES)
    vec = pl.BlockSpec((1, LANES), lambda i: (0, 0))
    row = pl.BlockSpec((tb, LANES), lambda i: (i, 0))
    return pl.pallas_call(
        _rope_table_kernel,
        grid=(s // tb,),
        in_specs=[pl.BlockSpec((tb, 1), lambda i: (i, 0)), vec, vec],
        out_specs=[row, row],
        out_shape=[jax.ShapeDtypeStruct((s, LANES), f32)] * 2,
        compiler_params=_params("parallel"),
        name="rope_tables",
    )(positions.reshape(s, 1), inv_t, sign)


def _qk_kernel(x_ref, g_ref, cos_ref, sin_ref, o_ref):
    which = pl.program_id(1)
    scale = jnp.where(which == 0, ATT_HEAD_DIM ** -0.5 * LOG2E, 1.0).astype(f32)
    g = g_ref[0] * scale
    cos = cos_ref[...]
    sin = sin_ref[...]
    r = lax.broadcasted_iota(jnp.int32, (LANES, LANES), 0) // ATT_HEAD_DIM
    c = lax.broadcasted_iota(jnp.int32, (LANES, LANES), 1) // ATT_HEAD_DIM
    seg = jnp.where(r == c, 1.0, 0.0).astype(bf16)
    lane = lax.broadcasted_iota(jnp.int32, cos.shape, 1)
    first_half = (lane % ATT_HEAD_DIM) < ATT_HEAD_DIM // 2
    for t in range(x_ref.shape[1] // LANES):
        x = x_ref[:, t * LANES:(t + 1) * LANES].astype(f32)
        sq = x * x
        hi = sq.astype(bf16)
        lo = (sq - hi.astype(f32)).astype(bf16)
        ss = (jnp.dot(hi, seg, preferred_element_type=f32)
              + jnp.dot(lo, seg, preferred_element_type=f32))
        xn = x * lax.rsqrt(ss * (1.0 / ATT_HEAD_DIM) + EPS) * g
        swapped = jnp.where(first_half,
                            pltpu.roll(xn, LANES - ATT_HEAD_DIM // 2, axis=1),
                            pltpu.roll(xn, ATT_HEAD_DIM // 2, axis=1))
        o_ref[:, t * LANES:(t + 1) * LANES] = (xn * cos + swapped * sin).astype(o_ref.dtype)


def _qk_prep(qkvg, q_g, k_g, cos_t, sin_t, tb=512):
    s = qkvg.shape[0]
    tb = min(tb, s)
    cols = 2 * ATT_HEADS * ATT_HEAD_DIM
    g2 = jnp.stack([jnp.tile(q_g, 2), jnp.tile(k_g, 2)]).reshape(2, 1, LANES)
    tab = pl.BlockSpec((tb, LANES), lambda i, j: (i, 0))
    return pl.pallas_call(
        _qk_kernel,
        grid=(s // tb, 2),
        in_specs=[pl.BlockSpec((tb, cols), lambda i, j: (i, j)),
                  pl.BlockSpec((1, 1, LANES), lambda i, j: (j, 0, 0)),
                  tab, tab],
        out_specs=pl.BlockSpec((tb, cols), lambda i, j: (i, j)),
        out_shape=jax.ShapeDtypeStruct((s, 2 * cols), bf16),
        compiler_params=_params("parallel", "parallel"),
        name="qk_prep",
    )(qkvg, g2, cos_t, sin_t)


def _attn_kernel(q_ref, k_ref, v_ref, lq1_ref, lk1_ref, lq2_ref, lk2_ref, g_ref, o_ref,
                 qm_ref, s_ref, p_ref, m_ref, l_ref, acc_ref, *, tk, strip, lam_init):
    tq = q_ref.shape[0]
    nk = k_ref.shape[0] // tk
    q = q_ref[...]
    lane = lax.broadcasted_iota(jnp.int32, q.shape, 1)
    zero = jnp.zeros_like(q)
    qm_ref[0] = jnp.where(lane < ATT_HEAD_DIM, q, zero)
    qm_ref[1] = jnp.where(lane >= ATT_HEAD_DIM, q, zero)
    m_ref[...] = jnp.full(m_ref.shape, -1e30, f32)
    l_ref[...] = jnp.zeros_like(l_ref)
    acc_ref[...] = jnp.zeros_like(acc_ref)

    def scores(c, slot):
        kc = k_ref[c * tk:(c + 1) * tk, :]
        for half in range(2):
            s_ref[slot, half] = lax.dot_general(qm_ref[half], kc, NT, preferred_element_type=f32)

    def absorb(c, slot):
        vc = v_ref[c * tk:(c + 1) * tk, :]
        for half in range(2):
            for r0 in range(0, tq, strip):
                rows = slice(r0, r0 + strip)
                s = s_ref[slot, half, rows, :]
                mx = s[:, :LANES]
                for t in range(1, tk // LANES):
                    mx = jnp.maximum(mx, s[:, t * LANES:(t + 1) * LANES])
                m_old = m_ref[half, rows, :]
                m_new = jnp.maximum(m_old, jnp.broadcast_to(jnp.max(mx, axis=-1, keepdims=True),
                                                            m_old.shape))
                alpha = jnp.exp2(m_old - m_new)
                shifted = jnp.concatenate([s[:, t * LANES:(t + 1) * LANES] - m_new
                                           for t in range(tk // LANES)], axis=1)
                p = jnp.exp2(shifted.astype(bf16))
                ps = p[:, :LANES]
                for t in range(1, tk // LANES):
                    ps = ps + p[:, t * LANES:(t + 1) * LANES]
                l_ref[half, rows, :] = alpha * l_ref[half, rows, :] + ps.astype(f32)
                acc_ref[half, rows, :] = alpha * acc_ref[half, rows, :]
                m_ref[half, rows, :] = m_new
                p_ref[half, rows, :] = p
            acc_ref[half] += jnp.dot(p_ref[half], vc, preferred_element_type=f32)

    scores(0, 0)
    for c in range(nk):
        if c + 1 < nk:
            scores(c + 1, (c + 1) % 2)
        absorb(c, c % 2)
    lam = (jnp.exp(jnp.sum(lq1_ref[...] * lk1_ref[...], axis=-1, keepdims=True))
           - jnp.exp(jnp.sum(lq2_ref[...] * lk2_ref[...], axis=-1, keepdims=True)) + lam_init)
    o1 = acc_ref[0] / jnp.sum(l_ref[0], axis=-1, keepdims=True)
    o2 = acc_ref[1] / jnp.sum(l_ref[1], axis=-1, keepdims=True)
    o = o1 - lam * o2
    ms = jnp.mean(o * o, axis=-1, keepdims=True)
    o_ref[...] = (o * lax.rsqrt(ms + EPS) * (g_ref[...] * (1.0 - lam_init))).astype(o_ref.dtype)


def _diff_attention(qk, qkvg, lam_q1, lam_k1, lam_q2, lam_k2, subln_g, lam_init,
                    tq=512, tk=512, strip=64):
    s = qk.shape[0]
    tq = min(tq, s)
    tk = min(tk, s)
    assert s % tk == 0 and tq % strip == 0
    kcol = 2 * ATT_HEADS * ATT_HEAD_DIM // LANES
    vcol = 2 * kcol
    vec = lambda n: pl.BlockSpec((1, n), lambda h, i: (0, 0))
    return pl.pallas_call(
        functools.partial(_attn_kernel, tk=tk, strip=strip, lam_init=lam_init),
        grid=(ATT_HEADS, s // tq),
        in_specs=[pl.BlockSpec((tq, LANES), lambda h, i: (i, h)),
                  pl.BlockSpec((s, LANES), lambda h, i: (0, kcol + h)),
                  pl.BlockSpec((s, LANES), lambda h, i: (0, vcol + h)),
                  vec(ATT_HEAD_DIM), vec(ATT_HEAD_DIM), vec(ATT_HEAD_DIM), vec(ATT_HEAD_DIM),
                  vec(ATT_V_DIM)],
        out_specs=pl.BlockSpec((tq, LANES), lambda h, i: (i, h)),
        out_shape=jax.ShapeDtypeStruct((s, ATT_HEADS * ATT_V_DIM), bf16),
        scratch_shapes=[pltpu.VMEM((2, tq, LANES), bf16),
                        pltpu.VMEM((2, 2, tq, tk), f32),
                        pltpu.VMEM((2, tq, tk), bf16),
                        pltpu.VMEM((2, tq, LANES), f32),
                        pltpu.VMEM((2, tq, LANES), f32),
                        pltpu.VMEM((2, tq, ATT_V_DIM), f32)],
        compiler_params=_params("parallel", "arbitrary"),
        name="diff_attention",
    )(qk, qk, qkvg, lam_q1.reshape(1, -1), lam_k1.reshape(1, -1), lam_q2.reshape(1, -1),
      lam_k2.reshape(1, -1), subln_g.reshape(1, -1))


def _merge_kernel(ys_ref, ya_ref, wa_ref, wb_ref, gs_ref, ga_ref, o_ref):
    a = jnp.dot(ys_ref[...], wa_ref[...].astype(bf16), preferred_element_type=f32)
    b = jnp.dot(ya_ref[...], wb_ref[...].astype(bf16), preferred_element_type=f32)
    mixed = _sigmoid(gs_ref[...].astype(f32)) * a + _sigmoid(ga_ref[...].astype(f32)) * b
    o_ref[...] = mixed.astype(o_ref.dtype)


def _merge(y_ssd, y_att, w_a, w_b, qkvg, tm=2048, tn=256):
    s = y_ssd.shape[0]
    tm = min(tm, s)
    gcol = 3 * 2 * ATT_HEADS * ATT_HEAD_DIM // tn
    nj = D_MODEL // tn
    row = pl.BlockSpec((tm, D_MODEL), lambda i, j: (i, 0))
    wcol = pl.BlockSpec((D_MODEL, tn), lambda i, j: (0, j))
    return pl.pallas_call(
        _merge_kernel,
        grid=(s // tm, nj),
        in_specs=[row, row, wcol, wcol,
                  pl.BlockSpec((tm, tn), lambda i, j: (i, gcol + j)),
                  pl.BlockSpec((tm, tn), lambda i, j: (i, gcol + nj + j))],
        out_specs=pl.BlockSpec((tm, tn), lambda i, j: (i, j)),
        out_shape=jax.ShapeDtypeStruct((s, D_MODEL), bf16),
        compiler_params=_params("parallel", "arbitrary"),
        name="merge",
    )(y_ssd, y_att, w_a, w_b, qkvg, qkvg)


def _take_topk(s, k, rank):
    big = jnp.float32(2 ** 30)
    vals, ranks = [], []
    for _ in range(k):
        m = jnp.max(s, axis=0, keepdims=True)
        pos = jnp.min(jnp.where(s == m, rank, big), axis=0, keepdims=True)
        vals.append(m)
        ranks.append(pos)
        s = jnp.where(rank == pos, -jnp.inf, s)
    return jnp.concatenate(vals, axis=0), jnp.concatenate(ranks, axis=0)


def _pair_candidates(v1, v2):
    k = PEER_TOPK
    tb = v1.shape[1]
    row8 = lax.broadcasted_iota(jnp.int32, (8, tb), 0)
    neg = jnp.float32(-jnp.inf)
    sums, flat = [], []
    for a in range(k // 2):
        nb = k // (a + 1)
        for b0 in range(0, nb, 8):
            piece = v1[a:a + 1] + v2[b0:b0 + 8]
            if nb - b0 < 8:
                piece = jnp.where(row8 < nb - b0, piece, neg)
            sums.append(piece)
            flat.append((row8 + (a * k + b0)).astype(f32))
    sums.append(v1[k // 2:] + v2[0:1])
    flat.append(((row8 + k // 2) * k).astype(f32))
    cat = lambda xs: jnp.concatenate(xs, axis=0)
    return cat(sums), cat(flat)


def _pick_rows(table, idx):
    out = jnp.zeros_like(idx)
    for r in range(table.shape[0]):
        out = jnp.where(idx == float(r), table[r:r + 1], out)
    return out


def _topk_kernel(q_ref, keys_ref, i1_ref, i2_ref, gate_ref):
    tb = q_ref.shape[0]
    kk = PEER_TOPK
    key_iota = lax.broadcasted_iota(jnp.int32, (PEER_NKEYS, tb), 0).astype(f32)
    i1s, i2s, gates = [], [], []
    for h in range(PEER_HEADS):
        tops = []
        for half in range(2):
            c0 = (h * 2 + half) * PEER_NKEYS
            qh = q_ref[:, c0:c0 + PEER_NKEYS].astype(bf16)
            keys = keys_ref[h, half].astype(bf16)
            s = lax.dot_general(keys, qh, NT, preferred_element_type=f32)
            tops.append(_take_topk(s, kk, key_iota))
        (v1, i1), (v2, i2) = tops
        cand, flat = _pair_candidates(v1, v2)
        sc, pos = _take_topk(cand, kk, flat)
        a = jnp.floor(pos * (1.0 / kk))
        e1 = _pick_rows(i1, a)
        e2 = _pick_rows(i2, pos - a * kk)
        e = jnp.exp(sc - jnp.max(sc, axis=0, keepdims=True))
        gates.append(e / jnp.sum(e, axis=0, keepdims=True))
        i1s.append(e1)
        i2s.append(e2)
    i1_ref[...] = jnp.concatenate(i1s, axis=0).T
    i2_ref[...] = jnp.concatenate(i2s, axis=0).T
    gate_ref[...] = jnp.concatenate(gates, axis=0).T


def _peer_topk(qp, keys, tb=256):
    s = qp.shape[0]
    tb = min(tb, s)
    nsel = PEER_HEADS * PEER_TOPK
    row = pl.BlockSpec((tb, nsel), lambda i: (i, 0))
    return pl.pallas_call(
        _topk_kernel,
        grid=(s // tb,),
        in_specs=[pl.BlockSpec((tb, qp.shape[1]), lambda i: (i, 0)),
                  pl.BlockSpec(keys.shape, lambda i: (0, 0, 0, 0))],
        out_specs=[row, row, row],
        out_shape=[jax.ShapeDtypeStruct((s, nsel), f32)] * 3,
        compiler_params=_params("parallel"),
        name="peer_topk",
    )(qp, keys)


ROUTE_PITCH = PEER_NKEYS + 8


def _route_kernel(i1_ref, i2_ref, gate_ref, w_ref, tile_ref):
    n = PEER_NKEYS
    tb = i1_ref.shape[0]
    sub = lax.broadcasted_iota(jnp.int32, (n, i1_ref.shape[1]), 0).astype(f32)

    def body(t, carry):
        i1 = i1_ref[pl.ds(t, 1), :]
        i2 = i2_ref[pl.ds(t, 1), :]
        g = gate_ref[pl.ds(t, 1), :]
        a = jnp.where(sub == i1, g, 0.0).astype(bf16)
        b = jnp.where(sub == i2, 1.0, 0.0).astype(bf16)
        tile_ref[pl.ds(pl.multiple_of(t * ROUTE_PITCH, 8), n), :] = lax.dot_general(
            a, b, NT, preferred_element_type=f32)
        return carry

    lax.fori_loop(0, tb, body, 0, unroll=64)
    for a in range(n):
        w_ref[:, a * n:(a + 1) * n] = tile_ref[pl.ds(a, tb, stride=ROUTE_PITCH), :].astype(w_ref.dtype)


def _peer_route(i1, i2, gate, tb=256):
    s, nsel = i1.shape
    tb = min(tb, s)
    row = pl.BlockSpec((tb, nsel), lambda i: (i, 0))
    return pl.pallas_call(
        _route_kernel,
        grid=(s // tb,),
        in_specs=[row, row, row],
        out_specs=pl.BlockSpec((tb, PEER_EXPERTS), lambda i: (i, 0)),
        out_shape=jax.ShapeDtypeStruct((s, PEER_EXPERTS), bf16),
        scratch_shapes=[pltpu.VMEM((tb * ROUTE_PITCH, PEER_NKEYS), f32)],
        compiler_params=_params("parallel"),
        name="peer_route",
    )(i1, i2, gate)


def _peer_dense_kernel(h_ref, u0_ref, ub_ref, un_ref, v_ref, w_ref, o_ref, a_ref, g_ref, *, strip):
    e = pl.program_id(1)
    te = ub_ref.shape[0]
    h = h_ref[...]

    def mix(slot, lo):
        for r0 in range(0, h_ref.shape[0], strip):
            rows = slice(r0, r0 + strip)
            a = a_ref[slot, rows, :]
            act = 0.5 * a * (1.0 + lax.erf(a * (2.0 ** -0.5)))
            g_ref[slot, rows, :] = (act * w_ref[rows, lo:lo + te].astype(f32)).astype(bf16)
        return jnp.dot(g_ref[slot], v_ref[lo:lo + te, :].astype(bf16), preferred_element_type=f32)

    @pl.when(e == 0)
    def _():
        a_ref[0] = lax.dot_general(h, u0_ref[...].astype(bf16), NT, preferred_element_type=f32)
        o_ref[...] = jnp.zeros_like(o_ref)

    a_ref[1] = lax.dot_general(h, ub_ref[...].astype(bf16), NT, preferred_element_type=f32)
    o_ref[...] += mix(0, 0)
    a_ref[0] = lax.dot_general(h, un_ref[...].astype(bf16), NT,
                               preferred_element_type=f32)
    o_ref[...] += mix(1, te)


def _peer_dense(h, u, v, w, tm=1024, te=256):
    s, d = h.shape
    tm = min(tm, s)
    nt = u.shape[0] // te
    utile = lambda f: pl.BlockSpec((te, d), lambda i, e: (f(e), 0))
    return pl.pallas_call(
        functools.partial(_peer_dense_kernel, strip=min(128, tm)),
        grid=(s // tm, nt // 2),
        in_specs=[pl.BlockSpec((tm, d), lambda i, e: (i, 0)),
                  utile(lambda e: 0),
                  utile(lambda e: 2 * e + 1),
                  utile(lambda e: jnp.minimum(2 * e + 2, nt - 1)),
                  pl.BlockSpec((2 * te, d), lambda i, e: (e, 0)),
                  pl.BlockSpec((tm, 2 * te), lambda i, e: (i, e))],
        out_specs=pl.BlockSpec((tm, d), lambda i, e: (i, 0)),
        out_shape=jax.ShapeDtypeStruct((s, d), f32),
        scratch_shapes=[pltpu.VMEM((2, tm, te), f32), pltpu.VMEM((2, tm, te), bf16)],
        compiler_params=_params("parallel", "arbitrary"),
        name="peer_dense",
    )(h, u, u, u, v, w)


def _ple_kernel(h_ref, wg_ref, p_ref, wp_ref, x_ref, o_ref):
    gate = _sigmoid(jnp.dot(h_ref[...], wg_ref[...].astype(bf16), preferred_element_type=f32))
    up = jnp.dot(p_ref[...].astype(bf16), wp_ref[...].astype(bf16), preferred_element_type=f32)
    o_ref[...] = x_ref[...] + gate * up


def _ple(h, w_gate, p, w_up, x, tm=2048, tn=512):
    s = h.shape[0]
    tm = min(tm, s)
    return pl.pallas_call(
        _ple_kernel,
        grid=(s // tm, D_MODEL // tn),
        in_specs=[pl.BlockSpec((tm, D_MODEL), lambda i, j: (i, 0)),
                  pl.BlockSpec((D_MODEL, tn), lambda i, j: (0, j)),
                  pl.BlockSpec((tm, PLE_DIM), lambda i, j: (i, 0)),
                  pl.BlockSpec((PLE_DIM, tn), lambda i, j: (0, j)),
                  pl.BlockSpec((tm, tn), lambda i, j: (i, j))],
        out_specs=pl.BlockSpec((tm, tn), lambda i, j: (i, j)),
        out_shape=jax.ShapeDtypeStruct((s, D_MODEL), f32),
        compiler_params=_params("parallel", "arbitrary"),
        name="ple",
    )(h, w_gate, p, w_up, x)


def _layer(i, x, p, pos, norm1_g, w_in, conv_w, conv_b, a_log_f, a_log_b, dt_bias_f, dt_bias_b,
           d_skip, ssd_norm_g, q_norm_g, k_norm_g, lam_q1, lam_k1, lam_q2, lam_k2, subln_g,
           w_ssd_br, w_att_br, w_out, norm2_g, peer_wq, peer_keys, peer_u, peer_v, norm3_g,
           ple_gate_w, ple_up_w):
    h = _rms_norm(x, norm1_g)
    w_t = w_in.T
    zx = _matmul_nt(h, w_t, 0, ZX_COLS, bf16)
    dt, dtt = _dt_proj(h, w_t, DT_COL0, 2 * SSD_HEADS)
    qkvg = _matmul_nt(h, w_t, REST_COL0, w_t.shape[0] - REST_COL0, bf16)

    xbc = _conv_silu(zx, conv_w, conv_b)
    y_f, y_b = _ssd_scan(xbc, dt, dtt, jnp.concatenate([dt_bias_f, dt_bias_b]),
                         jnp.concatenate([a_log_f, a_log_b]))
    y_ssd = _ssd_post(y_f, y_b, xbc, zx, d_skip, ssd_norm_g)

    cos_t, sin_t = _rope_tables(pos)
    qk = _qk_prep(qkvg, q_norm_g, k_norm_g, cos_t, sin_t)
    lam_init = 0.8 - 0.6 * math.exp(-0.3 * i)
    y_att = _diff_attention(qk, qkvg, lam_q1, lam_k1, lam_q2, lam_k2, subln_g, lam_init)

    mixed = _merge(y_ssd, y_att, w_ssd_br, w_att_br, qkvg)
    x = _matmul(mixed, w_out, 0, D_MODEL, f32, residual=x)

    h2 = _rms_norm(x, norm2_g)
    qp = _matmul(h2, peer_wq, 0, peer_wq.shape[1], f32)
    i1, i2, gate = _peer_topk(qp, peer_keys)
    w = _peer_route(i1, i2, gate)
    mix = _peer_dense(h2, peer_u, peer_v, w)
    x, h3 = _add_rms_norm(x, mix, norm3_g)

    return _ple(h3, ple_gate_w, p, ple_up_w, x)


def kernel(x, p, positions, norm1_g, w_in, conv_w, conv_b, a_log_f, a_log_b, dt_bias_f, dt_bias_b, d_skip, ssd_norm_g, q_norm_g, k_norm_g, lam_q1, lam_k1, lam_q2, lam_k2, subln_g, w_ssd_br, w_att_br, w_out, norm2_g, peer_wq, peer_keys, peer_u, peer_v, norm3_g, ple_gate_w, ple_up_w):
    batch, seq, d = x.shape
    depth = w_in.shape[0]
    outs = []
    for b in range(batch):
        xb = x[b]
        for i in range(depth):
            xb = _layer(i, xb, p[i, b], positions[b], norm1_g[i], w_in[i], conv_w[i], conv_b[i],
                        a_log_f[i], a_log_b[i], dt_bias_f[i], dt_bias_b[i], d_skip[i],
                        ssd_norm_g[i], q_norm_g[i], k_norm_g[i], lam_q1[i], lam_k1[i], lam_q2[i],
                        lam_k2[i], subln_g[i], w_ssd_br[i], w_att_br[i], w_out[i], norm2_g[i],
                        peer_wq[i], peer_keys[i], peer_u[i], peer_v[i], norm3_g[i],
                        ple_gate_w[i], ple_up_w[i])
        outs.append(xb)
    return jnp.stack(outs)
```

```python
import functools
import math

import jax
import jax.numpy as jnp
from jax import lax
from jax.experimental import pallas as pl
from jax.experimental.pallas import tpu as pltpu

f32 = jnp.float32
bf16 = jnp.bfloat16

D_MODEL = 2048
D_SSD = 2048
SSD_HEAD_DIM = 64
SSD_HEADS = 32
SSD_GROUPS = 4
SSD_STATE = 128
GROUP_COLS = D_SSD // SSD_GROUPS
CONV_K = 5
CONV_CH = 3072
CHUNK = 128
ATT_HEADS = 16
ATT_HEAD_DIM = 64
ATT_V_DIM = 128
ROPE_THETA = 10000.0
PEER_HEADS = 8
PEER_NKEYS = 128
PEER_EXPERTS = PEER_NKEYS * PEER_NKEYS
PEER_TOPK = 16
PLE_DIM = 256
EPS = 1e-6
LOG2E = math.log2(math.e)

ZX_COLS = D_SSD + CONV_CH
DT_COL0 = ZX_COLS
REST_COL0 = ZX_COLS + 2 * SSD_HEADS
LANES = 128
BF16_ROWS = 16

VMEM_LIMIT = 56 * 1024 * 1024

NT = (((1,), (1,)), ((), ()))
TN = (((0,), (0,)), ((), ()))


def _params(*sem):
    return pltpu.CompilerParams(dimension_semantics=sem, vmem_limit_bytes=VMEM_LIMIT)


def _sigmoid(x):
    return 1.0 / (1.0 + jnp.exp(-x))


def _softplus(x):
    return jnp.maximum(x, 0.0) + jnp.log1p(jnp.exp(-jnp.abs(x)))


def _norm_kernel(x_ref, g_ref, h_ref):
    x = x_ref[...]
    ms = jnp.mean(x * x, axis=-1, keepdims=True)
    h_ref[...] = (x * lax.rsqrt(ms + EPS) * g_ref[...]).astype(h_ref.dtype)


def _rms_norm(x, g, tb=512):
    s, d = x.shape
    tb = min(tb, s)
    return pl.pallas_call(
        _norm_kernel,
        grid=(s // tb,),
        in_specs=[pl.BlockSpec((tb, d), lambda i: (i, 0)),
                  pl.BlockSpec((1, d), lambda i: (0, 0))],
        out_specs=pl.BlockSpec((tb, d), lambda i: (i, 0)),
        out_shape=jax.ShapeDtypeStruct((s, d), bf16),
        compiler_params=_params("parallel"),
        name="rms_norm",
    )(x, g.reshape(1, d))


def _add_norm_kernel(x_ref, d_ref, g_ref, xo_ref, h_ref):
    x = x_ref[...] + d_ref[...]
    xo_ref[...] = x
    ms = jnp.mean(x * x, axis=-1, keepdims=True)
    h_ref[...] = (x * lax.rsqrt(ms + EPS) * g_ref[...]).astype(h_ref.dtype)


def _add_rms_norm(x, delta, g, tb=512):
    s, d = x.shape
    tb = min(tb, s)
    row = pl.BlockSpec((tb, d), lambda i: (i, 0))
    return pl.pallas_call(
        _add_norm_kernel,
        grid=(s // tb,),
        in_specs=[row, row, pl.BlockSpec((1, d), lambda i: (0, 0))],
        out_specs=[row, row],
        out_shape=[jax.ShapeDtypeStruct((s, d), f32), jax.ShapeDtypeStruct((s, d), bf16)],
        compiler_params=_params("parallel"),
        name="add_rms_norm",
    )(x, delta, g.reshape(1, d))


def _mm_kernel(h_ref, w_ref, o_ref):
    o_ref[...] = jnp.dot(h_ref[...], w_ref[...].astype(bf16),
                         preferred_element_type=f32).astype(o_ref.dtype)


def _mm_res_kernel(h_ref, w_ref, r_ref, o_ref):
    o_ref[...] = r_ref[...] + jnp.dot(h_ref[...], w_ref[...].astype(bf16),
                                      preferred_element_type=f32)


def _matmul(h, w, col0, n, out_dtype, residual=None, tm=2048, tn=512):
    s, k = h.shape
    tm = min(tm, s)
    assert col0 % tn == 0 and n % tn == 0 and s % tm == 0
    cb = col0 // tn
    in_specs = [pl.BlockSpec((tm, k), lambda i, j: (i, 0)),
                pl.BlockSpec((k, tn), lambda i, j: (0, j + cb))]
    args = [h, w]
    kern = _mm_kernel
    if residual is not None:
        in_specs.append(pl.BlockSpec((tm, tn), lambda i, j: (i, j)))
        args.append(residual)
        kern = _mm_res_kernel
    return pl.pallas_call(
        kern,
        grid=(s // tm, n // tn),
        in_specs=in_specs,
        out_specs=pl.BlockSpec((tm, tn), lambda i, j: (i, j)),
        out_shape=jax.ShapeDtypeStruct((s, n), out_dtype),
        compiler_params=_params("parallel", "arbitrary"),
        name="matmul",
    )(*args)


def _mm_nt_kernel(h_ref, w_ref, o_ref):
    o_ref[...] = lax.dot_general(h_ref[...], w_ref[...].astype(bf16), NT,
                                 preferred_element_type=f32).astype(o_ref.dtype)


def _matmul_nt(h, w_t, row0, n, out_dtype, tm=2048, tn=512):
    s, k = h.shape
    tm = min(tm, s)
    assert row0 % BF16_ROWS == 0 and n % tn == 0 and s % tm == 0 and row0 + n <= w_t.shape[0]
    return pl.pallas_call(
        _mm_nt_kernel,
        grid=(s // tm, n // tn),
        in_specs=[pl.BlockSpec((tm, k), lambda i, j: (i, 0)),
                  pl.BlockSpec((pl.Element(tn), pl.Element(k)),
                               lambda i, j: ((row0 // 8 + j * (tn // 8)) * 8, 0))],
        out_specs=pl.BlockSpec((tm, tn), lambda i, j: (i, j)),
        out_shape=jax.ShapeDtypeStruct((s, n), out_dtype),
        compiler_params=_params("parallel", "arbitrary"),
        name="matmul_nt",
    )(h, w_t)


def _dt_kernel(h_ref, w_ref, dt_ref, dtt_ref):
    h = h_ref[...]
    w = w_ref[...].astype(bf16)
    dt_ref[...] = lax.dot_general(h, w, NT, preferred_element_type=f32)
    dtt_ref[...] = lax.dot_general(w, h, NT, preferred_element_type=f32)


def _dt_proj(h, w_t, row0, n, tm=1024):
    s, k = h.shape
    tm = min(tm, s)
    assert row0 % n == 0
    return pl.pallas_call(
        _dt_kernel,
        grid=(s // tm,),
        in_specs=[pl.BlockSpec((tm, k), lambda i: (i, 0)),
                  pl.BlockSpec((n, k), lambda i: (row0 // n, 0))],
        out_specs=[pl.BlockSpec((tm, n), lambda i: (i, 0)),
                   pl.BlockSpec((n, tm), lambda i: (0, i))],
        out_shape=[jax.ShapeDtypeStruct((s, n), f32), jax.ShapeDtypeStruct((n, s), f32)],
        compiler_params=_params("parallel"),
        name="dt_proj",
    )(h, w_t)


def _conv_kernel(prev_ref, main_ref, next_ref, w_ref, b_ref, o_ref):
    i = pl.program_id(0)
    last = pl.num_programs(0) - 1
    tb = main_ref.shape[0]
    halo = prev_ref.shape[0]
    pv = jnp.where(i > 0, prev_ref[...].astype(f32), 0.0)
    nx = jnp.where(i < last, next_ref[...].astype(f32), 0.0)
    ext = jnp.concatenate([pv, main_ref[...].astype(f32), nx], axis=0)
    acc = b_ref[...] + jnp.zeros((tb, main_ref.shape[1]), f32)
    for k in range(CONV_K):
        off = halo + k - CONV_K // 2
        acc = acc + w_ref[k:k + 1, :] * ext[off:off + tb]
    o_ref[...] = (acc * _sigmoid(acc)).astype(o_ref.dtype)


def _conv_silu(zx, conv_w, conv_b, tb=512, tc=1024):
    s = zx.shape[0]
    tb = min(tb, s)
    halo = BF16_ROWS
    rb = tb // halo
    nhalo = s // halo
    cb = D_SSD // tc
    return pl.pallas_call(
        _conv_kernel,
        grid=(s // tb, CONV_CH // tc),
        in_specs=[
            pl.BlockSpec((halo, tc), lambda i, j: (jnp.maximum(i * rb - 1, 0), j + cb)),
            pl.BlockSpec((tb, tc), lambda i, j: (i, j + cb)),
            pl.BlockSpec((halo, tc), lambda i, j: (jnp.minimum((i + 1) * rb, nhalo - 1), j + cb)),
            pl.BlockSpec((CONV_K, tc), lambda i, j: (0, j)),
            pl.BlockSpec((1, tc), lambda i, j: (0, j)),
        ],
        out_specs=pl.BlockSpec((tb, tc), lambda i, j: (i, j)),
        out_shape=jax.ShapeDtypeStruct((s, CONV_CH), bf16),
        compiler_params=_params("parallel", "parallel"),
        name="conv_silu",
    )(zx, zx, zx, conv_w, conv_b.reshape(1, CONV_CH))


def _ssd_direction(xs, bm, cm, dt_raw, dtt_raw, bias, bias_t, a_log, a_log_t, expand,
                   state_ref, reverse):
    L = CHUNK
    dt = _softplus(dt_raw + bias)
    dtt = _softplus(dtt_raw + bias_t)
    a = dt * (-jnp.exp(a_log))
    at = dtt * (-jnp.exp(a_log_t))
    ri = lax.broadcasted_iota(jnp.int32, (L, L), 0)
    ci = lax.broadcasted_iota(jnp.int32, (L, L), 1)
    causal = (ri <= ci) if reverse else (ri >= ci)
    cum_l = jnp.where(causal, 1.0, 0.0).astype(f32)
    cum_r = jnp.where((ri >= ci) if reverse else (ri <= ci), 1.0, 0.0).astype(f32)
    acs = jnp.dot(cum_l, a, preferred_element_type=f32, precision=lax.Precision.HIGHEST)
    acst = jnp.dot(at, cum_r, preferred_element_type=f32, precision=lax.Precision.HIGHEST)
    edge = 0 if reverse else L - 1
    acs_end = acs[edge:edge + 1, :]
    end_decay = jnp.exp(acs_end)
    end_hi = end_decay.astype(bf16)
    end_lo = (end_decay - end_hi.astype(f32)).astype(bf16)
    small = jnp.concatenate([(dt * jnp.exp(acs_end - acs)).astype(bf16), jnp.exp(acs).astype(bf16),
                             jnp.broadcast_to(end_hi, (8, SSD_HEADS)),
                             jnp.broadcast_to(end_lo, (8, SSD_HEADS))], axis=0)
    wide = jnp.dot(small, expand, preferred_element_type=f32)
    dinx, eacsx = wide[:L], wide[L:2 * L]
    cdec = wide[2 * L:2 * L + 1] + wide[2 * L + 8:2 * L + 9]
    xs_b = xs.astype(bf16)
    xdec_b = (xs * dinx).astype(bf16)
    lane = lax.broadcasted_iota(jnp.int32, (L, LANES), 1)
    ys = []
    for g in range(SSD_GROUPS):
        bg = bm[:, g * SSD_STATE:(g + 1) * SSD_STATE]
        cg = cm[:, g * SSD_STATE:(g + 1) * SSD_STATE]
        gs = slice(g * GROUP_COLS, (g + 1) * GROUP_COLS)
        cb = lax.dot_general(cg, bg, NT, preferred_element_type=f32)
        h_in = state_ref[:, gs]
        y_off = jnp.dot(cg, h_in.astype(bf16), preferred_element_type=f32) * eacsx[:, gs]
        st = lax.dot_general(bg, xdec_b[:, gs], TN, preferred_element_type=f32)
        state_ref[:, gs] = h_in * cdec[:, gs] + st
        tiles = []
        for pair in range(GROUP_COLS // LANES):
            ms = []
            for sub in range(2):
                h = g * (SSD_HEADS // SSD_GROUPS) + pair * 2 + sub
                seg = acs[:, h:h + 1] - acst[h:h + 1, :]
                ms.append((jnp.where(causal, jnp.exp(seg), 0.0) * (cb * dtt[h:h + 1, :])).astype(bf16))
            col = g * GROUP_COLS + pair * LANES
            xp = xs_b[:, col:col + LANES]
            rhs = jnp.concatenate([jnp.where(lane < SSD_HEAD_DIM, xp, jnp.zeros_like(xp)),
                                   jnp.where(lane >= SSD_HEAD_DIM, xp, jnp.zeros_like(xp))], axis=0)
            tiles.append(jnp.dot(jnp.concatenate(ms, axis=1), rhs, preferred_element_type=f32))
        ys.append(jnp.concatenate(tiles, axis=1) + y_off)
    return jnp.concatenate(ys, axis=1)


def _ssd_kernel(xf_ref, bf_ref, cf_ref, dtf_ref, dttf_ref,
                xb_ref, bb_ref, cb_ref, dtb_ref, dttb_ref,
                bias_ref, biast_ref, alog_ref, alogt_ref, expand_ref,
                yf_ref, yb_ref, sf_ref, sb_ref):
    @pl.when(pl.program_id(0) == 0)
    def _():
        sf_ref[...] = jnp.zeros_like(sf_ref)
        sb_ref[...] = jnp.zeros_like(sb_ref)

    H = SSD_HEADS
    expand = expand_ref[...]
    yf_ref[...] = _ssd_direction(
        xf_ref[...].astype(f32), bf_ref[...], cf_ref[...],
        dtf_ref[:, :H], dttf_ref[:H, :], bias_ref[:, :H], biast_ref[:H, :],
        alog_ref[:, :H], alogt_ref[:H, :], expand, sf_ref, reverse=False).astype(yf_ref.dtype)
    yb_ref[...] = _ssd_direction(
        xb_ref[...].astype(f32), bb_ref[...], cb_ref[...],
        dtb_ref[:, H:], dttb_ref[H:, :], bias_ref[:, H:], biast_ref[H:, :],
        alog_ref[:, H:], alogt_ref[H:, :], expand, sb_ref, reverse=True).astype(yb_ref.dtype)


def _ssd_scan(xbc, dt, dtt, dt_bias, a_log):
    s = xbc.shape[0]
    nc = s // CHUNK
    gn = SSD_GROUPS * SSD_STATE
    bcol = D_SSD // gn
    fwd = lambda c: c
    bwd = lambda c: nc - 1 - c

    def chunk_specs(sel):
        return [pl.BlockSpec((CHUNK, D_SSD), lambda c: (sel(c), 0)),
                pl.BlockSpec((CHUNK, gn), lambda c: (sel(c), bcol)),
                pl.BlockSpec((CHUNK, gn), lambda c: (sel(c), bcol + 1)),
                pl.BlockSpec((CHUNK, 2 * SSD_HEADS), lambda c: (sel(c), 0)),
                pl.BlockSpec((2 * SSD_HEADS, CHUNK), lambda c: (0, sel(c)))]

    const = lambda shape: pl.BlockSpec(shape, lambda c: (0, 0))
    expand = (jnp.arange(D_SSD)[None, :] // SSD_HEAD_DIM == jnp.arange(SSD_HEADS)[:, None]).astype(bf16)
    return pl.pallas_call(
        _ssd_kernel,
        grid=(nc,),
        in_specs=chunk_specs(fwd) + chunk_specs(bwd) + [
            const((1, 2 * SSD_HEADS)), const((2 * SSD_HEADS, 1)),
            const((1, 2 * SSD_HEADS)), const((2 * SSD_HEADS, 1)),
            const((SSD_HEADS, D_SSD))],
        out_specs=[pl.BlockSpec((CHUNK, D_SSD), lambda c: (fwd(c), 0)),
                   pl.BlockSpec((CHUNK, D_SSD), lambda c: (bwd(c), 0))],
        out_shape=[jax.ShapeDtypeStruct((s, D_SSD), bf16)] * 2,
        scratch_shapes=[pltpu.VMEM((SSD_STATE, D_SSD), f32)] * 2,
        compiler_params=_params("arbitrary"),
        name="ssd_scan",
    )(xbc, xbc, xbc, dt, dtt, xbc, xbc, xbc, dt, dtt,
      dt_bias.reshape(1, -1), dt_bias.reshape(-1, 1), a_log.reshape(1, -1), a_log.reshape(-1, 1),
      expand)


def _ssd_post_kernel(yf_ref, yb_ref, xs_ref, z_ref, d_ref, g_ref, o_ref):
    z = z_ref[...].astype(f32)
    y = (yf_ref[...].astype(f32) + yb_ref[...].astype(f32)
         + d_ref[...] * xs_ref[...].astype(f32)) * (z * _sigmoid(z))
    ms = jnp.mean(y * y, axis=-1, keepdims=True)
    o_ref[...] = (y * lax.rsqrt(ms + EPS) * g_ref[...]).astype(o_ref.dtype)


def _ssd_post(y_f, y_b, xbc, zx, d_skip, g, tb=512):
    s = y_f.shape[0]
    tb = min(tb, s)
    row = pl.BlockSpec((tb, D_SSD), lambda i: (i, 0))
    vec = pl.BlockSpec((1, D_SSD), lambda i: (0, 0))
    return pl.pallas_call(
        _ssd_post_kernel,
        grid=(s // tb,),
        in_specs=[row, row, row, row, vec, vec],
        out_specs=row,
        out_shape=jax.ShapeDtypeStruct((s, D_SSD), bf16),
        compiler_params=_params("parallel"),
        name="ssd_post",
    )(y_f, y_b, xbc, zx, jnp.repeat(d_skip, SSD_HEAD_DIM).reshape(1, D_SSD), g.reshape(1, D_SSD))


def _rope_table_kernel(pos_ref, inv_ref, sign_ref, cos_ref, sin_ref):
    ang = pos_ref[...].astype(f32) * inv_ref[...]
    cos_ref[...] = jnp.cos(ang)
    sin_ref[...] = jnp.sin(ang) * sign_ref[...]


def _rope_tables(positions, tb=512):
    s = positions.shape[0]
    tb = min(tb, s)
    half = ATT_HEAD_DIM // 2
    inv = ROPE_THETA ** (-jnp.arange(0, ATT_HEAD_DIM, 2, dtype=f32) / ATT_HEAD_DIM)
    inv_t = jnp.tile(inv, LANES // half).reshape(1, LANES)
    sign = jnp.tile(jnp.concatenate([-jnp.ones((half,), f32), jnp.ones((half,), f32)]),
                    LANES // ATT_HEAD_DIM).reshape(1, LANES)
    vec = pl.BlockSpec((1, LANES), lambda i: (0, 0))
    row = pl.BlockSpec((tb, LANES), lambda i: (i, 0))
    return pl.pallas_call(
        _rope_table_kernel,
        grid=(s // tb,),
        in_specs=[pl.BlockSpec((tb, 1), lambda i: (i, 0)), vec, vec],
        out_specs=[row, row],
        out_shape=[jax.ShapeDtypeStruct((s, LANES), f32)] * 2,
        compiler_params=_params("parallel"),
        name="rope_tables",
    )(positions.reshape(s, 1), inv_t, sign)


def _qk_kernel(x_ref, g_ref, cos_ref, sin_ref, o_ref):
    which = pl.program_id(1)
    scale = jnp.where(which == 0, ATT_HEAD_DIM ** -0.5 * LOG2E, 1.0).astype(f32)
    g = g_ref[0] * scale
    cos = cos_ref[...]
    sin = sin_ref[...]
    r = lax.broadcasted_iota(jnp.int32, (LANES, LANES), 0) // ATT_HEAD_DIM
    c = lax.broadcasted_iota(jnp.int32, (LANES, LANES), 1) // ATT_HEAD_DIM
    seg = jnp.where(r == c, 1.0, 0.0).astype(bf16)
    lane = lax.broadcasted_iota(jnp.int32, cos.shape, 1)
    first_half = (lane % ATT_HEAD_DIM) < ATT_HEAD_DIM // 2
    for t in range(x_ref.shape[1] // LANES):
        x = x_ref[:, t * LANES:(t + 1) * LANES].astype(f32)
        sq = x * x
        hi = sq.astype(bf16)
        lo = (sq - hi.astype(f32)).astype(bf16)
        ss = (jnp.dot(hi, seg, preferred_element_type=f32)
              + jnp.dot(lo, seg, preferred_element_type=f32))
        xn = x * lax.rsqrt(ss * (1.0 / ATT_HEAD_DIM) + EPS) * g
        swapped = jnp.where(first_half,
                            pltpu.roll(xn, LANES - ATT_HEAD_DIM // 2, axis=1),
                            pltpu.roll(xn, ATT_HEAD_DIM // 2, axis=1))
        o_ref[:, t * LANES:(t + 1) * LANES] = (xn * cos + swapped * sin).astype(o_ref.dtype)


def _qk_prep(qkvg, q_g, k_g, cos_t, sin_t, tb=512):
    s = qkvg.shape[0]
    tb = min(tb, s)
    cols = 2 * ATT_HEADS * ATT_HEAD_DIM
    g2 = jnp.stack([jnp.tile(q_g, 2), jnp.tile(k_g, 2)]).reshape(2, 1, LANES)
    tab = pl.BlockSpec((tb, LANES), lambda i, j: (i, 0))
    return pl.pallas_call(
        _qk_kernel,
        grid=(s // tb, 2),
        in_specs=[pl.BlockSpec((tb, cols), lambda i, j: (i, j)),
                  pl.BlockSpec((1, 1, LANES), lambda i, j: (j, 0, 0)),
                  tab, tab],
        out_specs=pl.BlockSpec((tb, cols), lambda i, j: (i, j)),
        out_shape=jax.ShapeDtypeStruct((s, 2 * cols), bf16),
        compiler_params=_params("parallel", "parallel"),
        name="qk_prep",
    )(qkvg, g2, cos_t, sin_t)


def _attn_kernel(q_ref, k_ref, v_ref, lq1_ref, lk1_ref, lq2_ref, lk2_ref, g_ref, o_ref,
                 qm_ref, s_ref, p_ref, m_ref, l_ref, acc_ref, *, tk, strip, lam_init):
    tq = q_ref.shape[0]
    nk = k_ref.shape[0] // tk
    q = q_ref[...]
    lane = lax.broadcasted_iota(jnp.int32, q.shape, 1)
    zero = jnp.zeros_like(q)
    qm_ref[0] = jnp.where(lane < ATT_HEAD_DIM, q, zero)
    qm_ref[1] = jnp.where(lane >= ATT_HEAD_DIM, q, zero)
    m_ref[...] = jnp.full(m_ref.shape, -1e30, f32)
    l_ref[...] = jnp.zeros_like(l_ref)
    acc_ref[...] = jnp.zeros_like(acc_ref)

    def scores(c, slot):
        kc = k_ref[c * tk:(c + 1) * tk, :]
        for half in range(2):
            s_ref[slot, half] = lax.dot_general(qm_ref[half], kc, NT, preferred_element_type=f32)

    def absorb(c, slot):
        vc = v_ref[c * tk:(c + 1) * tk, :]
        for half in range(2):
            for r0 in range(0, tq, strip):
                rows = slice(r0, r0 + strip)
                s = s_ref[slot, half, rows, :]
                mx = s[:, :LANES]
                for t in range(1, tk // LANES):
                    mx = jnp.maximum(mx, s[:, t * LANES:(t + 1) * LANES])
                m_old = m_ref[half, rows, :]
                m_new = jnp.maximum(m_old, jnp.broadcast_to(jnp.max(mx, axis=-1, keepdims=True),
                                                            m_old.shape))
                alpha = jnp.exp2(m_old - m_new)
                shifted = jnp.concatenate([s[:, t * LANES:(t + 1) * LANES] - m_new
                                           for t in range(tk // LANES)], axis=1)
                p = jnp.exp2(shifted.astype(bf16))
                ps = p[:, :LANES]
                for t in range(1, tk // LANES):
                    ps = ps + p[:, t * LANES:(t + 1) * LANES]
                l_ref[half, rows, :] = alpha * l_ref[half, rows, :] + ps.astype(f32)
                acc_ref[half, rows, :] = alpha * acc_ref[half, rows, :]
                m_ref[half, rows, :] = m_new
                p_ref[half, rows, :] = p
            acc_ref[half] += jnp.dot(p_ref[half], vc, preferred_element_type=f32)

    scores(0, 0)
    for c in range(nk):
        if c + 1 < nk:
            scores(c + 1, (c + 1) % 2)
        absorb(c, c % 2)
    lam = (jnp.exp(jnp.sum(lq1_ref[...] * lk1_ref[...], axis=-1, keepdims=True))
           - jnp.exp(jnp.sum(lq2_ref[...] * lk2_ref[...], axis=-1, keepdims=True)) + lam_init)
    o1 = acc_ref[0] / jnp.sum(l_ref[0], axis=-1, keepdims=True)
    o2 = acc_ref[1] / jnp.sum(l_ref[1], axis=-1, keepdims=True)
    o = o1 - lam * o2
    ms = jnp.mean(o * o, axis=-1, keepdims=True)
    o_ref[...] = (o * lax.rsqrt(ms + EPS) * (g_ref[...] * (1.0 - lam_init))).astype(o_ref.dtype)


def _diff_attention(qk, qkvg, lam_q1, lam_k1, lam_q2, lam_k2, subln_g, lam_init,
                    tq=512, tk=512, strip=64):
    s = qk.shape[0]
    tq = min(tq, s)
    tk = min(tk, s)
    assert s % tk == 0 and tq % strip == 0
    kcol = 2 * ATT_HEADS * ATT_HEAD_DIM // LANES
    vcol = 2 * kcol
    vec = lambda n: pl.BlockSpec((1, n), lambda h, i: (0, 0))
    return pl.pallas_call(
        functools.partial(_attn_kernel, tk=tk, strip=strip, lam_init=lam_init),
        grid=(ATT_HEADS, s // tq),
        in_specs=[pl.BlockSpec((tq, LANES), lambda h, i: (i, h)),
                  pl.BlockSpec((s, LANES), lambda h, i: (0, kcol + h)),
                  pl.BlockSpec((s, LANES), lambda h, i: (0, vcol + h)),
                  vec(ATT_HEAD_DIM), vec(ATT_HEAD_DIM), vec(ATT_HEAD_DIM), vec(ATT_HEAD_DIM),
                  vec(ATT_V_DIM)],
        out_specs=pl.BlockSpec((tq, LANES), lambda h, i: (i, h)),
        out_shape=jax.ShapeDtypeStruct((s, ATT_HEADS * ATT_V_DIM), bf16),
        scratch_shapes=[pltpu.VMEM((2, tq, LANES), bf16),
                        pltpu.VMEM((2, 2, tq, tk), f32),
                        pltpu.VMEM((2, tq, tk), bf16),
                        pltpu.VMEM((2, tq, LANES), f32),
                        pltpu.VMEM((2, tq, LANES), f32),
                        pltpu.VMEM((2, tq, ATT_V_DIM), f32)],
        compiler_params=_params("parallel", "arbitrary"),
        name="diff_attention",
    )(qk, qk, qkvg, lam_q1.reshape(1, -1), lam_k1.reshape(1, -1), lam_q2.reshape(1, -1),
      lam_k2.reshape(1, -1), subln_g.reshape(1, -1))


def _merge_kernel(ys_ref, ya_ref, wa_ref, wb_ref, gs_ref, ga_ref, o_ref):
    a = jnp.dot(ys_ref[...], wa_ref[...].astype(bf16), preferred_element_type=f32)
    b = jnp.dot(ya_ref[...], wb_ref[...].astype(bf16), preferred_element_type=f32)
    mixed = _sigmoid(gs_ref[...].astype(f32)) * a + _sigmoid(ga_ref[...].astype(f32)) * b
    o_ref[...] = mixed.astype(o_ref.dtype)


def _merge(y_ssd, y_att, w_a, w_b, qkvg, tm=2048, tn=256):
    s = y_ssd.shape[0]
    tm = min(tm, s)
    gcol = 3 * 2 * ATT_HEADS * ATT_HEAD_DIM // tn
    nj = D_MODEL // tn
    row = pl.BlockSpec((tm, D_MODEL), lambda i, j: (i, 0))
    wcol = pl.BlockSpec((D_MODEL, tn), lambda i, j: (0, j))
    return pl.pallas_call(
        _merge_kernel,
        grid=(s // tm, nj),
        in_specs=[row, row, wcol, wcol,
                  pl.BlockSpec((tm, tn), lambda i, j: (i, gcol + j)),
                  pl.BlockSpec((tm, tn), lambda i, j: (i, gcol + nj + j))],
        out_specs=pl.BlockSpec((tm, tn), lambda i, j: (i, j)),
        out_shape=jax.ShapeDtypeStruct((s, D_MODEL), bf16),
        compiler_params=_params("parallel", "arbitrary"),
        name="merge",
    )(y_ssd, y_att, w_a, w_b, qkvg, qkvg)


def _take_topk(s, k, rank):
    big = jnp.float32(2 ** 30)
    vals, ranks = [], []
    for _ in range(k):
        m = jnp.max(s, axis=0, keepdims=True)
        pos = jnp.min(jnp.where(s == m, rank, big), axis=0, keepdims=True)
        vals.append(m)
        ranks.append(pos)
        s = jnp.where(rank == pos, -jnp.inf, s)
    return jnp.concatenate(vals, axis=0), jnp.concatenate(ranks, axis=0)


def _pair_candidates(v1, v2):
    k = PEER_TOPK
    tb = v1.shape[1]
    row8 = lax.broadcasted_iota(jnp.int32, (8, tb), 0)
    neg = jnp.float32(-jnp.inf)
    sums, flat = [], []
    for a in range(k // 2):
        nb = k // (a + 1)
        for b0 in range(0, nb, 8):
            piece = v1[a:a + 1] + v2[b0:b0 + 8]
            if nb - b0 < 8:
                piece = jnp.where(row8 < nb - b0, piece, neg)
            sums.append(piece)
            flat.append((row8 + (a * k + b0)).astype(f32))
    sums.append(v1[k // 2:] + v2[0:1])
    flat.append(((row8 + k // 2) * k).astype(f32))
    cat = lambda xs: jnp.concatenate(xs, axis=0)
    return cat(sums), cat(flat)


def _pick_rows(table, idx):
    out = jnp.zeros_like(idx)
    for r in range(table.shape[0]):
        out = jnp.where(idx == float(r), table[r:r + 1], out)
    return out


def _topk_kernel(q_ref, keys_ref, i1_ref, i2_ref, gate_ref):
    tb = q_ref.shape[0]
    kk = PEER_TOPK
    key_iota = lax.broadcasted_iota(jnp.int32, (PEER_NKEYS, tb), 0).astype(f32)
    i1s, i2s, gates = [], [], []
    for h in range(PEER_HEADS):
        tops = []
        for half in range(2):
            c0 = (h * 2 + half) * PEER_NKEYS
            qh = q_ref[:, c0:c0 + PEER_NKEYS].astype(bf16)
            keys = keys_ref[h, half].astype(bf16)
            s = lax.dot_general(keys, qh, NT, preferred_element_type=f32)
            tops.append(_take_topk(s, kk, key_iota))
        (v1, i1), (v2, i2) = tops
        cand, flat = _pair_candidates(v1, v2)
        sc, pos = _take_topk(cand, kk, flat)
        a = jnp.floor(pos * (1.0 / kk))
        e1 = _pick_rows(i1, a)
        e2 = _pick_rows(i2, pos - a * kk)
        e = jnp.exp(sc - jnp.max(sc, axis=0, keepdims=True))
        gates.append(e / jnp.sum(e, axis=0, keepdims=True))
        i1s.append(e1)
        i2s.append(e2)
    i1_ref[...] = jnp.concatenate(i1s, axis=0).T
    i2_ref[...] = jnp.concatenate(i2s, axis=0).T
    gate_ref[...] = jnp.concatenate(gates, axis=0).T


def _peer_topk(qp, keys, tb=256):
    s = qp.shape[0]
    tb = min(tb, s)
    nsel = PEER_HEADS * PEER_TOPK
    row = pl.BlockSpec((tb, nsel), lambda i: (i, 0))
    return pl.pallas_call(
        _topk_kernel,
        grid=(s // tb,),
        in_specs=[pl.BlockSpec((tb, qp.shape[1]), lambda i: (i, 0)),
                  pl.BlockSpec(keys.shape, lambda i: (0, 0, 0, 0))],
        out_specs=[row, row, row],
        out_shape=[jax.ShapeDtypeStruct((s, nsel), f32)] * 3,
        compiler_params=_params("parallel"),
        name="peer_topk",
    )(qp, keys)


ROUTE_PAD = 8


def _route_kernel(i1_ref, i2_ref, gate_ref, w_ref, tile_ref):
    n = PEER_NKEYS
    tb = i1_ref.shape[0]
    pitch = tb + ROUTE_PAD
    sub = lax.broadcasted_iota(jnp.int32, (n, i1_ref.shape[1]), 0).astype(f32)

    def body(t, carry):
        i1 = i1_ref[pl.ds(t, 1), :]
        i2 = i2_ref[pl.ds(t, 1), :]
        g = gate_ref[pl.ds(t, 1), :]
        a = jnp.where(sub == i1, g, 0.0).astype(bf16)
        b = jnp.where(sub == i2, 1.0, 0.0).astype(bf16)
        tile_ref[pl.ds(t, n, stride=pitch), :] = lax.dot_general(
            a, b, NT, preferred_element_type=f32)
        return carry

    lax.fori_loop(0, tb, body, 0, unroll=64)
    for a in range(n):
        w_ref[:, a * n:(a + 1) * n] = tile_ref[a * pitch:a * pitch + tb, :].astype(w_ref.dtype)


def _peer_route(i1, i2, gate, tb=256):
    s, nsel = i1.shape
    tb = min(tb, s)
    row = pl.BlockSpec((tb, nsel), lambda i: (i, 0))
    return pl.pallas_call(
        _route_kernel,
        grid=(s // tb,),
        in_specs=[row, row, row],
        out_specs=pl.BlockSpec((tb, PEER_EXPERTS), lambda i: (i, 0)),
        out_shape=jax.ShapeDtypeStruct((s, PEER_EXPERTS), bf16),
        scratch_shapes=[pltpu.VMEM((PEER_NKEYS * (tb + ROUTE_PAD), PEER_NKEYS), f32)],
        compiler_params=_params("parallel"),
        name="peer_route",
    )(i1, i2, gate)


def _peer_dense_kernel(h_ref, u0_ref, ub_ref, un_ref, v_ref, w_ref, o_ref, a_ref, g_ref, *, strip):
    e = pl.program_id(1)
    te = ub_ref.shape[0]
    h = h_ref[...]

    def mix(slot, lo):
        for r0 in range(0, h_ref.shape[0], strip):
            rows = slice(r0, r0 + strip)
            a = a_ref[slot, rows, :]
            act = 0.5 * a * (1.0 + lax.erf(a * (2.0 ** -0.5)))
            g_ref[slot, rows, :] = (act * w_ref[rows, lo:lo + te].astype(f32)).astype(bf16)
        return jnp.dot(g_ref[slot], v_ref[lo:lo + te, :].astype(bf16), preferred_element_type=f32)

    @pl.when(e == 0)
    def _():
        a_ref[0] = lax.dot_general(h, u0_ref[...].astype(bf16), NT, preferred_element_type=f32)
        o_ref[...] = jnp.zeros_like(o_ref)

    a_ref[1] = lax.dot_general(h, ub_ref[...].astype(bf16), NT, preferred_element_type=f32)
    o_ref[...] += mix(0, 0)
    a_ref[0] = lax.dot_general(h, un_ref[...].astype(bf16), NT,
                               preferred_element_type=f32)
    o_ref[...] += mix(1, te)


def _peer_dense(h, u, v, w, tm=1024, te=256):
    s, d = h.shape
    tm = min(tm, s)
    nt = u.shape[0] // te
    utile = lambda f: pl.BlockSpec((te, d), lambda i, e: (f(e), 0))
    return pl.pallas_call(
        functools.partial(_peer_dense_kernel, strip=min(128, tm)),
        grid=(s // tm, nt // 2),
        in_specs=[pl.BlockSpec((tm, d), lambda i, e: (i, 0)),
                  utile(lambda e: 0),
                  utile(lambda e: 2 * e + 1),
                  utile(lambda e: jnp.minimum(2 * e + 2, nt - 1)),
                  pl.BlockSpec((2 * te, d), lambda i, e: (e, 0)),
                  pl.BlockSpec((tm, 2 * te), lambda i, e: (i, e))],
        out_specs=pl.BlockSpec((tm, d), lambda i, e: (i, 0)),
        out_shape=jax.ShapeDtypeStruct((s, d), f32),
        scratch_shapes=[pltpu.VMEM((2, tm, te), f32), pltpu.VMEM((2, tm, te), bf16)],
        compiler_params=_params("parallel", "arbitrary"),
        name="peer_dense",
    )(h, u, u, u, v, w)


def _ple_kernel(h_ref, wg_ref, p_ref, wp_ref, x_ref, o_ref):
    gate = _sigmoid(jnp.dot(h_ref[...], wg_ref[...].astype(bf16), preferred_element_type=f32))
    up = jnp.dot(p_ref[...].astype(bf16), wp_ref[...].astype(bf16), preferred_element_type=f32)
    o_ref[...] = x_ref[...] + gate * up


def _ple(h, w_gate, p, w_up, x, tm=2048, tn=512):
    s = h.shape[0]
    tm = min(tm, s)
    return pl.pallas_call(
        _ple_kernel,
        grid=(s // tm, D_MODEL // tn),
        in_specs=[pl.BlockSpec((tm, D_MODEL), lambda i, j: (i, 0)),
                  pl.BlockSpec((D_MODEL, tn), lambda i, j: (0, j)),
                  pl.BlockSpec((tm, PLE_DIM), lambda i, j: (i, 0)),
                  pl.BlockSpec((PLE_DIM, tn), lambda i, j: (0, j)),
                  pl.BlockSpec((tm, tn), lambda i, j: (i, j))],
        out_specs=pl.BlockSpec((tm, tn), lambda i, j: (i, j)),
        out_shape=jax.ShapeDtypeStruct((s, D_MODEL), f32),
        compiler_params=_params("parallel", "arbitrary"),
        name="ple",
    )(h, w_gate, p, w_up, x)


def _layer(i, x, p, pos, norm1_g, w_in, conv_w, conv_b, a_log_f, a_log_b, dt_bias_f, dt_bias_b,
           d_skip, ssd_norm_g, q_norm_g, k_norm_g, lam_q1, lam_k1, lam_q2, lam_k2, subln_g,
           w_ssd_br, w_att_br, w_out, norm2_g, peer_wq, peer_keys, peer_u, peer_v, norm3_g,
           ple_gate_w, ple_up_w):
    h = _rms_norm(x, norm1_g)
    w_t = w_in.T
    zx = _matmul_nt(h, w_t, 0, ZX_COLS, bf16)
    dt, dtt = _dt_proj(h, w_t, DT_COL0, 2 * SSD_HEADS)
    qkvg = _matmul_nt(h, w_t, REST_COL0, w_t.shape[0] - REST_COL0, bf16)

    xbc = _conv_silu(zx, conv_w, conv_b)
    y_f, y_b = _ssd_scan(xbc, dt, dtt, jnp.concatenate([dt_bias_f, dt_bias_b]),
                         jnp.concatenate([a_log_f, a_log_b]))
    y_ssd = _ssd_post(y_f, y_b, xbc, zx, d_skip, ssd_norm_g)

    cos_t, sin_t = _rope_tables(pos)
    qk = _qk_prep(qkvg, q_norm_g, k_norm_g, cos_t, sin_t)
    lam_init = 0.8 - 0.6 * math.exp(-0.3 * i)
    y_att = _diff_attention(qk, qkvg, lam_q1, lam_k1, lam_q2, lam_k2, subln_g, lam_init)

    mixed = _merge(y_ssd, y_att, w_ssd_br, w_att_br, qkvg)
    x = _matmul(mixed, w_out, 0, D_MODEL, f32, residual=x)

    h2 = _rms_norm(x, norm2_g)
    qp = _matmul(h2, peer_wq, 0, peer_wq.shape[1], f32)
    i1, i2, gate = _peer_topk(qp, peer_keys)
    w = _peer_route(i1, i2, gate)
    mix = _peer_dense(h2, peer_u, peer_v, w)
    x, h3 = _add_rms_norm(x, mix, norm3_g)

    return _ple(h3, ple_gate_w, p, ple_up_w, x)


def kernel(x, p, positions, norm1_g, w_in, conv_w, conv_b, a_log_f, a_log_b, dt_bias_f, dt_bias_b, d_skip, ssd_norm_g, q_norm_g, k_norm_g, lam_q1, lam_k1, lam_q2, lam_k2, subln_g, w_ssd_br, w_att_br, w_out, norm2_g, peer_wq, peer_keys, peer_u, peer_v, norm3_g, ple_gate_w, ple_up_w):
    batch, seq, d = x.shape
    depth = w_in.shape[0]
    outs = []
    for b in range(batch):
        xb = x[b]
        for i in range(depth):
            xb = _layer(i, xb, p[i, b], positions[b], norm1_g[i], w_in[i], conv_w[i], conv_b[i],
                        a_log_f[i], a_log_b[i], dt_bias_f[i], dt_bias_b[i], d_skip[i],
                        ssd_norm_g[i], q_norm_g[i], k_norm_g[i], lam_q1[i], lam_k1[i], lam_q2[i],
                        lam_k2[i], subln_g[i], w_ssd_br[i], w_att_br[i], w_out[i], norm2_g[i],
                        peer_wq[i], peer_keys[i], peer_u[i], peer_v[i], norm3_g[i],
                        ple_gate_w[i], ple_up_w[i])
        outs.append(xb)
    return jnp.stack(outs)
```

```python
import functools
import math

import jax
import jax.numpy as jnp
from jax import lax
from jax.experimental import pallas as pl
from jax.experimental.pallas import tpu as pltpu

f32 = jnp.float32
bf16 = jnp.bfloat16

D_MODEL = 2048
D_SSD = 2048
SSD_HEAD_DIM = 64
SSD_HEADS = 32
SSD_GROUPS = 4
SSD_STATE = 128
GROUP_COLS = D_SSD // SSD_GROUPS
CONV_K = 5
CONV_CH = 3072
CHUNK = 128
ATT_HEADS = 16
ATT_HEAD_DIM = 64
ATT_V_DIM = 128
ROPE_THETA = 10000.0
PEER_HEADS = 8
PEER_NKEYS = 128
PEER_EXPERTS = PEER_NKEYS * PEER_NKEYS
PEER_TOPK = 16
PLE_DIM = 256
EPS = 1e-6
LOG2E = math.log2(math.e)

ZX_COLS = D_SSD + CONV_CH
DT_COL0 = ZX_COLS
REST_COL0 = ZX_COLS + 2 * SSD_HEADS
LANES = 128
BF16_ROWS = 16

VMEM_LIMIT = 56 * 1024 * 1024

NT = (((1,), (1,)), ((), ()))
TN = (((0,), (0,)), ((), ()))


def _params(*sem):
    return pltpu.CompilerParams(dimension_semantics=sem, vmem_limit_bytes=VMEM_LIMIT)


def _sigmoid(x):
    return 1.0 / (1.0 + jnp.exp(-x))


def _softplus(x):
    return jnp.maximum(x, 0.0) + jnp.log1p(jnp.exp(-jnp.abs(x)))


def _norm_kernel(x_ref, g_ref, h_ref):
    x = x_ref[...]
    ms = jnp.mean(x * x, axis=-1, keepdims=True)
    h_ref[...] = (x * lax.rsqrt(ms + EPS) * g_ref[...]).astype(h_ref.dtype)


def _rms_norm(x, g, tb=512):
    s, d = x.shape
    tb = min(tb, s)
    return pl.pallas_call(
        _norm_kernel,
        grid=(s // tb,),
        in_specs=[pl.BlockSpec((tb, d), lambda i: (i, 0)),
                  pl.BlockSpec((1, d), lambda i: (0, 0))],
        out_specs=pl.BlockSpec((tb, d), lambda i: (i, 0)),
        out_shape=jax.ShapeDtypeStruct((s, d), bf16),
        compiler_params=_params("parallel"),
        name="rms_norm",
    )(x, g.reshape(1, d))


def _add_norm_kernel(x_ref, d_ref, g_ref, xo_ref, h_ref):
    x = x_ref[...] + d_ref[...]
    xo_ref[...] = x
    ms = jnp.mean(x * x, axis=-1, keepdims=True)
    h_ref[...] = (x * lax.rsqrt(ms + EPS) * g_ref[...]).astype(h_ref.dtype)


def _add_rms_norm(x, delta, g, tb=512):
    s, d = x.shape
    tb = min(tb, s)
    row = pl.BlockSpec((tb, d), lambda i: (i, 0))
    return pl.pallas_call(
        _add_norm_kernel,
        grid=(s // tb,),
        in_specs=[row, row, pl.BlockSpec((1, d), lambda i: (0, 0))],
        out_specs=[row, row],
        out_shape=[jax.ShapeDtypeStruct((s, d), f32), jax.ShapeDtypeStruct((s, d), bf16)],
        compiler_params=_params("parallel"),
        name="add_rms_norm",
    )(x, delta, g.reshape(1, d))


def _mm_kernel(h_ref, w_ref, o_ref):
    o_ref[...] = jnp.dot(h_ref[...], w_ref[...].astype(bf16),
                         preferred_element_type=f32).astype(o_ref.dtype)


def _mm_res_kernel(h_ref, w_ref, r_ref, o_ref):
    o_ref[...] = r_ref[...] + jnp.dot(h_ref[...], w_ref[...].astype(bf16),
                                      preferred_element_type=f32)


def _matmul(h, w, col0, n, out_dtype, residual=None, tm=2048, tn=512):
    s, k = h.shape
    tm = min(tm, s)
    assert col0 % tn == 0 and n % tn == 0 and s % tm == 0
    cb = col0 // tn
    in_specs = [pl.BlockSpec((tm, k), lambda i, j: (i, 0)),
                pl.BlockSpec((k, tn), lambda i, j: (0, j + cb))]
    args = [h, w]
    kern = _mm_kernel
    if residual is not None:
        in_specs.append(pl.BlockSpec((tm, tn), lambda i, j: (i, j)))
        args.append(residual)
        kern = _mm_res_kernel
    return pl.pallas_call(
        kern,
        grid=(s // tm, n // tn),
        in_specs=in_specs,
        out_specs=pl.BlockSpec((tm, tn), lambda i, j: (i, j)),
        out_shape=jax.ShapeDtypeStruct((s, n), out_dtype),
        compiler_params=_params("parallel", "arbitrary"),
        name="matmul",
    )(*args)


def _mm_nt_kernel(h_ref, w_ref, o_ref):
    o_ref[...] = lax.dot_general(h_ref[...], w_ref[...].astype(bf16), NT,
                                 preferred_element_type=f32).astype(o_ref.dtype)


def _matmul_nt(h, w_t, row0, n, out_dtype, tm=2048, tn=512):
    s, k = h.shape
    tm = min(tm, s)
    assert row0 % BF16_ROWS == 0 and n % tn == 0 and s % tm == 0 and row0 + n <= w_t.shape[0]
    return pl.pallas_call(
        _mm_nt_kernel,
        grid=(s // tm, n // tn),
        in_specs=[pl.BlockSpec((tm, k), lambda i, j: (i, 0)),
                  pl.BlockSpec((pl.Element(tn), pl.Element(k)),
                               lambda i, j: ((row0 // 8 + j * (tn // 8)) * 8, 0))],
        out_specs=pl.BlockSpec((tm, tn), lambda i, j: (i, j)),
        out_shape=jax.ShapeDtypeStruct((s, n), out_dtype),
        compiler_params=_params("parallel", "arbitrary"),
        name="matmul_nt",
    )(h, w_t)


def _dt_kernel(h_ref, w_ref, dt_ref, dtt_ref):
    h = h_ref[...]
    w = w_ref[...].astype(bf16)
    dt_ref[...] = lax.dot_general(h, w, NT, preferred_element_type=f32)
    dtt_ref[...] = lax.dot_general(w, h, NT, preferred_element_type=f32)


def _dt_proj(h, w_t, row0, n, tm=1024):
    s, k = h.shape
    tm = min(tm, s)
    assert row0 % n == 0
    return pl.pallas_call(
        _dt_kernel,
        grid=(s // tm,),
        in_specs=[pl.BlockSpec((tm, k), lambda i: (i, 0)),
                  pl.BlockSpec((n, k), lambda i: (row0 // n, 0))],
        out_specs=[pl.BlockSpec((tm, n), lambda i: (i, 0)),
                   pl.BlockSpec((n, tm), lambda i: (0, i))],
        out_shape=[jax.ShapeDtypeStruct((s, n), f32), jax.ShapeDtypeStruct((n, s), f32)],
        compiler_params=_params("parallel"),
        name="dt_proj",
    )(h, w_t)


def _conv_kernel(prev_ref, main_ref, next_ref, w_ref, b_ref, o_ref):
    i = pl.program_id(0)
    last = pl.num_programs(0) - 1
    tb = main_ref.shape[0]
    halo = prev_ref.shape[0]
    pv = jnp.where(i > 0, prev_ref[...].astype(f32), 0.0)
    nx = jnp.where(i < last, next_ref[...].astype(f32), 0.0)
    ext = jnp.concatenate([pv, main_ref[...].astype(f32), nx], axis=0)
    acc = b_ref[...] + jnp.zeros((tb, main_ref.shape[1]), f32)
    for k in range(CONV_K):
        off = halo + k - CONV_K // 2
        acc = acc + w_ref[k:k + 1, :] * ext[off:off + tb]
    o_ref[...] = (acc * _sigmoid(acc)).astype(o_ref.dtype)


def _conv_silu(zx, conv_w, conv_b, tb=512, tc=1024):
    s = zx.shape[0]
    tb = min(tb, s)
    halo = BF16_ROWS
    rb = tb // halo
    nhalo = s // halo
    cb = D_SSD // tc
    return pl.pallas_call(
        _conv_kernel,
        grid=(s // tb, CONV_CH // tc),
        in_specs=[
            pl.BlockSpec((halo, tc), lambda i, j: (jnp.maximum(i * rb - 1, 0), j + cb)),
            pl.BlockSpec((tb, tc), lambda i, j: (i, j + cb)),
            pl.BlockSpec((halo, tc), lambda i, j: (jnp.minimum((i + 1) * rb, nhalo - 1), j + cb)),
            pl.BlockSpec((CONV_K, tc), lambda i, j: (0, j)),
            pl.BlockSpec((1, tc), lambda i, j: (0, j)),
        ],
        out_specs=pl.BlockSpec((tb, tc), lambda i, j: (i, j)),
        out_shape=jax.ShapeDtypeStruct((s, CONV_CH), bf16),
        compiler_params=_params("parallel", "parallel"),
        name="conv_silu",
    )(zx, zx, zx, conv_w, conv_b.reshape(1, CONV_CH))


def _ssd_direction(xs, bm, cm, dt_raw, dtt_raw, bias, bias_t, a_log, a_log_t, expand,
                   state_ref, reverse):
    L = CHUNK
    dt = _softplus(dt_raw + bias)
    dtt = _softplus(dtt_raw + bias_t)
    a = dt * (-jnp.exp(a_log))
    at = dtt * (-jnp.exp(a_log_t))
    ri = lax.broadcasted_iota(jnp.int32, (L, L), 0)
    ci = lax.broadcasted_iota(jnp.int32, (L, L), 1)
    causal = (ri <= ci) if reverse else (ri >= ci)
    cum_l = jnp.where(causal, 1.0, 0.0).astype(f32)
    cum_r = jnp.where((ri >= ci) if reverse else (ri <= ci), 1.0, 0.0).astype(f32)
    acs = jnp.dot(cum_l, a, preferred_element_type=f32, precision=lax.Precision.HIGHEST)
    acst = jnp.dot(at, cum_r, preferred_element_type=f32, precision=lax.Precision.HIGHEST)
    edge = 0 if reverse else L - 1
    acs_end = acs[edge:edge + 1, :]
    end_decay = jnp.exp(acs_end)
    end_hi = end_decay.astype(bf16)
    end_lo = (end_decay - end_hi.astype(f32)).astype(bf16)
    small = jnp.concatenate([(dt * jnp.exp(acs_end - acs)).astype(bf16), jnp.exp(acs).astype(bf16),
                             jnp.broadcast_to(end_hi, (8, SSD_HEADS)),
                             jnp.broadcast_to(end_lo, (8, SSD_HEADS))], axis=0)
    wide = jnp.dot(small, expand, preferred_element_type=f32)
    dinx, eacsx = wide[:L], wide[L:2 * L]
    cdec = wide[2 * L:2 * L + 1] + wide[2 * L + 8:2 * L + 9]
    xs_b = xs.astype(bf16)
    xdec_b = (xs * dinx).astype(bf16)
    lane = lax.broadcasted_iota(jnp.int32, (L, LANES), 1)
    ys = []
    for g in range(SSD_GROUPS):
        bg = bm[:, g * SSD_STATE:(g + 1) * SSD_STATE]
        cg = cm[:, g * SSD_STATE:(g + 1) * SSD_STATE]
        gs = slice(g * GROUP_COLS, (g + 1) * GROUP_COLS)
        cb = lax.dot_general(cg, bg, NT, preferred_element_type=f32)
        h_in = state_ref[:, gs]
        y_off = jnp.dot(cg, h_in.astype(bf16), preferred_element_type=f32) * eacsx[:, gs]
        st = lax.dot_general(bg, xdec_b[:, gs], TN, preferred_element_type=f32)
        state_ref[:, gs] = h_in * cdec[:, gs] + st
        tiles = []
        for pair in range(GROUP_COLS // LANES):
            ms = []
            for sub in range(2):
                h = g * (SSD_HEADS // SSD_GROUPS) + pair * 2 + sub
                seg = acs[:, h:h + 1] - acst[h:h + 1, :]
                ms.append((jnp.where(causal, jnp.exp(seg), 0.0) * (cb * dtt[h:h + 1, :])).astype(bf16))
            col = g * GROUP_COLS + pair * LANES
            xp = xs_b[:, col:col + LANES]
            rhs = jnp.concatenate([jnp.where(lane < SSD_HEAD_DIM, xp, jnp.zeros_like(xp)),
                                   jnp.where(lane >= SSD_HEAD_DIM, xp, jnp.zeros_like(xp))], axis=0)
            tiles.append(jnp.dot(jnp.concatenate(ms, axis=1), rhs, preferred_element_type=f32))
        ys.append(jnp.concatenate(tiles, axis=1) + y_off)
    return jnp.concatenate(ys, axis=1)


def _ssd_kernel(xf_ref, bf_ref, cf_ref, dtf_ref, dttf_ref,
                xb_ref, bb_ref, cb_ref, dtb_ref, dttb_ref,
                bias_ref, biast_ref, alog_ref, alogt_ref, expand_ref,
                yf_ref, yb_ref, sf_ref, sb_ref):
    @pl.when(pl.program_id(0) == 0)
    def _():
        sf_ref[...] = jnp.zeros_like(sf_ref)
        sb_ref[...] = jnp.zeros_like(sb_ref)

    H = SSD_HEADS
    expand = expand_ref[...]
    yf_ref[...] = _ssd_direction(
        xf_ref[...].astype(f32), bf_ref[...], cf_ref[...],
        dtf_ref[:, :H], dttf_ref[:H, :], bias_ref[:, :H], biast_ref[:H, :],
        alog_ref[:, :H], alogt_ref[:H, :], expand, sf_ref, reverse=False).astype(yf_ref.dtype)
    yb_ref[...] = _ssd_direction(
        xb_ref[...].astype(f32), bb_ref[...], cb_ref[...],
        dtb_ref[:, H:], dttb_ref[H:, :], bias_ref[:, H:], biast_ref[H:, :],
        alog_ref[:, H:], alogt_ref[H:, :], expand, sb_ref, reverse=True).astype(yb_ref.dtype)


def _ssd_scan(xbc, dt, dtt, dt_bias, a_log):
    s = xbc.shape[0]
    nc = s // CHUNK
    gn = SSD_GROUPS * SSD_STATE
    bcol = D_SSD // gn
    fwd = lambda c: c
    bwd = lambda c: nc - 1 - c

    def chunk_specs(sel):
        return [pl.BlockSpec((CHUNK, D_SSD), lambda c: (sel(c), 0)),
                pl.BlockSpec((CHUNK, gn), lambda c: (sel(c), bcol)),
                pl.BlockSpec((CHUNK, gn), lambda c: (sel(c), bcol + 1)),
                pl.BlockSpec((CHUNK, 2 * SSD_HEADS), lambda c: (sel(c), 0)),
                pl.BlockSpec((2 * SSD_HEADS, CHUNK), lambda c: (0, sel(c)))]

    const = lambda shape: pl.BlockSpec(shape, lambda c: (0, 0))
    expand = (jnp.arange(D_SSD)[None, :] // SSD_HEAD_DIM == jnp.arange(SSD_HEADS)[:, None]).astype(bf16)
    return pl.pallas_call(
        _ssd_kernel,
        grid=(nc,),
        in_specs=chunk_specs(fwd) + chunk_specs(bwd) + [
            const((1, 2 * SSD_HEADS)), const((2 * SSD_HEADS, 1)),
            const((1, 2 * SSD_HEADS)), const((2 * SSD_HEADS, 1)),
            const((SSD_HEADS, D_SSD))],
        out_specs=[pl.BlockSpec((CHUNK, D_SSD), lambda c: (fwd(c), 0)),
                   pl.BlockSpec((CHUNK, D_SSD), lambda c: (bwd(c), 0))],
        out_shape=[jax.ShapeDtypeStruct((s, D_SSD), bf16)] * 2,
        scratch_shapes=[pltpu.VMEM((SSD_STATE, D_SSD), f32)] * 2,
        compiler_params=_params("arbitrary"),
        name="ssd_scan",
    )(xbc, xbc, xbc, dt, dtt, xbc, xbc, xbc, dt, dtt,
      dt_bias.reshape(1, -1), dt_bias.reshape(-1, 1), a_log.reshape(1, -1), a_log.reshape(-1, 1),
      expand)


def _ssd_post_kernel(yf_ref, yb_ref, xs_ref, z_ref, d_ref, g_ref, o_ref):
    z = z_ref[...].astype(f32)
    y = (yf_ref[...].astype(f32) + yb_ref[...].astype(f32)
         + d_ref[...] * xs_ref[...].astype(f32)) * (z * _sigmoid(z))
    ms = jnp.mean(y * y, axis=-1, keepdims=True)
    o_ref[...] = (y * lax.rsqrt(ms + EPS) * g_ref[...]).astype(o_ref.dtype)


def _ssd_post(y_f, y_b, xbc, zx, d_skip, g, tb=512):
    s = y_f.shape[0]
    tb = min(tb, s)
    row = pl.BlockSpec((tb, D_SSD), lambda i: (i, 0))
    vec = pl.BlockSpec((1, D_SSD), lambda i: (0, 0))
    return pl.pallas_call(
        _ssd_post_kernel,
        grid=(s // tb,),
        in_specs=[row, row, row, row, vec, vec],
        out_specs=row,
        out_shape=jax.ShapeDtypeStruct((s, D_SSD), bf16),
        compiler_params=_params("parallel"),
        name="ssd_post",
    )(y_f, y_b, xbc, zx, jnp.repeat(d_skip, SSD_HEAD_DIM).reshape(1, D_SSD), g.reshape(1, D_SSD))


def _rope_table_kernel(pos_ref, inv_ref, sign_ref, cos_ref, sin_ref):
    ang = pos_ref[...].astype(f32) * inv_ref[...]
    cos_ref[...] = jnp.cos(ang)
    sin_ref[...] = jnp.sin(ang) * sign_ref[...]


def _rope_tables(positions, tb=512):
    s = positions.shape[0]
    tb = min(tb, s)
    half = ATT_HEAD_DIM // 2
    inv = ROPE_THETA ** (-jnp.arange(0, ATT_HEAD_DIM, 2, dtype=f32) / ATT_HEAD_DIM)
    inv_t = jnp.tile(inv, LANES // half).reshape(1, LANES)
    sign = jnp.tile(jnp.concatenate([-jnp.ones((half,), f32), jnp.ones((half,), f32)]),
                    LANES // ATT_HEAD_DIM).reshape(1, LANES)
    vec = pl.BlockSpec((1, LANES), lambda i: (0, 0))
    row = pl.BlockSpec((tb, LANES), lambda i: (i, 0))
    return pl.pallas_call(
        _rope_table_kernel,
        grid=(s // tb,),
        in_specs=[pl.BlockSpec((tb, 1), lambda i: (i, 0)), vec, vec],
        out_specs=[row, row],
        out_shape=[jax.ShapeDtypeStruct((s, LANES), f32)] * 2,
        compiler_params=_params("parallel"),
        name="rope_tables",
    )(positions.reshape(s, 1), inv_t, sign)


def _qk_kernel(x_ref, g_ref, cos_ref, sin_ref, o_ref):
    which = pl.program_id(1)
    scale = jnp.where(which == 0, ATT_HEAD_DIM ** -0.5 * LOG2E, 1.0).astype(f32)
    g = g_ref[0] * scale
    cos = cos_ref[...]
    sin = sin_ref[...]
    r = lax.broadcasted_iota(jnp.int32, (LANES, LANES), 0) // ATT_HEAD_DIM
    c = lax.broadcasted_iota(jnp.int32, (LANES, LANES), 1) // ATT_HEAD_DIM
    seg = jnp.where(r == c, 1.0, 0.0).astype(bf16)
    lane = lax.broadcasted_iota(jnp.int32, cos.shape, 1)
    first_half = (lane % ATT_HEAD_DIM) < ATT_HEAD_DIM // 2
    for t in range(x_ref.shape[1] // LANES):
        x = x_ref[:, t * LANES:(t + 1) * LANES].astype(f32)
        sq = x * x
        hi = sq.astype(bf16)
        lo = (sq - hi.astype(f32)).astype(bf16)
        ss = (jnp.dot(hi, seg, preferred_element_type=f32)
              + jnp.dot(lo, seg, preferred_element_type=f32))
        xn = x * lax.rsqrt(ss * (1.0 / ATT_HEAD_DIM) + EPS) * g
        swapped = jnp.where(first_half,
                            pltpu.roll(xn, LANES - ATT_HEAD_DIM // 2, axis=1),
                            pltpu.roll(xn, ATT_HEAD_DIM // 2, axis=1))
        o_ref[:, t * LANES:(t + 1) * LANES] = (xn * cos + swapped * sin).astype(o_ref.dtype)


def _qk_prep(qkvg, q_g, k_g, cos_t, sin_t, tb=512):
    s = qkvg.shape[0]
    tb = min(tb, s)
    cols = 2 * ATT_HEADS * ATT_HEAD_DIM
    g2 = jnp.stack([jnp.tile(q_g, 2), jnp.tile(k_g, 2)]).reshape(2, 1, LANES)
    tab = pl.BlockSpec((tb, LANES), lambda i, j: (i, 0))
    return pl.pallas_call(
        _qk_kernel,
        grid=(s // tb, 2),
        in_specs=[pl.BlockSpec((tb, cols), lambda i, j: (i, j)),
                  pl.BlockSpec((1, 1, LANES), lambda i, j: (j, 0, 0)),
                  tab, tab],
        out_specs=pl.BlockSpec((tb, cols), lambda i, j: (i, j)),
        out_shape=jax.ShapeDtypeStruct((s, 2 * cols), bf16),
        compiler_params=_params("parallel", "parallel"),
        name="qk_prep",
    )(qkvg, g2, cos_t, sin_t)


def _attn_kernel(q_ref, k_ref, v_ref, lq1_ref, lk1_ref, lq2_ref, lk2_ref, g_ref, o_ref,
                 qm_ref, s_ref, m_ref, l_ref, acc_ref, *, tk, lam_init):
    nk = k_ref.shape[0] // tk
    q = q_ref[...]
    lane = lax.broadcasted_iota(jnp.int32, q.shape, 1)
    zero = jnp.zeros_like(q)
    qm_ref[0] = jnp.where(lane < ATT_HEAD_DIM, q, zero)
    qm_ref[1] = jnp.where(lane >= ATT_HEAD_DIM, q, zero)
    m_ref[...] = jnp.full(m_ref.shape, -1e30, f32)
    l_ref[...] = jnp.zeros_like(l_ref)
    acc_ref[...] = jnp.zeros_like(acc_ref)

    def scores(c, slot):
        kc = k_ref[c * tk:(c + 1) * tk, :]
        for half in range(2):
            s_ref[slot, half] = lax.dot_general(qm_ref[half], kc, NT, preferred_element_type=f32)

    def absorb(c, slot):
        vc = v_ref[c * tk:(c + 1) * tk, :]
        for half in range(2):
            s = s_ref[slot, half]
            mx = s[:, :LANES]
            for t in range(1, tk // LANES):
                mx = jnp.maximum(mx, s[:, t * LANES:(t + 1) * LANES])
            m_old = m_ref[half]
            m_new = jnp.maximum(m_old, jnp.broadcast_to(jnp.max(mx, axis=-1, keepdims=True),
                                                        m_old.shape))
            alpha = jnp.exp2(m_old - m_new)
            shifted = jnp.concatenate([s[:, t * LANES:(t + 1) * LANES] - m_new
                                       for t in range(tk // LANES)], axis=1)
            p = jnp.exp2(shifted.astype(bf16))
            ps = p[:, :LANES]
            for t in range(1, tk // LANES):
                ps = ps + p[:, t * LANES:(t + 1) * LANES]
            l_ref[half] = alpha * l_ref[half] + ps.astype(f32)
            m_ref[half] = m_new
            acc_ref[half] = alpha * acc_ref[half] + jnp.dot(p, vc, preferred_element_type=f32)

    scores(0, 0)
    for c in range(nk):
        if c + 1 < nk:
            scores(c + 1, (c + 1) % 2)
        absorb(c, c % 2)
    lam = (jnp.exp(jnp.sum(lq1_ref[...] * lk1_ref[...], axis=-1, keepdims=True))
           - jnp.exp(jnp.sum(lq2_ref[...] * lk2_ref[...], axis=-1, keepdims=True)) + lam_init)
    o1 = acc_ref[0] / jnp.sum(l_ref[0], axis=-1, keepdims=True)
    o2 = acc_ref[1] / jnp.sum(l_ref[1], axis=-1, keepdims=True)
    o = o1 - lam * o2
    ms = jnp.mean(o * o, axis=-1, keepdims=True)
    o_ref[...] = (o * lax.rsqrt(ms + EPS) * (g_ref[...] * (1.0 - lam_init))).astype(o_ref.dtype)


def _diff_attention(qk, qkvg, lam_q1, lam_k1, lam_q2, lam_k2, subln_g, lam_init,
                    tq=512, tk=512):
    s = qk.shape[0]
    tq = min(tq, s)
    tk = min(tk, s)
    assert s % tk == 0 and s % tq == 0
    kcol = 2 * ATT_HEADS * ATT_HEAD_DIM // LANES
    vcol = 2 * kcol
    vec = lambda n: pl.BlockSpec((1, n), lambda h, i: (0, 0))
    return pl.pallas_call(
        functools.partial(_attn_kernel, tk=tk, lam_init=lam_init),
        grid=(ATT_HEADS, s // tq),
        in_specs=[pl.BlockSpec((tq, LANES), lambda h, i: (i, h)),
                  pl.BlockSpec((s, LANES), lambda h, i: (0, kcol + h)),
                  pl.BlockSpec((s, LANES), lambda h, i: (0, vcol + h)),
                  vec(ATT_HEAD_DIM), vec(ATT_HEAD_DIM), vec(ATT_HEAD_DIM), vec(ATT_HEAD_DIM),
                  vec(ATT_V_DIM)],
        out_specs=pl.BlockSpec((tq, LANES), lambda h, i: (i, h)),
        out_shape=jax.ShapeDtypeStruct((s, ATT_HEADS * ATT_V_DIM), bf16),
        scratch_shapes=[pltpu.VMEM((2, tq, LANES), bf16),
                        pltpu.VMEM((2, 2, tq, tk), f32),
                        pltpu.VMEM((2, tq, LANES), f32),
                        pltpu.VMEM((2, tq, LANES), f32),
                        pltpu.VMEM((2, tq, ATT_V_DIM), f32)],
        compiler_params=_params("parallel", "arbitrary"),
        name="diff_attention",
    )(qk, qk, qkvg, lam_q1.reshape(1, -1), lam_k1.reshape(1, -1), lam_q2.reshape(1, -1),
      lam_k2.reshape(1, -1), subln_g.reshape(1, -1))


def _merge_kernel(ys_ref, ya_ref, wa_ref, wb_ref, gs_ref, ga_ref, o_ref):
    a = jnp.dot(ys_ref[...], wa_ref[...].astype(bf16), preferred_element_type=f32)
    b = jnp.dot(ya_ref[...], wb_ref[...].astype(bf16), preferred_element_type=f32)
    mixed = _sigmoid(gs_ref[...].astype(f32)) * a + _sigmoid(ga_ref[...].astype(f32)) * b
    o_ref[...] = mixed.astype(o_ref.dtype)


def _merge(y_ssd, y_att, w_a, w_b, qkvg, tm=2048, tn=256):
    s = y_ssd.shape[0]
    tm = min(tm, s)
    gcol = 3 * 2 * ATT_HEADS * ATT_HEAD_DIM // tn
    nj = D_MODEL // tn
    row = pl.BlockSpec((tm, D_MODEL), lambda i, j: (i, 0))
    wcol = pl.BlockSpec((D_MODEL, tn), lambda i, j: (0, j))
    return pl.pallas_call(
        _merge_kernel,
        grid=(s // tm, nj),
        in_specs=[row, row, wcol, wcol,
                  pl.BlockSpec((tm, tn), lambda i, j: (i, gcol + j)),
                  pl.BlockSpec((tm, tn), lambda i, j: (i, gcol + nj + j))],
        out_specs=pl.BlockSpec((tm, tn), lambda i, j: (i, j)),
        out_shape=jax.ShapeDtypeStruct((s, D_MODEL), bf16),
        compiler_params=_params("parallel", "arbitrary"),
        name="merge",
    )(y_ssd, y_att, w_a, w_b, qkvg, qkvg)


def _take_topk(s, k, rank):
    big = jnp.float32(2 ** 30)
    vals, ranks = [], []
    for _ in range(k):
        m = jnp.max(s, axis=0, keepdims=True)
        pos = jnp.min(jnp.where(s == m, rank, big), axis=0, keepdims=True)
        vals.append(m)
        ranks.append(pos)
        s = jnp.where(rank == pos, -jnp.inf, s)
    return jnp.concatenate(vals, axis=0), jnp.concatenate(ranks, axis=0)


def _pair_candidates(v1, v2):
    k = PEER_TOPK
    tb = v1.shape[1]
    row8 = lax.broadcasted_iota(jnp.int32, (8, tb), 0)
    neg = jnp.float32(-jnp.inf)
    sums, flat = [], []
    for a in range(k // 2):
        nb = k // (a + 1)
        for b0 in range(0, nb, 8):
            piece = v1[a:a + 1] + v2[b0:b0 + 8]
            if nb - b0 < 8:
                piece = jnp.where(row8 < nb - b0, piece, neg)
            sums.append(piece)
            flat.append((row8 + (a * k + b0)).astype(f32))
    sums.append(v1[k // 2:] + v2[0:1])
    flat.append(((row8 + k // 2) * k).astype(f32))
    cat = lambda xs: jnp.concatenate(xs, axis=0)
    return cat(sums), cat(flat)


def _pick_rows(table, idx):
    out = jnp.zeros_like(idx)
    for r in range(table.shape[0]):
        out = jnp.where(idx == float(r), table[r:r + 1], out)
    return out


def _topk_kernel(q_ref, keys_ref, i1_ref, i2_ref, gate_ref):
    tb = q_ref.shape[0]
    kk = PEER_TOPK
    key_iota = lax.broadcasted_iota(jnp.int32, (PEER_NKEYS, tb), 0).astype(f32)
    i1s, i2s, gates = [], [], []
    for h in range(PEER_HEADS):
        tops = []
        for half in range(2):
            c0 = (h * 2 + half) * PEER_NKEYS
            qh = q_ref[:, c0:c0 + PEER_NKEYS].astype(bf16)
            keys = keys_ref[h, half].astype(bf16)
            s = lax.dot_general(keys, qh, NT, preferred_element_type=f32)
            tops.append(_take_topk(s, kk, key_iota))
        (v1, i1), (v2, i2) = tops
        cand, flat = _pair_candidates(v1, v2)
        sc, pos = _take_topk(cand, kk, flat)
        a = jnp.floor(pos * (1.0 / kk))
        e1 = _pick_rows(i1, a)
        e2 = _pick_rows(i2, pos - a * kk)
        e = jnp.exp(sc - jnp.max(sc, axis=0, keepdims=True))
        gates.append(e / jnp.sum(e, axis=0, keepdims=True))
        i1s.append(e1)
        i2s.append(e2)
    i1_ref[...] = jnp.concatenate(i1s, axis=0).T
    i2_ref[...] = jnp.concatenate(i2s, axis=0).T
    gate_ref[...] = jnp.concatenate(gates, axis=0).T


def _peer_topk(qp, keys, tb=256):
    s = qp.shape[0]
    tb = min(tb, s)
    nsel = PEER_HEADS * PEER_TOPK
    row = pl.BlockSpec((tb, nsel), lambda i: (i, 0))
    return pl.pallas_call(
        _topk_kernel,
        grid=(s // tb,),
        in_specs=[pl.BlockSpec((tb, qp.shape[1]), lambda i: (i, 0)),
                  pl.BlockSpec(keys.shape, lambda i: (0, 0, 0, 0))],
        out_specs=[row, row, row],
        out_shape=[jax.ShapeDtypeStruct((s, nsel), f32)] * 3,
        compiler_params=_params("parallel"),
        name="peer_topk",
    )(qp, keys)


ROUTE_PAD = 8


def _route_kernel(i1_ref, i2_ref, gate_ref, w_ref, tile_ref):
    n = PEER_NKEYS
    tb = i1_ref.shape[0]
    pitch = tb + ROUTE_PAD
    sub = lax.broadcasted_iota(jnp.int32, (n, i1_ref.shape[1]), 0).astype(f32)

    def body(t, carry):
        i1 = i1_ref[pl.ds(t, 1), :]
        i2 = i2_ref[pl.ds(t, 1), :]
        g = gate_ref[pl.ds(t, 1), :]
        a = jnp.where(sub == i1, g, 0.0).astype(bf16)
        b = jnp.where(sub == i2, 1.0, 0.0).astype(bf16)
        tile_ref[pl.ds(t, n, stride=pitch), :] = lax.dot_general(
            a, b, NT, preferred_element_type=f32)
        return carry

    lax.fori_loop(0, tb, body, 0, unroll=64)
    for a in range(n):
        w_ref[:, a * n:(a + 1) * n] = tile_ref[a * pitch:a * pitch + tb, :].astype(w_ref.dtype)


def _peer_route(i1, i2, gate, tb=256):
    s, nsel = i1.shape
    tb = min(tb, s)
    row = pl.BlockSpec((tb, nsel), lambda i: (i, 0))
    return pl.pallas_call(
        _route_kernel,
        grid=(s // tb,),
        in_specs=[row, row, row],
        out_specs=pl.BlockSpec((tb, PEER_EXPERTS), lambda i: (i, 0)),
        out_shape=jax.ShapeDtypeStruct((s, PEER_EXPERTS), bf16),
        scratch_shapes=[pltpu.VMEM((PEER_NKEYS * (tb + ROUTE_PAD), PEER_NKEYS), f32)],
        compiler_params=_params("parallel"),
        name="peer_route",
    )(i1, i2, gate)


def _peer_dense_kernel(h_ref, u0_ref, ub_ref, un_ref, v_ref, w_ref, o_ref, a_ref):
    e = pl.program_id(1)
    te = ub_ref.shape[0]
    h = h_ref[...]

    def mix(slot, lo):
        a = a_ref[slot]
        act = 0.5 * a * (1.0 + lax.erf(a * (2.0 ** -0.5)))
        g = (act * w_ref[:, lo:lo + te].astype(f32)).astype(bf16)
        return jnp.dot(g, v_ref[lo:lo + te, :].astype(bf16), preferred_element_type=f32)

    @pl.when(e == 0)
    def _():
        a_ref[0] = lax.dot_general(h, u0_ref[...].astype(bf16), NT, preferred_element_type=f32)
        o_ref[...] = jnp.zeros_like(o_ref)

    a_ref[1] = lax.dot_general(h, ub_ref[...].astype(bf16), NT, preferred_element_type=f32)
    o_ref[...] += mix(0, 0)
    a_ref[0] = lax.dot_general(h, un_ref[...].astype(bf16), NT,
                               preferred_element_type=f32)
    o_ref[...] += mix(1, te)


def _peer_dense(h, u, v, w, tm=1024, te=256):
    s, d = h.shape
    tm = min(tm, s)
    nt = u.shape[0] // te
    utile = lambda f: pl.BlockSpec((te, d), lambda i, e: (f(e), 0))
    return pl.pallas_call(
        _peer_dense_kernel,
        grid=(s // tm, nt // 2),
        in_specs=[pl.BlockSpec((tm, d), lambda i, e: (i, 0)),
                  utile(lambda e: 0),
                  utile(lambda e: 2 * e + 1),
                  utile(lambda e: jnp.minimum(2 * e + 2, nt - 1)),
                  pl.BlockSpec((2 * te, d), lambda i, e: (e, 0)),
                  pl.BlockSpec((tm, 2 * te), lambda i, e: (i, e))],
        out_specs=pl.BlockSpec((tm, d), lambda i, e: (i, 0)),
        out_shape=jax.ShapeDtypeStruct((s, d), f32),
        scratch_shapes=[pltpu.VMEM((2, tm, te), f32)],
        compiler_params=_params("parallel", "arbitrary"),
        name="peer_dense",
    )(h, u, u, u, v, w)


def _ple_kernel(h_ref, wg_ref, p_ref, wp_ref, x_ref, o_ref):
    gate = _sigmoid(jnp.dot(h_ref[...], wg_ref[...].astype(bf16), preferred_element_type=f32))
    up = jnp.dot(p_ref[...].astype(bf16), wp_ref[...].astype(bf16), preferred_element_type=f32)
    o_ref[...] = x_ref[...] + gate * up


def _ple(h, w_gate, p, w_up, x, tm=2048, tn=512):
    s = h.shape[0]
    tm = min(tm, s)
    return pl.pallas_call(
        _ple_kernel,
        grid=(s // tm, D_MODEL // tn),
        in_specs=[pl.BlockSpec((tm, D_MODEL), lambda i, j: (i, 0)),
                  pl.BlockSpec((D_MODEL, tn), lambda i, j: (0, j)),
                  pl.BlockSpec((tm, PLE_DIM), lambda i, j: (i, 0)),
                  pl.BlockSpec((PLE_DIM, tn), lambda i, j: (0, j)),
                  pl.BlockSpec((tm, tn), lambda i, j: (i, j))],
        out_specs=pl.BlockSpec((tm, tn), lambda i, j: (i, j)),
        out_shape=jax.ShapeDtypeStruct((s, D_MODEL), f32),
        compiler_params=_params("parallel", "arbitrary"),
        name="ple",
    )(h, w_gate, p, w_up, x)


def _layer(i, x, p, pos, norm1_g, w_in, conv_w, conv_b, a_log_f, a_log_b, dt_bias_f, dt_bias_b,
           d_skip, ssd_norm_g, q_norm_g, k_norm_g, lam_q1, lam_k1, lam_q2, lam_k2, subln_g,
           w_ssd_br, w_att_br, w_out, norm2_g, peer_wq, peer_keys, peer_u, peer_v, norm3_g,
           ple_gate_w, ple_up_w):
    h = _rms_norm(x, norm1_g)
    w_t = w_in.T
    zx = _matmul_nt(h, w_t, 0, ZX_COLS, bf16)
    dt, dtt = _dt_proj(h, w_t, DT_COL0, 2 * SSD_HEADS)
    qkvg = _matmul_nt(h, w_t, REST_COL0, w_t.shape[0] - REST_COL0, bf16)

    xbc = _conv_silu(zx, conv_w, conv_b)
    y_f, y_b = _ssd_scan(xbc, dt, dtt, jnp.concatenate([dt_bias_f, dt_bias_b]),
                         jnp.concatenate([a_log_f, a_log_b]))
    y_ssd = _ssd_post(y_f, y_b, xbc, zx, d_skip, ssd_norm_g)

    cos_t, sin_t = _rope_tables(pos)
    qk = _qk_prep(qkvg, q_norm_g, k_norm_g, cos_t, sin_t)
    lam_init = 0.8 - 0.6 * math.exp(-0.3 * i)
    y_att = _diff_attention(qk, qkvg, lam_q1, lam_k1, lam_q2, lam_k2, subln_g, lam_init)

    mixed = _merge(y_ssd, y_att, w_ssd_br, w_att_br, qkvg)
    x = _matmul(mixed, w_out, 0, D_MODEL, f32, residual=x)

    h2 = _rms_norm(x, norm2_g)
    qp = _matmul(h2, peer_wq, 0, peer_wq.shape[1], f32)
    i1, i2, gate = _peer_topk(qp, peer_keys)
    w = _peer_route(i1, i2, gate)
    mix = _peer_dense(h2, peer_u, peer_v, w)
    x, h3 = _add_rms_norm(x, mix, norm3_g)

    return _ple(h3, ple_gate_w, p, ple_up_w, x)


def kernel(x, p, positions, norm1_g, w_in, conv_w, conv_b, a_log_f, a_log_b, dt_bias_f, dt_bias_b, d_skip, ssd_norm_g, q_norm_g, k_norm_g, lam_q1, lam_k1, lam_q2, lam_k2, subln_g, w_ssd_br, w_att_br, w_out, norm2_g, peer_wq, peer_keys, peer_u, peer_v, norm3_g, ple_gate_w, ple_up_w):
    batch, seq, d = x.shape
    depth = w_in.shape[0]
    outs = []
    for b in range(batch):
        xb = x[b]
        for i in range(depth):
            xb = _layer(i, xb, p[i, b], positions[b], norm1_g[i], w_in[i], conv_w[i], conv_b[i],
                        a_log_f[i], a_log_b[i], dt_bias_f[i], dt_bias_b[i], d_skip[i],
                        ssd_norm_g[i], q_norm_g[i], k_norm_g[i], lam_q1[i], lam_k1[i], lam_q2[i],
                        lam_k2[i], subln_g[i], w_ssd_br[i], w_att_br[i], w_out[i], norm2_g[i],
                        peer_wq[i], peer_keys[i], peer_u[i], peer_v[i], norm3_g[i],
                        ple_gate_w[i], ple_up_w[i])
        outs.append(xb)
    return jnp.stack(outs)
```

```python
import functools
import math

import jax
import jax.numpy as jnp
from jax import lax
from jax.experimental import pallas as pl
from jax.experimental.pallas import tpu as pltpu

f32 = jnp.float32
bf16 = jnp.bfloat16

D_MODEL = 2048
D_SSD = 2048
SSD_HEAD_DIM = 64
SSD_HEADS = 32
SSD_GROUPS = 4
SSD_STATE = 128
GROUP_COLS = D_SSD // SSD_GROUPS
CONV_K = 5
CONV_CH = 3072
CHUNK = 128
ATT_HEADS = 16
ATT_HEAD_DIM = 64
ATT_V_DIM = 128
ROPE_THETA = 10000.0
PEER_HEADS = 8
PEER_NKEYS = 128
PEER_EXPERTS = PEER_NKEYS * PEER_NKEYS
PEER_TOPK = 16
PLE_DIM = 256
EPS = 1e-6
LOG2E = math.log2(math.e)

ZX_COLS = D_SSD + CONV_CH
DT_COL0 = ZX_COLS
REST_COL0 = ZX_COLS + 2 * SSD_HEADS
LANES = 128
BF16_ROWS = 16

VMEM_LIMIT = 56 * 1024 * 1024

NT = (((1,), (1,)), ((), ()))
TN = (((0,), (0,)), ((), ()))


def _params(*sem):
    return pltpu.CompilerParams(dimension_semantics=sem, vmem_limit_bytes=VMEM_LIMIT)


def _sigmoid(x):
    return 1.0 / (1.0 + jnp.exp(-x))


def _softplus(x):
    return jnp.maximum(x, 0.0) + jnp.log1p(jnp.exp(-jnp.abs(x)))


def _norm_kernel(x_ref, g_ref, h_ref):
    x = x_ref[...]
    ms = jnp.mean(x * x, axis=-1, keepdims=True)
    h_ref[...] = (x * lax.rsqrt(ms + EPS) * g_ref[...]).astype(h_ref.dtype)


def _rms_norm(x, g, tb=512):
    s, d = x.shape
    tb = min(tb, s)
    return pl.pallas_call(
        _norm_kernel,
        grid=(s // tb,),
        in_specs=[pl.BlockSpec((tb, d), lambda i: (i, 0)),
                  pl.BlockSpec((1, d), lambda i: (0, 0))],
        out_specs=pl.BlockSpec((tb, d), lambda i: (i, 0)),
        out_shape=jax.ShapeDtypeStruct((s, d), bf16),
        compiler_params=_params("parallel"),
        name="rms_norm",
    )(x, g.reshape(1, d))


def _add_norm_kernel(x_ref, d_ref, g_ref, xo_ref, h_ref):
    x = x_ref[...] + d_ref[...]
    xo_ref[...] = x
    ms = jnp.mean(x * x, axis=-1, keepdims=True)
    h_ref[...] = (x * lax.rsqrt(ms + EPS) * g_ref[...]).astype(h_ref.dtype)


def _add_rms_norm(x, delta, g, tb=512):
    s, d = x.shape
    tb = min(tb, s)
    row = pl.BlockSpec((tb, d), lambda i: (i, 0))
    return pl.pallas_call(
        _add_norm_kernel,
        grid=(s // tb,),
        in_specs=[row, row, pl.BlockSpec((1, d), lambda i: (0, 0))],
        out_specs=[row, row],
        out_shape=[jax.ShapeDtypeStruct((s, d), f32), jax.ShapeDtypeStruct((s, d), bf16)],
        compiler_params=_params("parallel"),
        name="add_rms_norm",
    )(x, delta, g.reshape(1, d))


def _mm_kernel(h_ref, w_ref, o_ref):
    o_ref[...] = jnp.dot(h_ref[...], w_ref[...].astype(bf16),
                         preferred_element_type=f32).astype(o_ref.dtype)


def _mm_res_kernel(h_ref, w_ref, r_ref, o_ref):
    o_ref[...] = r_ref[...] + jnp.dot(h_ref[...], w_ref[...].astype(bf16),
                                      preferred_element_type=f32)


def _matmul(h, w, col0, n, out_dtype, residual=None, tm=2048, tn=512):
    s, k = h.shape
    tm = min(tm, s)
    assert col0 % tn == 0 and n % tn == 0 and s % tm == 0
    cb = col0 // tn
    in_specs = [pl.BlockSpec((tm, k), lambda i, j: (i, 0)),
                pl.BlockSpec((k, tn), lambda i, j: (0, j + cb))]
    args = [h, w]
    kern = _mm_kernel
    if residual is not None:
        in_specs.append(pl.BlockSpec((tm, tn), lambda i, j: (i, j)))
        args.append(residual)
        kern = _mm_res_kernel
    return pl.pallas_call(
        kern,
        grid=(s // tm, n // tn),
        in_specs=in_specs,
        out_specs=pl.BlockSpec((tm, tn), lambda i, j: (i, j)),
        out_shape=jax.ShapeDtypeStruct((s, n), out_dtype),
        compiler_params=_params("parallel", "arbitrary"),
        name="matmul",
    )(*args)


def _mm_nt_kernel(h_ref, w_ref, o_ref):
    o_ref[...] = lax.dot_general(h_ref[...], w_ref[...].astype(bf16), NT,
                                 preferred_element_type=f32).astype(o_ref.dtype)


def _matmul_nt(h, w_t, row0, n, out_dtype, tm=2048, tn=512):
    s, k = h.shape
    tm = min(tm, s)
    assert row0 % BF16_ROWS == 0 and n % tn == 0 and s % tm == 0 and row0 + n <= w_t.shape[0]
    return pl.pallas_call(
        _mm_nt_kernel,
        grid=(s // tm, n // tn),
        in_specs=[pl.BlockSpec((tm, k), lambda i, j: (i, 0)),
                  pl.BlockSpec((pl.Element(tn), pl.Element(k)),
                               lambda i, j: ((row0 // 8 + j * (tn // 8)) * 8, 0))],
        out_specs=pl.BlockSpec((tm, tn), lambda i, j: (i, j)),
        out_shape=jax.ShapeDtypeStruct((s, n), out_dtype),
        compiler_params=_params("parallel", "arbitrary"),
        name="matmul_nt",
    )(h, w_t)


def _dt_kernel(h_ref, w_ref, dt_ref, dtt_ref):
    h = h_ref[...]
    w = w_ref[...].astype(bf16)
    dt_ref[...] = lax.dot_general(h, w, NT, preferred_element_type=f32)
    dtt_ref[...] = lax.dot_general(w, h, NT, preferred_element_type=f32)


def _dt_proj(h, w_t, row0, n, tm=1024):
    s, k = h.shape
    tm = min(tm, s)
    assert row0 % n == 0
    return pl.pallas_call(
        _dt_kernel,
        grid=(s // tm,),
        in_specs=[pl.BlockSpec((tm, k), lambda i: (i, 0)),
                  pl.BlockSpec((n, k), lambda i: (row0 // n, 0))],
        out_specs=[pl.BlockSpec((tm, n), lambda i: (i, 0)),
                   pl.BlockSpec((n, tm), lambda i: (0, i))],
        out_shape=[jax.ShapeDtypeStruct((s, n), f32), jax.ShapeDtypeStruct((n, s), f32)],
        compiler_params=_params("parallel"),
        name="dt_proj",
    )(h, w_t)


def _conv_kernel(prev_ref, main_ref, next_ref, w_ref, b_ref, o_ref):
    i = pl.program_id(0)
    last = pl.num_programs(0) - 1
    tb = main_ref.shape[0]
    halo = prev_ref.shape[0]
    pv = jnp.where(i > 0, prev_ref[...].astype(f32), 0.0)
    nx = jnp.where(i < last, next_ref[...].astype(f32), 0.0)
    ext = jnp.concatenate([pv, main_ref[...].astype(f32), nx], axis=0)
    acc = b_ref[...] + jnp.zeros((tb, main_ref.shape[1]), f32)
    for k in range(CONV_K):
        off = halo + k - CONV_K // 2
        acc = acc + w_ref[k:k + 1, :] * ext[off:off + tb]
    o_ref[...] = (acc * _sigmoid(acc)).astype(o_ref.dtype)


def _conv_silu(zx, conv_w, conv_b, tb=512, tc=1024):
    s = zx.shape[0]
    tb = min(tb, s)
    halo = BF16_ROWS
    rb = tb // halo
    nhalo = s // halo
    cb = D_SSD // tc
    return pl.pallas_call(
        _conv_kernel,
        grid=(s // tb, CONV_CH // tc),
        in_specs=[
            pl.BlockSpec((halo, tc), lambda i, j: (jnp.maximum(i * rb - 1, 0), j + cb)),
            pl.BlockSpec((tb, tc), lambda i, j: (i, j + cb)),
            pl.BlockSpec((halo, tc), lambda i, j: (jnp.minimum((i + 1) * rb, nhalo - 1), j + cb)),
            pl.BlockSpec((CONV_K, tc), lambda i, j: (0, j)),
            pl.BlockSpec((1, tc), lambda i, j: (0, j)),
        ],
        out_specs=pl.BlockSpec((tb, tc), lambda i, j: (i, j)),
        out_shape=jax.ShapeDtypeStruct((s, CONV_CH), bf16),
        compiler_params=_params("parallel", "parallel"),
        name="conv_silu",
    )(zx, zx, zx, conv_w, conv_b.reshape(1, CONV_CH))


def _ssd_direction(xs, bm, cm, dt_raw, dtt_raw, bias, bias_t, a_log, a_log_t, expand,
                   state_ref, reverse):
    L = CHUNK
    dt = _softplus(dt_raw + bias)
    dtt = _softplus(dtt_raw + bias_t)
    a = dt * (-jnp.exp(a_log))
    at = dtt * (-jnp.exp(a_log_t))
    ri = lax.broadcasted_iota(jnp.int32, (L, L), 0)
    ci = lax.broadcasted_iota(jnp.int32, (L, L), 1)
    causal = (ri <= ci) if reverse else (ri >= ci)
    cum_l = jnp.where(causal, 1.0, 0.0).astype(f32)
    cum_r = jnp.where((ri >= ci) if reverse else (ri <= ci), 1.0, 0.0).astype(f32)
    acs = jnp.dot(cum_l, a, preferred_element_type=f32, precision=lax.Precision.HIGHEST)
    acst = jnp.dot(at, cum_r, preferred_element_type=f32, precision=lax.Precision.HIGHEST)
    edge = 0 if reverse else L - 1
    acs_end = acs[edge:edge + 1, :]
    end_decay = jnp.exp(acs_end)
    end_hi = end_decay.astype(bf16)
    end_lo = (end_decay - end_hi.astype(f32)).astype(bf16)
    small = jnp.concatenate([(dt * jnp.exp(acs_end - acs)).astype(bf16), jnp.exp(acs).astype(bf16),
                             jnp.broadcast_to(end_hi, (8, SSD_HEADS)),
                             jnp.broadcast_to(end_lo, (8, SSD_HEADS))], axis=0)
    wide = jnp.dot(small, expand, preferred_element_type=f32)
    dinx, eacsx = wide[:L], wide[L:2 * L]
    cdec = wide[2 * L:2 * L + 1] + wide[2 * L + 8:2 * L + 9]
    xs_b = xs.astype(bf16)
    xdec_b = (xs * dinx).astype(bf16)
    lane = lax.broadcasted_iota(jnp.int32, (L, LANES), 1)
    ys = []
    for g in range(SSD_GROUPS):
        bg = bm[:, g * SSD_STATE:(g + 1) * SSD_STATE]
        cg = cm[:, g * SSD_STATE:(g + 1) * SSD_STATE]
        gs = slice(g * GROUP_COLS, (g + 1) * GROUP_COLS)
        cb = lax.dot_general(cg, bg, NT, preferred_element_type=f32)
        h_in = state_ref[:, gs]
        y_off = jnp.dot(cg, h_in.astype(bf16), preferred_element_type=f32) * eacsx[:, gs]
        st = lax.dot_general(bg, xdec_b[:, gs], TN, preferred_element_type=f32)
        state_ref[:, gs] = h_in * cdec[:, gs] + st
        tiles = []
        for pair in range(GROUP_COLS // LANES):
            ms = []
            for sub in range(2):
                h = g * (SSD_HEADS // SSD_GROUPS) + pair * 2 + sub
                seg = acs[:, h:h + 1] - acst[h:h + 1, :]
                ms.append((jnp.where(causal, jnp.exp(seg), 0.0) * (cb * dtt[h:h + 1, :])).astype(bf16))
            col = g * GROUP_COLS + pair * LANES
            xp = xs_b[:, col:col + LANES]
            rhs = jnp.concatenate([jnp.where(lane < SSD_HEAD_DIM, xp, jnp.zeros_like(xp)),
                                   jnp.where(lane >= SSD_HEAD_DIM, xp, jnp.zeros_like(xp))], axis=0)
            tiles.append(jnp.dot(jnp.concatenate(ms, axis=1), rhs, preferred_element_type=f32))
        ys.append(jnp.concatenate(tiles, axis=1) + y_off)
    return jnp.concatenate(ys, axis=1)


def _ssd_kernel(xf_ref, bf_ref, cf_ref, dtf_ref, dttf_ref,
                xb_ref, bb_ref, cb_ref, dtb_ref, dttb_ref,
                bias_ref, biast_ref, alog_ref, alogt_ref, expand_ref,
                yf_ref, yb_ref, sf_ref, sb_ref):
    @pl.when(pl.program_id(0) == 0)
    def _():
        sf_ref[...] = jnp.zeros_like(sf_ref)
        sb_ref[...] = jnp.zeros_like(sb_ref)

    H = SSD_HEADS
    expand = expand_ref[...]
    yf_ref[...] = _ssd_direction(
        xf_ref[...].astype(f32), bf_ref[...], cf_ref[...],
        dtf_ref[:, :H], dttf_ref[:H, :], bias_ref[:, :H], biast_ref[:H, :],
        alog_ref[:, :H], alogt_ref[:H, :], expand, sf_ref, reverse=False).astype(yf_ref.dtype)
    yb_ref[...] = _ssd_direction(
        xb_ref[...].astype(f32), bb_ref[...], cb_ref[...],
        dtb_ref[:, H:], dttb_ref[H:, :], bias_ref[:, H:], biast_ref[H:, :],
        alog_ref[:, H:], alogt_ref[H:, :], expand, sb_ref, reverse=True).astype(yb_ref.dtype)


def _ssd_scan(xbc, dt, dtt, dt_bias, a_log):
    s = xbc.shape[0]
    nc = s // CHUNK
    gn = SSD_GROUPS * SSD_STATE
    bcol = D_SSD // gn
    fwd = lambda c: c
    bwd = lambda c: nc - 1 - c

    def chunk_specs(sel):
        return [pl.BlockSpec((CHUNK, D_SSD), lambda c: (sel(c), 0)),
                pl.BlockSpec((CHUNK, gn), lambda c: (sel(c), bcol)),
                pl.BlockSpec((CHUNK, gn), lambda c: (sel(c), bcol + 1)),
                pl.BlockSpec((CHUNK, 2 * SSD_HEADS), lambda c: (sel(c), 0)),
                pl.BlockSpec((2 * SSD_HEADS, CHUNK), lambda c: (0, sel(c)))]

    const = lambda shape: pl.BlockSpec(shape, lambda c: (0, 0))
    expand = (jnp.arange(D_SSD)[None, :] // SSD_HEAD_DIM == jnp.arange(SSD_HEADS)[:, None]).astype(bf16)
    return pl.pallas_call(
        _ssd_kernel,
        grid=(nc,),
        in_specs=chunk_specs(fwd) + chunk_specs(bwd) + [
            const((1, 2 * SSD_HEADS)), const((2 * SSD_HEADS, 1)),
            const((1, 2 * SSD_HEADS)), const((2 * SSD_HEADS, 1)),
            const((SSD_HEADS, D_SSD))],
        out_specs=[pl.BlockSpec((CHUNK, D_SSD), lambda c: (fwd(c), 0)),
                   pl.BlockSpec((CHUNK, D_SSD), lambda c: (bwd(c), 0))],
        out_shape=[jax.ShapeDtypeStruct((s, D_SSD), bf16)] * 2,
        scratch_shapes=[pltpu.VMEM((SSD_STATE, D_SSD), f32)] * 2,
        compiler_params=_params("arbitrary"),
        name="ssd_scan",
    )(xbc, xbc, xbc, dt, dtt, xbc, xbc, xbc, dt, dtt,
      dt_bias.reshape(1, -1), dt_bias.reshape(-1, 1), a_log.reshape(1, -1), a_log.reshape(-1, 1),
      expand)


def _ssd_post_kernel(yf_ref, yb_ref, xs_ref, z_ref, d_ref, g_ref, o_ref):
    z = z_ref[...].astype(f32)
    y = (yf_ref[...].astype(f32) + yb_ref[...].astype(f32)
         + d_ref[...] * xs_ref[...].astype(f32)) * (z * _sigmoid(z))
    ms = jnp.mean(y * y, axis=-1, keepdims=True)
    o_ref[...] = (y * lax.rsqrt(ms + EPS) * g_ref[...]).astype(o_ref.dtype)


def _ssd_post(y_f, y_b, xbc, zx, d_skip, g, tb=512):
    s = y_f.shape[0]
    tb = min(tb, s)
    row = pl.BlockSpec((tb, D_SSD), lambda i: (i, 0))
    vec = pl.BlockSpec((1, D_SSD), lambda i: (0, 0))
    return pl.pallas_call(
        _ssd_post_kernel,
        grid=(s // tb,),
        in_specs=[row, row, row, row, vec, vec],
        out_specs=row,
        out_shape=jax.ShapeDtypeStruct((s, D_SSD), bf16),
        compiler_params=_params("parallel"),
        name="ssd_post",
    )(y_f, y_b, xbc, zx, jnp.repeat(d_skip, SSD_HEAD_DIM).reshape(1, D_SSD), g.reshape(1, D_SSD))


def _rope_table_kernel(pos_ref, inv_ref, sign_ref, cos_ref, sin_ref):
    ang = pos_ref[...].astype(f32) * inv_ref[...]
    cos_ref[...] = jnp.cos(ang)
    sin_ref[...] = jnp.sin(ang) * sign_ref[...]


def _rope_tables(positions, tb=512):
    s = positions.shape[0]
    tb = min(tb, s)
    half = ATT_HEAD_DIM // 2
    inv = ROPE_THETA ** (-jnp.arange(0, ATT_HEAD_DIM, 2, dtype=f32) / ATT_HEAD_DIM)
    inv_t = jnp.tile(inv, LANES // half).reshape(1, LANES)
    sign = jnp.tile(jnp.concatenate([-jnp.ones((half,), f32), jnp.ones((half,), f32)]),
                    LANES // ATT_HEAD_DIM).reshape(1, LANES)
    vec = pl.BlockSpec((1, LANES), lambda i: (0, 0))
    row = pl.BlockSpec((tb, LANES), lambda i: (i, 0))
    return pl.pallas_call(
        _rope_table_kernel,
        grid=(s // tb,),
        in_specs=[pl.BlockSpec((tb, 1), lambda i: (i, 0)), vec, vec],
        out_specs=[row, row],
        out_shape=[jax.ShapeDtypeStruct((s, LANES), f32)] * 2,
        compiler_params=_params("parallel"),
        name="rope_tables",
    )(positions.reshape(s, 1), inv_t, sign)


def _qk_kernel(x_ref, g_ref, cos_ref, sin_ref, o_ref):
    which = pl.program_id(1)
    scale = jnp.where(which == 0, ATT_HEAD_DIM ** -0.5 * LOG2E, 1.0).astype(f32)
    g = g_ref[0] * scale
    cos = cos_ref[...]
    sin = sin_ref[...]
    r = lax.broadcasted_iota(jnp.int32, (LANES, LANES), 0) // ATT_HEAD_DIM
    c = lax.broadcasted_iota(jnp.int32, (LANES, LANES), 1) // ATT_HEAD_DIM
    seg = jnp.where(r == c, 1.0, 0.0).astype(bf16)
    lane = lax.broadcasted_iota(jnp.int32, cos.shape, 1)
    first_half = (lane % ATT_HEAD_DIM) < ATT_HEAD_DIM // 2
    for t in range(x_ref.shape[1] // LANES):
        x = x_ref[:, t * LANES:(t + 1) * LANES].astype(f32)
        sq = x * x
        hi = sq.astype(bf16)
        lo = (sq - hi.astype(f32)).astype(bf16)
        ss = (jnp.dot(hi, seg, preferred_element_type=f32)
              + jnp.dot(lo, seg, preferred_element_type=f32))
        xn = x * lax.rsqrt(ss * (1.0 / ATT_HEAD_DIM) + EPS) * g
        swapped = jnp.where(first_half,
                            pltpu.roll(xn, LANES - ATT_HEAD_DIM // 2, axis=1),
                            pltpu.roll(xn, ATT_HEAD_DIM // 2, axis=1))
        o_ref[:, t * LANES:(t + 1) * LANES] = (xn * cos + swapped * sin).astype(o_ref.dtype)


def _qk_prep(qkvg, q_g, k_g, cos_t, sin_t, tb=512):
    s = qkvg.shape[0]
    tb = min(tb, s)
    cols = 2 * ATT_HEADS * ATT_HEAD_DIM
    g2 = jnp.stack([jnp.tile(q_g, 2), jnp.tile(k_g, 2)]).reshape(2, 1, LANES)
    tab = pl.BlockSpec((tb, LANES), lambda i, j: (i, 0))
    return pl.pallas_call(
        _qk_kernel,
        grid=(s // tb, 2),
        in_specs=[pl.BlockSpec((tb, cols), lambda i, j: (i, j)),
                  pl.BlockSpec((1, 1, LANES), lambda i, j: (j, 0, 0)),
                  tab, tab],
        out_specs=pl.BlockSpec((tb, cols), lambda i, j: (i, j)),
        out_shape=jax.ShapeDtypeStruct((s, 2 * cols), bf16),
        compiler_params=_params("parallel", "parallel"),
        name="qk_prep",
    )(qkvg, g2, cos_t, sin_t)


def _attn_kernel(q_ref, k_ref, v_ref, lq1_ref, lk1_ref, lq2_ref, lk2_ref, g_ref, o_ref,
                 qm_ref, s_ref, m_ref, l_ref, acc_ref, *, tk, lam_init):
    nk = k_ref.shape[0] // tk
    q = q_ref[...]
    lane = lax.broadcasted_iota(jnp.int32, q.shape, 1)
    zero = jnp.zeros_like(q)
    qm_ref[0] = jnp.where(lane < ATT_HEAD_DIM, q, zero)
    qm_ref[1] = jnp.where(lane >= ATT_HEAD_DIM, q, zero)
    m_ref[...] = jnp.full(m_ref.shape, -1e30, f32)
    l_ref[...] = jnp.zeros_like(l_ref)
    acc_ref[...] = jnp.zeros_like(acc_ref)

    def scores(c, slot):
        kc = k_ref[c * tk:(c + 1) * tk, :]
        for half in range(2):
            s_ref[slot, half] = lax.dot_general(qm_ref[half], kc, NT, preferred_element_type=f32)

    def absorb(c, slot):
        vc = v_ref[c * tk:(c + 1) * tk, :]
        for half in range(2):
            s = s_ref[slot, half]
            mx = s[:, :LANES]
            for t in range(1, tk // LANES):
                mx = jnp.maximum(mx, s[:, t * LANES:(t + 1) * LANES])
            m_old = m_ref[half]
            m_new = jnp.maximum(m_old, jnp.broadcast_to(jnp.max(mx, axis=-1, keepdims=True),
                                                        m_old.shape))
            alpha = jnp.exp2(m_old - m_new)
            shifted = jnp.concatenate([s[:, t * LANES:(t + 1) * LANES] - m_new
                                       for t in range(tk // LANES)], axis=1)
            p = jnp.exp2(shifted.astype(bf16))
            ps = p[:, :LANES]
            for t in range(1, tk // LANES):
                ps = ps + p[:, t * LANES:(t + 1) * LANES]
            l_ref[half] = alpha * l_ref[half] + ps.astype(f32)
            m_ref[half] = m_new
            acc_ref[half] = alpha * acc_ref[half] + jnp.dot(p, vc, preferred_element_type=f32)

    scores(0, 0)
    for c in range(nk):
        if c + 1 < nk:
            scores(c + 1, (c + 1) % 2)
        absorb(c, c % 2)
    lam = (jnp.exp(jnp.sum(lq1_ref[...] * lk1_ref[...], axis=-1, keepdims=True))
           - jnp.exp(jnp.sum(lq2_ref[...] * lk2_ref[...], axis=-1, keepdims=True)) + lam_init)
    o1 = acc_ref[0] / jnp.sum(l_ref[0], axis=-1, keepdims=True)
    o2 = acc_ref[1] / jnp.sum(l_ref[1], axis=-1, keepdims=True)
    o = o1 - lam * o2
    ms = jnp.mean(o * o, axis=-1, keepdims=True)
    o_ref[...] = (o * lax.rsqrt(ms + EPS) * (g_ref[...] * (1.0 - lam_init))).astype(o_ref.dtype)


def _diff_attention(qk, qkvg, lam_q1, lam_k1, lam_q2, lam_k2, subln_g, lam_init,
                    tq=512, tk=512):
    s = qk.shape[0]
    tq = min(tq, s)
    tk = min(tk, s)
    assert s % tk == 0 and s % tq == 0
    kcol = 2 * ATT_HEADS * ATT_HEAD_DIM // LANES
    vcol = 2 * kcol
    vec = lambda n: pl.BlockSpec((1, n), lambda h, i: (0, 0))
    return pl.pallas_call(
        functools.partial(_attn_kernel, tk=tk, lam_init=lam_init),
        grid=(ATT_HEADS, s // tq),
        in_specs=[pl.BlockSpec((tq, LANES), lambda h, i: (i, h)),
                  pl.BlockSpec((s, LANES), lambda h, i: (0, kcol + h)),
                  pl.BlockSpec((s, LANES), lambda h, i: (0, vcol + h)),
                  vec(ATT_HEAD_DIM), vec(ATT_HEAD_DIM), vec(ATT_HEAD_DIM), vec(ATT_HEAD_DIM),
                  vec(ATT_V_DIM)],
        out_specs=pl.BlockSpec((tq, LANES), lambda h, i: (i, h)),
        out_shape=jax.ShapeDtypeStruct((s, ATT_HEADS * ATT_V_DIM), bf16),
        scratch_shapes=[pltpu.VMEM((2, tq, LANES), bf16),
                        pltpu.VMEM((2, 2, tq, tk), f32),
                        pltpu.VMEM((2, tq, LANES), f32),
                        pltpu.VMEM((2, tq, LANES), f32),
                        pltpu.VMEM((2, tq, ATT_V_DIM), f32)],
        compiler_params=_params("parallel", "arbitrary"),
        name="diff_attention",
    )(qk, qk, qkvg, lam_q1.reshape(1, -1), lam_k1.reshape(1, -1), lam_q2.reshape(1, -1),
      lam_k2.reshape(1, -1), subln_g.reshape(1, -1))


def _merge_kernel(ys_ref, ya_ref, wa_ref, wb_ref, gs_ref, ga_ref, o_ref):
    a = jnp.dot(ys_ref[...], wa_ref[...].astype(bf16), preferred_element_type=f32)
    b = jnp.dot(ya_ref[...], wb_ref[...].astype(bf16), preferred_element_type=f32)
    mixed = _sigmoid(gs_ref[...].astype(f32)) * a + _sigmoid(ga_ref[...].astype(f32)) * b
    o_ref[...] = mixed.astype(o_ref.dtype)


def _merge(y_ssd, y_att, w_a, w_b, qkvg, tm=2048, tn=256):
    s = y_ssd.shape[0]
    tm = min(tm, s)
    gcol = 3 * 2 * ATT_HEADS * ATT_HEAD_DIM // tn
    nj = D_MODEL // tn
    row = pl.BlockSpec((tm, D_MODEL), lambda i, j: (i, 0))
    wcol = pl.BlockSpec((D_MODEL, tn), lambda i, j: (0, j))
    return pl.pallas_call(
        _merge_kernel,
        grid=(s // tm, nj),
        in_specs=[row, row, wcol, wcol,
                  pl.BlockSpec((tm, tn), lambda i, j: (i, gcol + j)),
                  pl.BlockSpec((tm, tn), lambda i, j: (i, gcol + nj + j))],
        out_specs=pl.BlockSpec((tm, tn), lambda i, j: (i, j)),
        out_shape=jax.ShapeDtypeStruct((s, D_MODEL), bf16),
        compiler_params=_params("parallel", "arbitrary"),
        name="merge",
    )(y_ssd, y_att, w_a, w_b, qkvg, qkvg)


def _take_topk(s, k, rank):
    big = jnp.float32(2 ** 30)
    vals, ranks = [], []
    for _ in range(k):
        m = jnp.max(s, axis=0, keepdims=True)
        pos = jnp.min(jnp.where(s == m, rank, big), axis=0, keepdims=True)
        vals.append(m)
        ranks.append(pos)
        s = jnp.where(rank == pos, -jnp.inf, s)
    return jnp.concatenate(vals, axis=0), jnp.concatenate(ranks, axis=0)


def _pair_candidates(v1, v2):
    k = PEER_TOPK
    tb = v1.shape[1]
    row8 = lax.broadcasted_iota(jnp.int32, (8, tb), 0)
    neg = jnp.float32(-jnp.inf)
    sums, flat, bins = [], [], []
    for a in range(k // 2):
        nb = k // (a + 1)
        assert nb % 8 == 0 or nb < 8
        for b0 in range(0, nb - 7, 8):
            sums.append(v1[a:a + 1] + v2[b0:b0 + 8])
            flat.append((row8 + (a * k + b0)).astype(f32))
        if nb % 8:
            room = [i for i, runs in enumerate(bins) if sum(n for _, n in runs) + nb <= 8]
            if room:
                bins[room[0]].append((a, nb))
            else:
                bins.append([(a, nb)])
    for runs in bins:
        piece = jnp.full((8, tb), neg)
        index = row8.astype(f32) + float(k * k)
        off = 0
        for a, nb in runs:
            here = (row8 >= off) & (row8 < off + nb)
            v2_rows = v2[0:8] if off == 0 else pltpu.roll(v2[0:8], off, axis=0)
            piece = jnp.where(here, v1[a:a + 1] + v2_rows, piece)
            index = jnp.where(here, (row8 + (a * k - off)).astype(f32), index)
            off += nb
        sums.append(piece)
        flat.append(index)
    sums.append(v1[k // 2:] + v2[0:1])
    flat.append(((row8 + k // 2) * k).astype(f32))
    cat = lambda xs: jnp.concatenate(xs, axis=0)
    return cat(sums), cat(flat)


def _pick_rows(table, idx):
    out = jnp.zeros_like(idx)
    for r in range(table.shape[0]):
        out = jnp.where(idx == float(r), table[r:r + 1], out)
    return out


def _topk_kernel(q_ref, keys_ref, i1_ref, i2_ref, gate_ref):
    tb = q_ref.shape[0]
    kk = PEER_TOPK
    key_iota = lax.broadcasted_iota(jnp.int32, (PEER_NKEYS, tb), 0).astype(f32)
    i1s, i2s, gates = [], [], []
    for h in range(PEER_HEADS):
        tops = []
        for half in range(2):
            c0 = (h * 2 + half) * PEER_NKEYS
            qh = q_ref[:, c0:c0 + PEER_NKEYS].astype(bf16)
            keys = keys_ref[h, half].astype(bf16)
            s = lax.dot_general(keys, qh, NT, preferred_element_type=f32)
            tops.append(_take_topk(s, kk, key_iota))
        (v1, i1), (v2, i2) = tops
        cand, flat = _pair_candidates(v1, v2)
        sc, pos = _take_topk(cand, kk, flat)
        a = jnp.floor(pos * (1.0 / kk))
        e1 = _pick_rows(i1, a)
        e2 = _pick_rows(i2, pos - a * kk)
        e = jnp.exp(sc - jnp.max(sc, axis=0, keepdims=True))
        gates.append(e / jnp.sum(e, axis=0, keepdims=True))
        i1s.append(e1)
        i2s.append(e2)
    i1_ref[...] = jnp.concatenate(i1s, axis=0).T
    i2_ref[...] = jnp.concatenate(i2s, axis=0).T
    gate_ref[...] = jnp.concatenate(gates, axis=0).T


def _peer_topk(qp, keys, tb=256):
    s = qp.shape[0]
    tb = min(tb, s)
    nsel = PEER_HEADS * PEER_TOPK
    row = pl.BlockSpec((tb, nsel), lambda i: (i, 0))
    return pl.pallas_call(
        _topk_kernel,
        grid=(s // tb,),
        in_specs=[pl.BlockSpec((tb, qp.shape[1]), lambda i: (i, 0)),
                  pl.BlockSpec(keys.shape, lambda i: (0, 0, 0, 0))],
        out_specs=[row, row, row],
        out_shape=[jax.ShapeDtypeStruct((s, nsel), f32)] * 3,
        compiler_params=_params("parallel"),
        name="peer_topk",
    )(qp, keys)


ROUTE_PAD = 8


def _route_kernel(i1_ref, i2_ref, gate_ref, w_ref, tile_ref):
    n = PEER_NKEYS
    tb = i1_ref.shape[0]
    pitch = tb + ROUTE_PAD
    sub = lax.broadcasted_iota(jnp.int32, (n, i1_ref.shape[1]), 0).astype(f32)

    def body(t, carry):
        i1 = i1_ref[pl.ds(t, 1), :]
        i2 = i2_ref[pl.ds(t, 1), :]
        g = gate_ref[pl.ds(t, 1), :]
        a = jnp.where(sub == i1, g, 0.0).astype(bf16)
        b = jnp.where(sub == i2, 1.0, 0.0).astype(bf16)
        tile_ref[pl.ds(t, n, stride=pitch), :] = lax.dot_general(
            a, b, NT, preferred_element_type=f32)
        return carry

    lax.fori_loop(0, tb, body, 0, unroll=64)
    for a in range(n):
        w_ref[:, a * n:(a + 1) * n] = tile_ref[a * pitch:a * pitch + tb, :].astype(w_ref.dtype)


def _peer_route(i1, i2, gate, tb=256):
    s, nsel = i1.shape
    tb = min(tb, s)
    row = pl.BlockSpec((tb, nsel), lambda i: (i, 0))
    return pl.pallas_call(
        _route_kernel,
        grid=(s // tb,),
        in_specs=[row, row, row],
        out_specs=pl.BlockSpec((tb, PEER_EXPERTS), lambda i: (i, 0)),
        out_shape=jax.ShapeDtypeStruct((s, PEER_EXPERTS), bf16),
        scratch_shapes=[pltpu.VMEM((PEER_NKEYS * (tb + ROUTE_PAD), PEER_NKEYS), f32)],
        compiler_params=_params("parallel"),
        name="peer_route",
    )(i1, i2, gate)


def _peer_dense_kernel(h_ref, u0_ref, ub_ref, un_ref, v_ref, w_ref, o_ref, a_ref):
    e = pl.program_id(1)
    te = ub_ref.shape[0]
    h = h_ref[...]

    def mix(slot, lo):
        a = a_ref[slot]
        act = 0.5 * a * (1.0 + lax.erf(a * (2.0 ** -0.5)))
        g = (act * w_ref[:, lo:lo + te].astype(f32)).astype(bf16)
        return jnp.dot(g, v_ref[lo:lo + te, :].astype(bf16), preferred_element_type=f32)

    @pl.when(e == 0)
    def _():
        a_ref[0] = lax.dot_general(h, u0_ref[...].astype(bf16), NT, preferred_element_type=f32)
        o_ref[...] = jnp.zeros_like(o_ref)

    a_ref[1] = lax.dot_general(h, ub_ref[...].astype(bf16), NT, preferred_element_type=f32)
    o_ref[...] += mix(0, 0)
    a_ref[0] = lax.dot_general(h, un_ref[...].astype(bf16), NT,
                               preferred_element_type=f32)
    o_ref[...] += mix(1, te)


def _peer_dense(h, u, v, w, tm=1024, te=256):
    s, d = h.shape
    tm = min(tm, s)
    nt = u.shape[0] // te
    utile = lambda f: pl.BlockSpec((te, d), lambda i, e: (f(e), 0))
    return pl.pallas_call(
        _peer_dense_kernel,
        grid=(s // tm, nt // 2),
        in_specs=[pl.BlockSpec((tm, d), lambda i, e: (i, 0)),
                  utile(lambda e: 0),
                  utile(lambda e: 2 * e + 1),
                  utile(lambda e: jnp.minimum(2 * e + 2, nt - 1)),
                  pl.BlockSpec((2 * te, d), lambda i, e: (e, 0)),
                  pl.BlockSpec((tm, 2 * te), lambda i, e: (i, e))],
        out_specs=pl.BlockSpec((tm, d), lambda i, e: (i, 0)),
        out_shape=jax.ShapeDtypeStruct((s, d), f32),
        scratch_shapes=[pltpu.VMEM((2, tm, te), f32)],
        compiler_params=_params("parallel", "arbitrary"),
        name="peer_dense",
    )(h, u, u, u, v, w)


def _ple_kernel(h_ref, wg_ref, p_ref, wp_ref, x_ref, o_ref):
    gate = _sigmoid(jnp.dot(h_ref[...], wg_ref[...].astype(bf16), preferred_element_type=f32))
    up = jnp.dot(p_ref[...].astype(bf16), wp_ref[...].astype(bf16), preferred_element_type=f32)
    o_ref[...] = x_ref[...] + gate * up


def _ple(h, w_gate, p, w_up, x, tm=2048, tn=512):
    s = h.shape[0]
    tm = min(tm, s)
    return pl.pallas_call(
        _ple_kernel,
        grid=(s // tm, D_MODEL // tn),
        in_specs=[pl.BlockSpec((tm, D_MODEL), lambda i, j: (i, 0)),
                  pl.BlockSpec((D_MODEL, tn), lambda i, j: (0, j)),
                  pl.BlockSpec((tm, PLE_DIM), lambda i, j: (i, 0)),
                  pl.BlockSpec((PLE_DIM, tn), lambda i, j: (0, j)),
                  pl.BlockSpec((tm, tn), lambda i, j: (i, j))],
        out_specs=pl.BlockSpec((tm, tn), lambda i, j: (i, j)),
        out_shape=jax.ShapeDtypeStruct((s, D_MODEL), f32),
        compiler_params=_params("parallel", "arbitrary"),
        name="ple",
    )(h, w_gate, p, w_up, x)


def _layer(i, x, p, pos, norm1_g, w_in, conv_w, conv_b, a_log_f, a_log_b, dt_bias_f, dt_bias_b,
           d_skip, ssd_norm_g, q_norm_g, k_norm_g, lam_q1, lam_k1, lam_q2, lam_k2, subln_g,
           w_ssd_br, w_att_br, w_out, norm2_g, peer_wq, peer_keys, peer_u, peer_v, norm3_g,
           ple_gate_w, ple_up_w):
    h = _rms_norm(x, norm1_g)
    w_t = w_in.T
    zx = _matmul_nt(h, w_t, 0, ZX_COLS, bf16)
    dt, dtt = _dt_proj(h, w_t, DT_COL0, 2 * SSD_HEADS)
    qkvg = _matmul_nt(h, w_t, REST_COL0, w_t.shape[0] - REST_COL0, bf16)

    xbc = _conv_silu(zx, conv_w, conv_b)
    y_f, y_b = _ssd_scan(xbc, dt, dtt, jnp.concatenate([dt_bias_f, dt_bias_b]),
                         jnp.concatenate([a_log_f, a_log_b]))
    y_ssd = _ssd_post(y_f, y_b, xbc, zx, d_skip, ssd_norm_g)

    cos_t, sin_t = _rope_tables(pos)
    qk = _qk_prep(qkvg, q_norm_g, k_norm_g, cos_t, sin_t)
    lam_init = 0.8 - 0.6 * math.exp(-0.3 * i)
    y_att = _diff_attention(qk, qkvg, lam_q1, lam_k1, lam_q2, lam_k2, subln_g, lam_init)

    mixed = _merge(y_ssd, y_att, w_ssd_br, w_att_br, qkvg)
    x = _matmul(mixed, w_out, 0, D_MODEL, f32, residual=x)

    h2 = _rms_norm(x, norm2_g)
    qp = _matmul(h2, peer_wq, 0, peer_wq.shape[1], f32)
    i1, i2, gate = _peer_topk(qp, peer_keys)
    w = _peer_route(i1, i2, gate)
    mix = _peer_dense(h2, peer_u, peer_v, w)
    x, h3 = _add_rms_norm(x, mix, norm3_g)

    return _ple(h3, ple_gate_w, p, ple_up_w, x)


def kernel(x, p, positions, norm1_g, w_in, conv_w, conv_b, a_log_f, a_log_b, dt_bias_f, dt_bias_b, d_skip, ssd_norm_g, q_norm_g, k_norm_g, lam_q1, lam_k1, lam_q2, lam_k2, subln_g, w_ssd_br, w_att_br, w_out, norm2_g, peer_wq, peer_keys, peer_u, peer_v, norm3_g, ple_gate_w, ple_up_w):
    batch, seq, d = x.shape
    depth = w_in.shape[0]
    outs = []
    for b in range(batch):
        xb = x[b]
        for i in range(depth):
            xb = _layer(i, xb, p[i, b], positions[b], norm1_g[i], w_in[i], conv_w[i], conv_b[i],
                        a_log_f[i], a_log_b[i], dt_bias_f[i], dt_bias_b[i], d_skip[i],
                        ssd_norm_g[i], q_norm_g[i], k_norm_g[i], lam_q1[i], lam_k1[i], lam_q2[i],
                        lam_k2[i], subln_g[i], w_ssd_br[i], w_att_br[i], w_out[i], norm2_g[i],
                        peer_wq[i], peer_keys[i], peer_u[i], peer_v[i], norm3_g[i],
                        ple_gate_w[i], ple_up_w[i])
        outs.append(xb)
    return jnp.stack(outs)
```
